```python
import math
import jax
import jax.numpy as jnp
from jax import lax
import numpy as np

D_MODEL = 1024
BATCH = 1
SEQ = 16384
DEPTH = 2
DEC_BATCH = 8
DEC_SEQ = 64
PAST_LEN = 4096

CHUNK = 64
Q_BLOCK = 128
EPS = 1e-6
NEG_INF = -1e30

A_HEADS = 8
A_HEAD_DIM = 64
A_VDIM = 2 * A_HEAD_DIM
A_QK = A_HEADS * A_VDIM
A_WIDTH = A_HEADS * A_VDIM

B_HEADS = 8
B_DK = 128
B_DV = 128
B_WIDTH = B_HEADS * B_DV
B_QKV = B_HEADS * (2 * B_DK + B_DV)
GDN_CONV = 4
GDN_CHUNK = CHUNK

D_FF = 2816
FFN_CONV = 3

COL_QA = A_QK
COL_KA = COL_QA + A_QK
COL_VA = COL_KA + A_WIDTH
COL_QKVB = COL_VA + B_QKV
COL_BETA = COL_QKVB + B_HEADS
COL_ALPHA = COL_BETA + B_HEADS
COL_Z = COL_ALPHA + B_WIDTH
COL_GA = COL_Z + D_MODEL
IN_COLS = COL_GA + D_MODEL
COL_SPLITS = (COL_QA, COL_KA, COL_VA, COL_QKVB, COL_BETA, COL_ALPHA, COL_Z, COL_GA)

kernel_name = 'hybrid_diffattn_gdn_stream_step'


def rmsnorm(x, g):
    xf = x.astype(jnp.float32)
    y = xf * lax.rsqrt(jnp.mean(xf * xf, axis=-1, keepdims=True) + EPS)
    return (y * g.astype(jnp.float32)).astype(x.dtype)


def l2norm(x):
    xf = x.astype(jnp.float32)
    return xf * lax.rsqrt(jnp.sum(xf * xf, axis=-1, keepdims=True) + EPS)


def lambda_init(layer):
    return 0.8 - 0.6 * math.exp(-0.3 * layer)


def causal_dwconv(x, buf, w, b=None):
    width = w.shape[0]
    t = x.shape[1]
    xp = jnp.concatenate([buf.astype(x.dtype), x], axis=1)
    y = xp[:, 0:t] * w[0]
    for j in range(1, width):
        y = y + xp[:, j:j + t] * w[j]
    if b is not None:
        y = y + b
    return y, xp[:, xp.shape[1] - (width - 1):]


def diff_attention(q, k, v, q_start, lam, lam_init, subln_g):
    bsz, t = q.shape[0], q.shape[1]
    n_keys = k.shape[1]
    qb = min(t, Q_BLOCK)
    nblk = t // qb
    k5 = k.reshape(bsz, n_keys, A_HEADS, 2, A_HEAD_DIM)
    k_pos = jnp.arange(n_keys)
    slopes = 2.0 ** (-8.0 * jnp.arange(1, A_HEADS + 1, dtype=jnp.float32) / A_HEADS)
    scale = A_HEAD_DIM ** -0.5
    qs = q.reshape(bsz, nblk, qb, A_HEADS, 2, A_HEAD_DIM).transpose(1, 0, 2, 3, 4, 5)
    starts = q_start + jnp.arange(nblk) * qb

    def block(args):
        qblk, start = args
        q_pos = start + jnp.arange(qb)
        s = jnp.einsum('bqhmd,bkhmd->bmhqk', qblk, k5).astype(jnp.float32) * scale
        dist = jnp.abs(q_pos[:, None] - k_pos[None, :]).astype(jnp.float32)
        visible = (k_pos[None, :] // CHUNK) <= (q_pos[:, None] // CHUNK)
        s = s - slopes[:, None, None] * dist
        s = jnp.where(visible, s, NEG_INF)
        p = jax.nn.softmax(s, axis=-1)
        p = p[:, 0] - lam * p[:, 1]
        return jnp.einsum('bhqk,bkhe->bqhe', p.astype(v.dtype), v)

    o = lax.map(block, (qs, starts))
    o = o.transpose(1, 0, 2, 3, 4).reshape(bsz, t, A_HEADS, A_VDIM)
    o = rmsnorm(o, subln_g) * (1.0 - lam_init)
    return o.reshape(bsz, t, A_WIDTH)


def gdn_chunk_step(S, inp):
    q, k, v, beta, g = inp
    c = q.shape[2]
    G = jnp.cumsum(g, axis=-1)
    incl = jnp.tril(jnp.ones((c, c), dtype=bool))
    strict = jnp.tril(jnp.ones((c, c), dtype=bool), -1)
    decay = jnp.where(incl, jnp.exp(jnp.where(incl, G[..., :, None] - G[..., None, :], 0.0)), 0.0)
    kb = k * beta[..., None]
    L = jnp.where(strict, jnp.einsum('bhid,bhjd->bhij', kb, k) * decay, 0.0)
    rhs = jnp.concatenate([v * beta[..., None], kb * jnp.exp(G)[..., None]], axis=-1)
    sol = lax.linalg.triangular_solve(L + jnp.eye(c, dtype=jnp.float32), rhs,
                                      left_side=True, lower=True, unit_diagonal=True)
    u, w = sol[..., :B_DV], sol[..., B_DV:]
    v_new = u - jnp.einsum('bhck,bhkv->bhcv', w, S)
    attn = jnp.einsum('bhid,bhjd->bhij', q, k) * decay
    o = (jnp.einsum('bhck,bhkv->bhcv', q * jnp.exp(G)[..., None], S)
         + jnp.einsum('bhij,bhjv->bhiv', attn, v_new))
    g_last = G[..., -1:]
    S_new = (S * jnp.exp(g_last)[..., None]
             + jnp.einsum('bhck,bhcv->bhkv', k * jnp.exp(g_last - G)[..., None], v_new))
    return S_new, o


def gdn_scan(S0, q, k, v, beta, g):
    bsz, t = q.shape[0], q.shape[1]
    c = min(t, GDN_CHUNK)
    n = t // c

    def chunks(a):
        a = a.reshape((bsz, n, c) + a.shape[2:])
        return jnp.swapaxes(jnp.swapaxes(a, 0, 1), 2, 3)

    S, o = lax.scan(gdn_chunk_step, S0, (chunks(q), chunks(k), chunks(v), chunks(beta), chunks(g)))
    o = o.transpose(1, 0, 3, 2, 4).reshape(bsz, t, B_HEADS, B_DV)
    return S, o


def trunk_layer(x, l, k_past, v_past, gdn_state, gdn_buf, ffn_buf, W):
    bsz, t, _ = x.shape
    xn = rmsnorm(x, W['norm_mix_g'][l])
    proj = xn @ W['w_in'][l]
    qa, ka, va, qkv_b, b_raw, a_raw, z, ga, gb = jnp.split(proj, COL_SPLITS, axis=-1)

    qa = qa.reshape(bsz, t, A_HEADS, A_VDIM)
    ka = ka.reshape(bsz, t, A_HEADS, A_VDIM)
    va = va.reshape(bsz, t, A_HEADS, A_VDIM)
    if k_past is None:
        k_all, v_all, past = ka, va, 0
    else:
        k_all = jnp.concatenate([k_past.astype(ka.dtype), ka], axis=1)
        v_all = jnp.concatenate([v_past.astype(va.dtype), va], axis=1)
        past = k_past.shape[1]
    lam0 = lambda_init(l)
    lq1 = W['lambda_q1'][l].astype(jnp.float32)
    lk1 = W['lambda_k1'][l].astype(jnp.float32)
    lq2 = W['lambda_q2'][l].astype(jnp.float32)
    lk2 = W['lambda_k2'][l].astype(jnp.float32)
    lam = jnp.exp(jnp.sum(lq1 * lk1)) - jnp.exp(jnp.sum(lq2 * lk2)) + lam0
    o_a = diff_attention(qa, k_all, v_all, past, lam, lam0, W['subln_g'][l])

    conv_out, new_gdn_buf = causal_dwconv(qkv_b, gdn_buf, W['gdn_conv_w'][l])
    conv_out = jax.nn.silu(conv_out)
    qb_, kb_, vb_ = jnp.split(conv_out, (B_HEADS * B_DK, 2 * B_HEADS * B_DK), axis=-1)
    qb_ = l2norm(qb_.reshape(bsz, t, B_HEADS, B_DK)) * (B_DK ** -0.5)
    kb_ = l2norm(kb_.reshape(bsz, t, B_HEADS, B_DK))
    vb_ = vb_.reshape(bsz, t, B_HEADS, B_DV).astype(jnp.float32)
    beta = jax.nn.sigmoid(b_raw.astype(jnp.float32))
    g = -jnp.exp(W['gdn_a_log'][l].astype(jnp.float32)) * jax.nn.softplus(
        a_raw.astype(jnp.float32) + W['gdn_dt_bias'][l].astype(jnp.float32))
    S_new, o_b = gdn_scan(gdn_state.astype(jnp.float32), qb_, kb_, vb_, beta, g)
    o_b = rmsnorm(o_b, W['gdn_norm_g'][l]) * jax.nn.silu(
        z.reshape(bsz, t, B_HEADS, B_DV).astype(jnp.float32))
    o_b = o_b.astype(x.dtype).reshape(bsz, t, B_WIDTH)

    merged = (jax.nn.sigmoid(ga) * (o_a @ W['w_proj_a'][l])
              + jax.nn.sigmoid(gb) * (o_b @ W['w_proj_b'][l]))
    h = x + merged @ W['w_out'][l]

    hn = rmsnorm(h, W['norm_ffn_g'][l])
    u = hn @ W['w_up'][l]
    u_c, new_ffn_buf = causal_dwconv(u, ffn_buf, W['ffn_conv_w'][l], W['ffn_conv_b'][l])
    gate, val = jnp.split(u_c, 2, axis=-1)
    y = h + (jax.nn.silu(gate) * val) @ W['w_down'][l]
    return y, ka, va, S_new.astype(x.dtype), new_gdn_buf, new_ffn_buf


def setup_inputs(seed: int = 0) -> dict:
    key = jax.random.key(seed)
    ks = jax.random.split(key, 32)
    f32 = jnp.float32

    def nrm(k, shape, scale):
        return jax.random.normal(k, shape, f32) * scale

    dt = jnp.exp(jax.random.uniform(ks[14], (DEPTH, B_HEADS), f32, math.log(1e-3), math.log(1e-1)))
    return {
        'x_prompt': nrm(ks[0], (BATCH, SEQ, D_MODEL), 1.0),
        'x_sample': nrm(ks[1], (DEC_BATCH, DEC_SEQ, D_MODEL), 1.0),
        'cache_k': nrm(ks[2], (DEPTH, DEC_BATCH, PAST_LEN, A_HEADS, A_VDIM), 1.0),
        'cache_v': nrm(ks[3], (DEPTH, DEC_BATCH, PAST_LEN, A_HEADS, A_VDIM), 1.0),
        'state_gdn': nrm(ks[4], (DEPTH, DEC_BATCH, B_HEADS, B_DK, B_DV), 0.1),
        'state_gdn_conv': nrm(ks[5], (DEPTH, DEC_BATCH, GDN_CONV - 1, B_QKV), 1.0),
        'state_ffn_conv': nrm(ks[6], (DEPTH, DEC_BATCH, FFN_CONV - 1, 2 * D_FF), 1.0),
        'norm_mix_g': 1.0 + nrm(ks[7], (DEPTH, D_MODEL), 0.02),
        'w_in': nrm(ks[8], (DEPTH, D_MODEL, IN_COLS), D_MODEL ** -0.5),
        'lambda_q1': nrm(ks[9], (DEPTH, A_HEAD_DIM), 0.1),
        'lambda_k1': nrm(ks[10], (DEPTH, A_HEAD_DIM), 0.1),
        'lambda_q2': nrm(ks[11], (DEPTH, A_HEAD_DIM), 0.1),
        'lambda_k2': nrm(ks[12], (DEPTH, A_HEAD_DIM), 0.1),
        'subln_g': 1.0 + nrm(ks[13], (DEPTH, A_VDIM), 0.02),
        'gdn_conv_w': nrm(ks[15], (DEPTH, GDN_CONV, B_QKV), GDN_CONV ** -0.5),
        'gdn_a_log': jnp.log(jax.random.uniform(ks[16], (DEPTH, B_HEADS), f32, 1.0, 16.0)),
        'gdn_dt_bias': dt + jnp.log(-jnp.expm1(-dt)),
        'gdn_norm_g': 1.0 + nrm(ks[17], (DEPTH, B_DV), 0.02),
        'w_proj_a': nrm(ks[18], (DEPTH, A_WIDTH, D_MODEL), A_WIDTH ** -0.5),
        'w_proj_b': nrm(ks[19], (DEPTH, B_WIDTH, D_MODEL), B_WIDTH ** -0.5),
        'w_out': nrm(ks[20], (DEPTH, D_MODEL, D_MODEL), D_MODEL ** -0.5),
        'norm_ffn_g': 1.0 + nrm(ks[21], (DEPTH, D_MODEL), 0.02),
        'w_up': nrm(ks[22], (DEPTH, D_MODEL, 2 * D_FF), D_MODEL ** -0.5),
        'ffn_conv_w': nrm(ks[23], (DEPTH, FFN_CONV, 2 * D_FF), FFN_CONV ** -0.5),
        'ffn_conv_b': nrm(ks[24], (DEPTH, 2 * D_FF), 0.01),
        'w_down': nrm(ks[25], (DEPTH, D_FF, D_MODEL), D_FF ** -0.5),
        'final_norm_g': 1.0 + nrm(ks[26], (D_MODEL,), 0.02),
    }


def reference(x_prompt, x_sample, cache_k, cache_v, state_gdn, state_gdn_conv, state_ffn_conv,
              norm_mix_g, w_in, lambda_q1, lambda_k1, lambda_q2, lambda_k2, subln_g,
              gdn_conv_w, gdn_a_log, gdn_dt_bias, gdn_norm_g, w_proj_a, w_proj_b, w_out,
              norm_ffn_g, w_up, ffn_conv_w, ffn_conv_b, w_down, final_norm_g):
    W = dict(norm_mix_g=norm_mix_g, w_in=w_in, lambda_q1=lambda_q1, lambda_k1=lambda_k1,
             lambda_q2=lambda_q2, lambda_k2=lambda_k2, subln_g=subln_g, gdn_conv_w=gdn_conv_w,
             gdn_a_log=gdn_a_log, gdn_dt_bias=gdn_dt_bias, gdn_norm_g=gdn_norm_g,
             w_proj_a=w_proj_a, w_proj_b=w_proj_b, w_out=w_out, norm_ffn_g=norm_ffn_g,
             w_up=w_up, ffn_conv_w=ffn_conv_w, ffn_conv_b=ffn_conv_b, w_down=w_down)

    xp = x_prompt
    bp = xp.shape[0]
    kp, vp, sp, gcp, fcp = [], [], [], [], []
    for l in range(DEPTH):
        s0 = jnp.zeros((bp, B_HEADS, B_DK, B_DV), jnp.float32)
        gbuf0 = jnp.zeros((bp, GDN_CONV - 1, B_QKV), xp.dtype)
        fbuf0 = jnp.zeros((bp, FFN_CONV - 1, 2 * D_FF), xp.dtype)
        xp, k_new, v_new, s_new, gc_new, fc_new = trunk_layer(xp, l, None, None, s0, gbuf0, fbuf0, W)
        kp.append(k_new); vp.append(v_new); sp.append(s_new); gcp.append(gc_new); fcp.append(fc_new)
    y_prompt = rmsnorm(xp, final_norm_g)

    xs = x_sample
    ksm, vsm, ssm, gcs, fcs = [], [], [], [], []
    for l in range(DEPTH):
        xs, k_new, v_new, s_new, gc_new, fc_new = trunk_layer(
            xs, l, cache_k[l], cache_v[l], state_gdn[l], state_gdn_conv[l], state_ffn_conv[l], W)
        ksm.append(k_new); vsm.append(v_new); ssm.append(s_new); gcs.append(gc_new); fcs.append(fc_new)
    y_sample = rmsnorm(xs, final_norm_g)

    k_prompt = jnp.stack(kp)
    v_prompt = jnp.stack(vp)
    gdn_prompt = jnp.stack(sp)
    gdn_conv_prompt = jnp.stack(gcp)
    ffn_conv_prompt = jnp.stack(fcp)
    k_sample = jnp.stack(ksm)
    v_sample = jnp.stack(vsm)
    gdn_sample = jnp.stack(ssm)
    gdn_conv_sample = jnp.stack(gcs)
    ffn_conv_sample = jnp.stack(fcs)
    return (y_prompt, y_sample, k_prompt, v_prompt, gdn_prompt, gdn_conv_prompt, ffn_conv_prompt,
            k_sample, v_sample, gdn_sample, gdn_conv_sample, ffn_conv_sample)
```

```python
import functools
import math

import jax
import jax.numpy as jnp
from jax import lax
from jax.experimental import pallas as pl
from jax.experimental.pallas import tpu as pltpu

F32 = jnp.float32
BF16 = jnp.bfloat16

EPS = 1e-6
NEG = -1e30
LOG2E = 1.4426950408889634

CHUNK = 64
A_HEADS = 8
A_HEAD_DIM = 64
A_VDIM = 128
B_HEADS = 8
B_DK = 128
B_DV = 128
GDN_CONV = 4
FFN_CONV = 3
LANES = 128
HIST = 8

VMEM_LIMIT = 56 * 1024 * 1024


def _cparams(n_axes):
    return pltpu.CompilerParams(
        dimension_semantics=("arbitrary",) * n_axes,
        vmem_limit_bytes=VMEM_LIMIT)


def _lambda_init(layer):
    return 0.8 - 0.6 * math.exp(-0.3 * layer)


def _norm_proj_kernel(x_ref, g_ref, w_ref, *out_refs, segs):
    x = x_ref[...]
    ms = jnp.mean(x * x, axis=-1, keepdims=True)
    xn = (x * lax.rsqrt(ms + EPS) * g_ref[...]).astype(BF16)
    k = 0
    for start, width, kinds in segs:
        w = w_ref[:, start:start + width]
        r = None
        for kind in kinds:
            o_ref = out_refs[k]
            k += 1
            if r is None:
                r = jnp.dot(xn, w, preferred_element_type=F32)
            if kind == "t":
                o_ref[...] = r.T.astype(o_ref.dtype)
            else:
                o_ref[...] = r.astype(o_ref.dtype)


def _norm_proj(x, g, w, segs, tm):
    rows, d = x.shape
    assert rows % tm == 0
    out_shape, out_specs, ksegs = [], [], []
    for start, width, kinds in segs:
        kk = []
        for kind in kinds:
            if kind == "t":
                out_shape.append(jax.ShapeDtypeStruct((rows // tm, width, tm), BF16))
                out_specs.append(pl.BlockSpec((None, width, tm), lambda i: (i, 0, 0)))
                kk.append("t")
            else:
                out_shape.append(jax.ShapeDtypeStruct((rows, width), kind))
                out_specs.append(pl.BlockSpec((tm, width), lambda i: (i, 0)))
                kk.append("n")
        ksegs.append((start, width, tuple(kk)))
    return pl.pallas_call(
        functools.partial(_norm_proj_kernel, segs=tuple(ksegs)),
        grid=(rows // tm,),
        in_specs=[pl.BlockSpec((tm, d), lambda i: (i, 0)),
                  pl.BlockSpec((1, d), lambda i: (0, 0)),
                  pl.BlockSpec(w.shape, lambda i: (0, 0))],
        out_specs=out_specs,
        out_shape=out_shape,
        compiler_params=_cparams(1),
        name="norm_proj",
    )(x, g, w)


def _lambda_value(lam_ref, lam0):
    a = jnp.sum(lam_ref[0:1, :] * lam_ref[1:2, :], axis=-1, keepdims=True)
    b = jnp.sum(lam_ref[2:3, :] * lam_ref[3:4, :], axis=-1, keepdims=True)
    return jnp.exp(a) - jnp.exp(b) + lam0


def _stack_maps(q):
    lane = lax.broadcasted_iota(jnp.int32, q.shape, 1)
    zero = jnp.zeros_like(q)
    return jnp.concatenate([jnp.where(lane < A_HEAD_DIM, q, zero),
                            jnp.where(lane >= A_HEAD_DIM, q, zero)], axis=0)


def _attn_prompt_kernel(slopes_ref, lam_ref, g_ref, q_ref, k_ref, vt_ref, o_ref,
                        m_sc, l_sc, acc_sc, *, tq, lam0):
    h = pl.program_id(0)
    i = pl.program_id(1)
    slope = slopes_ref[h]
    qq = _stack_maps(q_ref[...])
    m_sc[...] = jnp.full(m_sc.shape, NEG, F32)
    l_sc[...] = jnp.zeros(l_sc.shape, F32)
    acc_sc[...] = jnp.zeros(acc_sc.shape, F32)
    q0 = i * tq
    key_idx = lax.broadcasted_iota(jnp.int32, (tq, 1), 0)

    def update(s, vtt):
        m_prev = m_sc[...]
        m_new = jnp.maximum(m_prev, jnp.max(s, axis=0, keepdims=True))
        alpha = jnp.exp2(m_prev - m_new)
        p = jnp.exp2(s - m_new)
        l_sc[...] = alpha * l_sc[...] + jnp.sum(p, axis=0, keepdims=True)
        acc_sc[...] = alpha * acc_sc[...] + jnp.dot(
            vtt, p.astype(BF16), preferred_element_type=F32)
        m_sc[...] = m_new

    def scores(k0):
        kt = k_ref[pl.ds(pl.multiple_of(k0, tq), tq), :]
        return lax.dot_general(kt, qq, (((1,), (1,)), ((), ())),
                               preferred_element_type=F32)

    def off_diag(j, carry):
        k0 = j * tq
        bias = slope * (key_idx + (k0 - q0)).astype(F32)
        update(scores(k0) + bias, vt_ref[j])
        return carry

    lax.fori_loop(0, i, off_diag, 0)

    c = lax.broadcasted_iota(jnp.int32, (tq, 2 * tq), 0)
    r = lax.broadcasted_iota(jnp.int32, (tq, 2 * tq), 1)
    r = jnp.where(r >= tq, r - tq, r)
    bias = slope * jnp.minimum(c, 2 * r - c).astype(F32)
    visible = (c // CHUNK) <= (r // CHUNK)
    s = jnp.where(visible, scores(q0) + bias, NEG)
    update(s, vt_ref[i])

    lam = _lambda_value(lam_ref, lam0)
    inv_l = 1.0 / l_sc[...]
    acc = acc_sc[...] * inv_l
    o = acc[:, :tq] - lam * acc[:, tq:]
    ms = jnp.mean(o * o, axis=0, keepdims=True)
    o = (o * lax.rsqrt(ms + EPS)).T
    o_ref[...] = (o * g_ref[...] * (1.0 - lam0)).astype(o_ref.dtype)


def _attn_prompt(q, kb, vt, lam_rows, subln_g, slopes, lam0, tq):
    t = q.shape[0]
    nt = t // tq
    return pl.pallas_call(
        functools.partial(_attn_prompt_kernel, tq=tq, lam0=lam0),
        grid=(A_HEADS, nt),
        in_specs=[pl.BlockSpec(memory_space=pltpu.SMEM),
                  pl.BlockSpec((8, LANES), lambda h, i: (0, 0)),
                  pl.BlockSpec((1, A_VDIM), lambda h, i: (0, 0)),
                  pl.BlockSpec((tq, A_VDIM), lambda h, i: (i, h)),
                  pl.BlockSpec((t, A_VDIM), lambda h, i: (0, h)),
                  pl.BlockSpec((nt, A_VDIM, tq), lambda h, i: (0, h, 0))],
        out_specs=pl.BlockSpec((tq, A_VDIM), lambda h, i: (i, h)),
        out_shape=jax.ShapeDtypeStruct((t, A_HEADS * A_VDIM), BF16),
        scratch_shapes=[pltpu.VMEM((1, 2 * tq), F32),
                        pltpu.VMEM((1, 2 * tq), F32),
                        pltpu.VMEM((A_VDIM, 2 * tq), F32)],
        compiler_params=_cparams(2),
        name="attn_prompt",
    )(slopes, lam_rows, subln_g, q, kb, vt)


def _attn_sample_kernel(slopes_ref, lam_ref, g_ref, q_ref, kn_ref, vn_ref, kc_ref, vc_ref,
                        o_ref, *, past, lam0):
    h = pl.program_id(1)
    slope = slopes_ref[h]
    tq = q_ref.shape[0]
    qq = _stack_maps(q_ref[...])
    dn = (((1,), (1,)), ((), ()))
    s_c = lax.dot_general(qq, kc_ref[...].astype(BF16), dn, preferred_element_type=F32)
    kpos = lax.broadcasted_iota(jnp.int32, (1, past), 1)
    s_c = s_c + slope * (kpos - past).astype(F32)
    s_n = lax.dot_general(qq, kn_ref[...], dn, preferred_element_type=F32)
    r = lax.broadcasted_iota(jnp.int32, (2 * tq, tq), 0)
    c = lax.broadcasted_iota(jnp.int32, (2 * tq, tq), 1)
    r = jnp.where(r >= tq, r - tq, r)
    s_n = s_n + slope * jnp.minimum(c, 2 * r - c).astype(F32)
    m = jnp.maximum(jnp.max(s_c, axis=-1, keepdims=True), jnp.max(s_n, axis=-1, keepdims=True))
    p_c = jnp.exp2(s_c - m)
    p_n = jnp.exp2(s_n - m)
    l = jnp.sum(p_c, axis=-1, keepdims=True) + jnp.sum(p_n, axis=-1, keepdims=True)
    acc = (jnp.dot(p_c.astype(BF16), vc_ref[...].astype(BF16), preferred_element_type=F32)
           + jnp.dot(p_n.astype(BF16), vn_ref[...], preferred_element_type=F32))
    acc = acc * (1.0 / l)
    lam = _lambda_value(lam_ref, lam0)
    o = acc[:tq] - lam * acc[tq:]
    ms = jnp.mean(o * o, axis=-1, keepdims=True)
    o_ref[...] = (o * lax.rsqrt(ms + EPS) * g_ref[...] * (1.0 - lam0)).astype(o_ref.dtype)


def _attn_sample(q, kb, vb, cache_k, cache_v, layer, lam_rows, subln_g, slopes, lam0, tq):
    nb, past = cache_k.shape[1], cache_k.shape[2]
    return pl.pallas_call(
        functools.partial(_attn_sample_kernel, past=past, lam0=lam0),
        grid=(nb, A_HEADS),
        in_specs=[pl.BlockSpec(memory_space=pltpu.SMEM),
                  pl.BlockSpec((8, LANES), lambda b, h: (0, 0)),
                  pl.BlockSpec((1, A_VDIM), lambda b, h: (0, 0)),
                  pl.BlockSpec((tq, A_VDIM), lambda b, h: (b, h)),
                  pl.BlockSpec((tq, A_VDIM), lambda b, h: (b, h)),
                  pl.BlockSpec((tq, A_VDIM), lambda b, h: (b, h)),
                  pl.BlockSpec((None, None, past, A_VDIM), lambda b, h: (layer, b, 0, h)),
                  pl.BlockSpec((None, None, past, A_VDIM), lambda b, h: (layer, b, 0, h))],
        out_specs=pl.BlockSpec((tq, A_VDIM), lambda b, h: (b, h)),
        out_shape=jax.ShapeDtypeStruct((nb * tq, A_HEADS * A_VDIM), BF16),
        compiler_params=_cparams(2),
        name="attn_sample",
    )(slopes, lam_rows, subln_g, q, kb, vb, cache_k, cache_v)


_HI = lax.Precision.HIGHEST


def _mm(a, b, precision=_HI):
    return jnp.dot(a, b, precision=precision, preferred_element_type=F32)


def _mm_nt(a, b, precision=_HI):
    return lax.dot_general(a, b, (((1,), (1,)), ((), ())), precision=precision,
                           preferred_element_type=F32)


def _inverse_masks(ri, ci):
    n = ri.shape[0]
    masks = [(ri // 2) == (ci // 2)]
    b = 2
    while b < n:
        same_big = ((ri // (2 * b)) == (ci // (2 * b))).astype(jnp.int32)
        same_small = ((ri // b) == (ci // b)).astype(jnp.int32)
        masks.append((same_big - same_small) > 0)
        b *= 2
    return masks


def _unit_lower_inverse(low, masks, eye):
    x = eye - jnp.where(masks[0], low, 0.0)
    for mask in masks[1:]:
        e = jnp.where(mask, low, 0.0)
        x = x - _mm(x, _mm(e, x))
    return x


def _gdn_kernel(x_ref, buf_ref, ba_ref, z_ref, s0_ref, cw_ref, par_ref, ng_ref,
                o_ref, s_out_ref, buf_out_ref, xp_sc, s_sc):
    c_idx = pl.program_id(1)
    nc = pl.num_programs(1)
    C = x_ref.shape[0]
    nq = B_HEADS * B_DK

    @pl.when(c_idx == 0)
    def _():
        xp_sc[HIST - (GDN_CONV - 1):HIST, :] = buf_ref[...]
        s_sc[...] = s0_ref[...].astype(F32)

    xp_sc[HIST:HIST + C, :] = x_ref[...]
    conv = xp_sc[HIST:HIST + C, :] * cw_ref[GDN_CONV - 1:GDN_CONV, :]
    for j in range(GDN_CONV - 1):
        off = HIST - (GDN_CONV - 1) + j
        conv = conv + xp_sc[off:off + C, :] * cw_ref[j:j + 1, :]
    conv = conv * jax.nn.sigmoid(conv)
    tail = xp_sc[C:C + HIST, :]
    xp_sc[0:HIST, :] = tail
    buf_out_ref[...] = tail[HIST - (GDN_CONV - 1):HIST, :]

    ba = ba_ref[...]
    lane = lax.broadcasted_iota(jnp.int32, ba.shape, 1)
    beta_all = jax.nn.sigmoid(ba)
    g_all = jnp.where((lane >= B_HEADS) & (lane < 2 * B_HEADS),
                      -jnp.exp(par_ref[0:1, :]) * jax.nn.softplus(ba + par_ref[1:2, :]), 0.0)
    ri = lax.broadcasted_iota(jnp.int32, (C, C), 0)
    ci = lax.broadcasted_iota(jnp.int32, (C, C), 1)
    incl = ri >= ci
    strict = ri > ci
    eye = jnp.where(ri == ci, 1.0, 0.0).astype(F32)
    tri = jnp.where(incl, 1.0, 0.0).astype(F32)
    ones = jnp.ones((C, C), F32)
    eye_k = jnp.where(lax.broadcasted_iota(jnp.int32, (B_DK, B_DK), 0)
                      == lax.broadcasted_iota(jnp.int32, (B_DK, B_DK), 1), 1.0, 0.0).astype(F32)
    gcum_all = _mm(tri, g_all)
    inv_masks = _inverse_masks(ri, ci)

    for h in range(B_HEADS):
        qh = conv[:, h * B_DK:(h + 1) * B_DK]
        kh = conv[:, nq + h * B_DK:nq + (h + 1) * B_DK]
        vh = conv[:, 2 * nq + h * B_DV:2 * nq + (h + 1) * B_DV]
        qh = qh * lax.rsqrt(jnp.sum(qh * qh, axis=-1, keepdims=True) + EPS) * (B_DK ** -0.5)
        kh = kh * lax.rsqrt(jnp.sum(kh * kh, axis=-1, keepdims=True) + EPS)
        beta = beta_all[:, h:h + 1]
        gc = gcum_all[:, B_HEADS + h:B_HEADS + h + 1]
        g_row = _mm(ones, jnp.where(ri == ci, gc, 0.0))
        decay = jnp.where(incl, jnp.exp(jnp.where(incl, gc - g_row, 0.0)), 0.0)
        kb = kh * beta
        low = jnp.where(strict, _mm_nt(kb, kh) * decay, 0.0)
        tinv = _unit_lower_inverse(low, inv_masks, eye)
        eg = jnp.exp(gc)
        sol = _mm(tinv, jnp.concatenate([vh * beta, kb * eg], axis=-1))
        u, w = sol[:, :B_DV], sol[:, B_DV:]
        attn = _mm_nt(qh, kh) * decay
        g_last = gc[C - 1:C, :]
        kd = kh * jnp.exp(g_last - gc)
        kd_t = _mm_nt(eye_k, kd)
        s = s_sc[h]
        v_new = u - _mm(w, s)
        o = _mm(qh * eg, s) + _mm(attn, v_new)
        s_sc[h] = s * jnp.exp(g_last) + _mm(kd_t, v_new)
        zh = z_ref[:, h * B_DV:(h + 1) * B_DV]
        on = o * lax.rsqrt(jnp.mean(o * o, axis=-1, keepdims=True) + EPS) * ng_ref[...]
        o_ref[:, h * B_DV:(h + 1) * B_DV] = (on * (zh * jax.nn.sigmoid(zh))).astype(o_ref.dtype)

    @pl.when(c_idx == nc - 1)
    def _():
        s_out_ref[...] = s_sc[...].astype(s_out_ref.dtype)


def _gdn(qkv, ba, z, s0, buf, conv_w, par, norm_g, n_seq, rows_per_seq):
    nc = rows_per_seq // CHUNK
    nqkv = qkv.shape[1]
    row = lambda b, c: (b * nc + c, 0)
    return pl.pallas_call(
        _gdn_kernel,
        grid=(n_seq, nc),
        in_specs=[pl.BlockSpec((CHUNK, nqkv), row),
                  pl.BlockSpec((None, GDN_CONV - 1, nqkv), lambda b, c: (b, 0, 0)),
                  pl.BlockSpec((CHUNK, LANES), row),
                  pl.BlockSpec((CHUNK, B_HEADS * B_DV), row),
                  pl.BlockSpec((None, B_HEADS, B_DK, B_DV), lambda b, c: (b, 0, 0, 0)),
                  pl.BlockSpec((GDN_CONV, nqkv), lambda b, c: (0, 0)),
                  pl.BlockSpec((8, LANES), lambda b, c: (0, 0)),
                  pl.BlockSpec((1, B_DV), lambda b, c: (0, 0))],
        out_specs=[pl.BlockSpec((CHUNK, B_HEADS * B_DV), row),
                   pl.BlockSpec((None, B_HEADS, B_DK, B_DV), lambda b, c: (b, 0, 0, 0)),
                   pl.BlockSpec((None, GDN_CONV - 1, nqkv), lambda b, c: (b, 0, 0))],
        out_shape=[jax.ShapeDtypeStruct((n_seq * rows_per_seq, B_HEADS * B_DV), BF16),
                   jax.ShapeDtypeStruct((n_seq, B_HEADS, B_DK, B_DV), F32),
                   jax.ShapeDtypeStruct((n_seq, GDN_CONV - 1, nqkv), F32)],
        scratch_shapes=[pltpu.VMEM((HIST + CHUNK, nqkv), F32),
                        pltpu.VMEM((B_HEADS, B_DK, B_DV), F32)],
        compiler_params=_cparams(2),
        name="gdn",
    )(qkv, buf, ba, z, s0, conv_w, par, norm_g)


def _merge_kernel(x_ref, oa_ref, ob_ref, ga_ref, gb_ref, wa_ref, wb_ref, wo_ref, h_ref):
    a = jnp.dot(oa_ref[...], wa_ref[...], preferred_element_type=F32)
    b = jnp.dot(ob_ref[...], wb_ref[...], preferred_element_type=F32)
    merged = jax.nn.sigmoid(ga_ref[...]) * a + jax.nn.sigmoid(gb_ref[...]) * b
    h_ref[...] = x_ref[...] + jnp.dot(merged.astype(BF16), wo_ref[...],
                                      preferred_element_type=F32)


def _merge(x, oa, ob, ga, gb, wa, wb, wo, tm):
    rows, d = x.shape
    rowspec = pl.BlockSpec((tm, d), lambda i: (i, 0))
    wspec = pl.BlockSpec((d, d), lambda i: (0, 0))
    return pl.pallas_call(
        _merge_kernel,
        grid=(rows // tm,),
        in_specs=[rowspec] * 5 + [wspec] * 3,
        out_specs=rowspec,
        out_shape=jax.ShapeDtypeStruct((rows, d), F32),
        compiler_params=_cparams(1),
        name="merge",
    )(x, oa, ob, ga, gb, wa, wb, wo)


def _ffn_kernel(h_ref, g_ref, buf_g_ref, buf_v_ref, wg_ref, wv_ref, cwg_ref, cwv_ref,
                cbg_ref, cbv_ref, wd_ref, fg_ref, y_ref, nbuf_g_ref, nbuf_v_ref,
                hn_sc, ug_sc, uv_sc, acc_sc, *, tiles_per_seq, final):
    i = pl.program_id(0)
    j = pl.program_id(1)
    nj = pl.num_programs(1)
    tm = h_ref.shape[0]
    first = (i % tiles_per_seq) == 0
    nh = FFN_CONV - 1

    @pl.when(j == 0)
    def _():
        x = h_ref[...]
        ms = jnp.mean(x * x, axis=-1, keepdims=True)
        hn_sc[...] = (x * lax.rsqrt(ms + EPS) * g_ref[...]).astype(BF16)
        acc_sc[...] = jnp.zeros(acc_sc.shape, F32)

    def conv_half(u_sc, w_ref, cw_ref, cb_ref, buf_ref, nbuf_ref):
        @pl.when(first)
        def _():
            u_sc[j, HIST - nh:HIST, :] = buf_ref[...]
        u_sc[j, HIST:HIST + tm, :] = jnp.dot(hn_sc[...], w_ref[...], preferred_element_type=F32)
        out = u_sc[j, HIST:HIST + tm, :] * cw_ref[nh:nh + 1, :] + cb_ref[...]
        for t in range(nh):
            off = HIST - nh + t
            out = out + u_sc[j, off:off + tm, :] * cw_ref[t:t + 1, :]
        tail = u_sc[j, tm:tm + HIST, :]
        u_sc[j, 0:HIST, :] = tail
        nbuf_ref[j] = tail[HIST - nh:HIST, :]
        return out

    gate = conv_half(ug_sc, wg_ref, cwg_ref, cbg_ref, buf_g_ref, nbuf_g_ref)
    val = conv_half(uv_sc, wv_ref, cwv_ref, cbv_ref, buf_v_ref, nbuf_v_ref)
    act = (gate * jax.nn.sigmoid(gate) * val).astype(BF16)
    acc_sc[...] += jnp.dot(act, wd_ref[...], preferred_element_type=F32)

    @pl.when(j == nj - 1)
    def _():
        y = h_ref[...] + acc_sc[...]
        if final:
            ms = jnp.mean(y * y, axis=-1, keepdims=True)
            y = y * lax.rsqrt(ms + EPS) * fg_ref[...]
        y_ref[...] = y


def _ffn(h, g, buf, w_up, conv_w, conv_b, w_down, final_g, n_seq, rows_per_seq, tm, cw, final):
    rows, d = h.shape
    d_ff = w_down.shape[0]
    assert d_ff % cw == 0 and rows_per_seq % tm == 0
    nj = d_ff // cw
    tps = rows_per_seq // tm
    nh = FFN_CONV - 1
    gate_col = lambda i, j: (0, j)
    val_col = lambda i, j: (0, nj + j)
    seq_gate = lambda i, j: (i // tps, 0, j)
    seq_val = lambda i, j: (i // tps, 0, nj + j)
    y, nbg, nbv = pl.pallas_call(
        functools.partial(_ffn_kernel, tiles_per_seq=tps, final=final),
        grid=(rows // tm, nj),
        in_specs=[pl.BlockSpec((tm, d), lambda i, j: (i, 0)),
                  pl.BlockSpec((1, d), lambda i, j: (0, 0)),
                  pl.BlockSpec((None, nh, cw), seq_gate),
                  pl.BlockSpec((None, nh, cw), seq_val),
                  pl.BlockSpec((d, cw), gate_col),
                  pl.BlockSpec((d, cw), val_col),
                  pl.BlockSpec((FFN_CONV, cw), gate_col),
                  pl.BlockSpec((FFN_CONV, cw), val_col),
                  pl.BlockSpec((1, cw), gate_col),
                  pl.BlockSpec((1, cw), val_col),
                  pl.BlockSpec((cw, d), lambda i, j: (j, 0)),
                  pl.BlockSpec((1, d), lambda i, j: (0, 0))],
        out_specs=[pl.BlockSpec((tm, d), lambda i, j: (i, 0)),
                   pl.BlockSpec((None, nj, nh, cw), lambda i, j: (i // tps, 0, 0, 0)),
                   pl.BlockSpec((None, nj, nh, cw), lambda i, j: (i // tps, 0, 0, 0))],
        out_shape=[jax.ShapeDtypeStruct((rows, d), F32),
                   jax.ShapeDtypeStruct((n_seq, nj, nh, cw), F32),
                   jax.ShapeDtypeStruct((n_seq, nj, nh, cw), F32)],
        scratch_shapes=[pltpu.VMEM((tm, d), BF16),
                        pltpu.VMEM((nj, HIST + tm, cw), F32),
                        pltpu.VMEM((nj, HIST + tm, cw), F32),
                        pltpu.VMEM((tm, d), F32)],
        compiler_params=_cparams(2),
        name="ffn",
    )(h, g, buf, buf, w_up, w_up, conv_w, conv_w, conv_b, conv_b, w_down, final_g)
    unchunk = lambda a: a.transpose(0, 2, 1, 3).reshape(n_seq, nh, d_ff)
    return y, jnp.concatenate([unchunk(nbg), unchunk(nbv)], axis=-1)


def _prep_layer_weights(l, P):
    w_in = P["w_in"][l]
    d = w_in.shape[0]
    aw = A_HEADS * A_VDIM
    nqkv = B_HEADS * (2 * B_DK + B_DV)
    c_qkvb = 3 * aw
    c_beta = c_qkvb + nqkv
    c_z = c_beta + 2 * B_HEADS
    bw = B_HEADS * B_DV
    w_q = w_in[:, :aw] * (A_HEAD_DIM ** -0.5 * LOG2E)
    w_a = jnp.concatenate([w_q, w_in[:, aw:3 * aw]], axis=1).astype(BF16)
    w_b = jnp.concatenate([w_in[:, c_qkvb:c_beta + 2 * B_HEADS],
                           jnp.zeros((d, LANES - 2 * B_HEADS), F32)], axis=1).astype(BF16)
    w_c = w_in[:, c_z:c_z + bw + 2 * d].astype(BF16)
    lam_rows = jnp.zeros((8, LANES), F32)
    for r, name in enumerate(("lambda_q1", "lambda_k1", "lambda_q2", "lambda_k2")):
        lam_rows = lam_rows.at[r, :A_HEAD_DIM].set(P[name][l])
    par = jnp.zeros((8, LANES), F32)
    par = par.at[0, B_HEADS:2 * B_HEADS].set(P["gdn_a_log"][l])
    par = par.at[1, B_HEADS:2 * B_HEADS].set(P["gdn_dt_bias"][l])
    return dict(
        w_a=w_a, w_b=w_b, w_c=w_c, lam_rows=lam_rows, par=par,
        norm_mix_g=P["norm_mix_g"][l][None, :],
        subln_g=P["subln_g"][l][None, :],
        gdn_conv_w=P["gdn_conv_w"][l],
        gdn_norm_g=P["gdn_norm_g"][l][None, :],
        w_proj_a=P["w_proj_a"][l].astype(BF16),
        w_proj_b=P["w_proj_b"][l].astype(BF16),
        w_out=P["w_out"][l].astype(BF16),
        norm_ffn_g=P["norm_ffn_g"][l][None, :],
        w_up=P["w_up"][l].astype(BF16),
        ffn_conv_w=P["ffn_conv_w"][l],
        ffn_conv_b=P["ffn_conv_b"][l][None, :],
        w_down=P["w_down"][l].astype(BF16),
        final_g=P["final_norm_g"][None, :],
    )


def _layer(x, l, W, n_seq, rows_per_seq, cache_k, cache_v, s0, gbuf, fbuf, final, tm, tq):
    rows, d = x.shape
    tm = min(tm, rows)
    aw = A_HEADS * A_VDIM
    nqkv = B_HEADS * (2 * B_DK + B_DV)
    lam0 = _lambda_init(l)
    slopes = (2.0 ** (-8.0 * jnp.arange(1, A_HEADS + 1, dtype=F32) / A_HEADS)) * LOG2E
    prompt = cache_k is None

    v_kinds = (F32, "t") if prompt else (F32, BF16)
    q, k, kb, v, vx = _norm_proj(
        x, W["norm_mix_g"], W["w_a"],
        ((0, aw, (BF16,)), (aw, aw, (F32, BF16)), (2 * aw, aw, v_kinds)), tm if not prompt else tq)
    qkvb, ba = _norm_proj(x, W["norm_mix_g"], W["w_b"],
                          ((0, nqkv, (F32,)), (nqkv, LANES, (F32,))), tm)
    z, ga, gb = _norm_proj(x, W["norm_mix_g"], W["w_c"],
                           ((0, aw, (F32,)), (aw, d, (F32,)), (aw + d, d, (F32,))), tm)

    if prompt:
        o_a = _attn_prompt(q, kb, vx, W["lam_rows"], W["subln_g"], slopes, lam0, tq)
    else:
        o_a = _attn_sample(q, kb, vx, cache_k, cache_v, l, W["lam_rows"], W["subln_g"],
                           slopes, lam0, rows_per_seq)

    o_b, s_new, gbuf_new = _gdn(qkvb, ba, z, s0, gbuf, W["gdn_conv_w"], W["par"],
                                W["gdn_norm_g"], n_seq, rows_per_seq)

    h = _merge(x, o_a, o_b, ga, gb, W["w_proj_a"], W["w_proj_b"], W["w_out"], tm)
    d_ff = W["w_down"].shape[0]
    y, fbuf_new = _ffn(h, W["norm_ffn_g"], fbuf, W["w_up"], W["ffn_conv_w"], W["ffn_conv_b"],
                       W["w_down"], W["final_g"], n_seq, rows_per_seq,
                       min(tm, rows_per_seq), d_ff // 2, final)
    return y, k, v, s_new, gbuf_new, fbuf_new


def kernel(x_prompt, x_sample, cache_k, cache_v, state_gdn, state_gdn_conv, state_ffn_conv, norm_mix_g, w_in, lambda_q1, lambda_k1, lambda_q2, lambda_k2, subln_g, gdn_conv_w, gdn_a_log, gdn_dt_bias, gdn_norm_g, w_proj_a, w_proj_b, w_out, norm_ffn_g, w_up, ffn_conv_w, ffn_conv_b, w_down, final_norm_g):
    P = dict(norm_mix_g=norm_mix_g, w_in=w_in, lambda_q1=lambda_q1, lambda_k1=lambda_k1,
             lambda_q2=lambda_q2, lambda_k2=lambda_k2, subln_g=subln_g, gdn_conv_w=gdn_conv_w,
             gdn_a_log=gdn_a_log, gdn_dt_bias=gdn_dt_bias, gdn_norm_g=gdn_norm_g,
             w_proj_a=w_proj_a, w_proj_b=w_proj_b, w_out=w_out, norm_ffn_g=norm_ffn_g,
             w_up=w_up, ffn_conv_w=ffn_conv_w, ffn_conv_b=ffn_conv_b, w_down=w_down,
             final_norm_g=final_norm_g)
    depth = w_in.shape[0]
    bp, tp, d = x_prompt.shape
    bs, ts, _ = x_sample.shape
    past = cache_k.shape[2]
    assert ts == CHUNK and past % CHUNK == 0 and tp % CHUNK == 0
    nqkv = state_gdn_conv.shape[-1]
    d_ff2 = state_ffn_conv.shape[-1]
    weights = [_prep_layer_weights(l, P) for l in range(depth)]
    ck = cache_k.reshape(depth, bs, past, A_HEADS * A_VDIM)
    cv = cache_v.reshape(depth, bs, past, A_HEADS * A_VDIM)

    tq = min(512, tp)
    tm = 256

    def run(x3, nseq, rps, sample):
        assert sample or nseq == 1
        x = x3.reshape(nseq * rps, d)
        ks, vs, ss, gcs, fcs = [], [], [], [], []
        for l in range(depth):
            if sample:
                s0, gbuf, fbuf = state_gdn[l], state_gdn_conv[l], state_ffn_conv[l]
                c_k, c_v = ck, cv
            else:
                s0 = jnp.zeros((nseq, B_HEADS, B_DK, B_DV), F32)
                gbuf = jnp.zeros((nseq, GDN_CONV - 1, nqkv), F32)
                fbuf = jnp.zeros((nseq, FFN_CONV - 1, d_ff2), F32)
                c_k = c_v = None
            x, k, v, s_new, g_new, f_new = _layer(
                x, l, weights[l], nseq, rps, c_k, c_v, s0, gbuf, fbuf,
                l == depth - 1, tm, tq)
            ks.append(k.reshape(nseq, rps, A_HEADS, A_VDIM))
            vs.append(v.reshape(nseq, rps, A_HEADS, A_VDIM))
            ss.append(s_new)
            gcs.append(g_new)
            fcs.append(f_new)
        return (x.reshape(nseq, rps, d), jnp.stack(ks), jnp.stack(vs), jnp.stack(ss),
                jnp.stack(gcs), jnp.stack(fcs))

    y_p, k_p, v_p, s_p, gc_p, fc_p = run(x_prompt, bp, tp, False)
    y_s, k_s, v_s, s_s, gc_s, fc_s = run(x_sample, bs, ts, True)
    return (y_p, y_s, k_p, v_p, s_p, gc_p, fc_p, k_s, v_s, s_s, gc_s, fc_s)
```

```python
import functools
import math

import jax
import jax.numpy as jnp
from jax import lax
from jax.experimental import pallas as pl
from jax.experimental.pallas import tpu as pltpu

F32 = jnp.float32
BF16 = jnp.bfloat16

EPS = 1e-6
NEG = -1e30
LOG2E = 1.4426950408889634

CHUNK = 64
A_HEADS = 8
A_HEAD_DIM = 64
A_VDIM = 128
B_HEADS = 8
B_DK = 128
B_DV = 128
GDN_CONV = 4
FFN_CONV = 3
LANES = 128
HIST = 8

VMEM_LIMIT = 56 * 1024 * 1024


def _cparams(n_axes):
    return pltpu.CompilerParams(
        dimension_semantics=("arbitrary",) * n_axes,
        vmem_limit_bytes=VMEM_LIMIT)


def _lambda_init(layer):
    return 0.8 - 0.6 * math.exp(-0.3 * layer)


def _norm_proj_kernel(x_ref, g_ref, w_ref, *out_refs, segs):
    x = x_ref[...]
    ms = jnp.mean(x * x, axis=-1, keepdims=True)
    xn = (x * lax.rsqrt(ms + EPS) * g_ref[...]).astype(BF16)
    k = 0
    for start, width, kinds in segs:
        w = w_ref[:, start:start + width]
        r = None
        for kind in kinds:
            o_ref = out_refs[k]
            k += 1
            if r is None:
                r = jnp.dot(xn, w, preferred_element_type=F32)
            if kind == "t":
                o_ref[...] = r.T.astype(o_ref.dtype)
            elif kind == "h":
                for hd in range(width // LANES):
                    o_ref[:, hd, :] = r[:, hd * LANES:(hd + 1) * LANES]
            else:
                o_ref[...] = r.astype(o_ref.dtype)


def _norm_proj(x, g, w, segs, tm):
    rows, d = x.shape
    assert rows % tm == 0
    out_shape, out_specs, ksegs = [], [], []
    for start, width, kinds in segs:
        kk = []
        for kind in kinds:
            if kind == "t":
                out_shape.append(jax.ShapeDtypeStruct((rows // tm, width, tm), BF16))
                out_specs.append(pl.BlockSpec((None, width, tm), lambda i: (i, 0, 0)))
                kk.append("t")
            elif kind == "h":
                out_shape.append(jax.ShapeDtypeStruct((rows, width // LANES, LANES), F32))
                out_specs.append(pl.BlockSpec((tm, width // LANES, LANES), lambda i: (i, 0, 0)))
                kk.append("h")
            else:
                out_shape.append(jax.ShapeDtypeStruct((rows, width), kind))
                out_specs.append(pl.BlockSpec((tm, width), lambda i: (i, 0)))
                kk.append("n")
        ksegs.append((start, width, tuple(kk)))
    return pl.pallas_call(
        functools.partial(_norm_proj_kernel, segs=tuple(ksegs)),
        grid=(rows // tm,),
        in_specs=[pl.BlockSpec((tm, d), lambda i: (i, 0)),
                  pl.BlockSpec((1, d), lambda i: (0, 0)),
                  pl.BlockSpec(w.shape, lambda i: (0, 0))],
        out_specs=out_specs,
        out_shape=out_shape,
        compiler_params=_cparams(1),
        name="norm_proj",
    )(x, g, w)


def _lambda_value(lam_ref, lam0):
    a = jnp.sum(lam_ref[0:1, :] * lam_ref[1:2, :], axis=-1, keepdims=True)
    b = jnp.sum(lam_ref[2:3, :] * lam_ref[3:4, :], axis=-1, keepdims=True)
    return jnp.exp(a) - jnp.exp(b) + lam0


def _stack_maps(q):
    lane = lax.broadcasted_iota(jnp.int32, q.shape, 1)
    zero = jnp.zeros_like(q)
    return jnp.concatenate([jnp.where(lane < A_HEAD_DIM, q, zero),
                            jnp.where(lane >= A_HEAD_DIM, q, zero)], axis=0)


ATTN_STRIP = 256
NT_DIMS = (((1,), (1,)), ((), ()))


def _attn_prompt_kernel(slopes_ref, lam_ref, g_ref, q_ref, k_ref, vt_ref, o_ref,
                        qq_sc, kaug_sc, sa_sc, sb_sc, m_sc, l_sc, acc_sc, *, tq, lam0):
    h = pl.program_id(0)
    i = pl.program_id(1)
    slope = slopes_ref[h]
    q0 = i * tq
    nt = NT_DIMS
    w = ATTN_STRIP

    qq_sc[:, :A_VDIM] = _stack_maps(q_ref[...])
    lane_q = lax.broadcasted_iota(jnp.int32, (2 * tq, LANES), 1)
    qq_sc[:, A_VDIM:] = jnp.where(lane_q < 3, 1.0, 0.0).astype(BF16)
    b = slope * lax.broadcasted_iota(jnp.int32, (tq, LANES), 0).astype(F32)
    b1 = b.astype(BF16).astype(F32)
    b2 = (b - b1).astype(BF16).astype(F32)
    b3 = b - b1 - b2
    lane_k = lax.broadcasted_iota(jnp.int32, (tq, LANES), 1)
    kaug_sc[...] = jnp.where(lane_k == 0, b1, jnp.where(
        lane_k == 1, b2, jnp.where(lane_k == 2, b3, 0.0))).astype(BF16)

    m_sc[...] = jnp.full(m_sc.shape, NEG, F32)
    l_sc[...] = jnp.zeros(l_sc.shape, F32)
    acc_sc[...] = jnp.zeros(acc_sc.shape, F32)

    def softmax_update(st, s, kappa, vtt):
        sl = slice(st * w, (st + 1) * w)
        m_prev = m_sc[:, sl]
        m_new = jnp.maximum(m_prev, jnp.max(s, axis=0, keepdims=True) + kappa)
        alpha = jnp.exp2(m_prev - m_new)
        p = jnp.exp2(s - (m_new - kappa))
        l_sc[:, sl] = alpha * l_sc[:, sl] + jnp.sum(p, axis=0, keepdims=True)
        acc_sc[:, sl] = alpha * acc_sc[:, sl] + jnp.dot(
            vtt, p.astype(BF16), preferred_element_type=F32)
        m_sc[:, sl] = m_new

    kt = k_ref[pl.ds(pl.multiple_of(q0, tq), tq), :]
    s_diag = lax.dot_general(kt, qq_sc[:, :A_VDIM], nt, preferred_element_type=F32)
    vtt = vt_ref[i]
    for st in range(2 * tq // w):
        c = lax.broadcasted_iota(jnp.int32, (tq, w), 0)
        r = lax.broadcasted_iota(jnp.int32, (tq, w), 1) + (st * w) % tq
        bias = slope * jnp.minimum(c, 2 * r - c).astype(F32)
        visible = (c // CHUNK) <= (r // CHUNK)
        s = jnp.where(visible, s_diag[:, st * w:(st + 1) * w] + bias, NEG)
        softmax_update(st, s, 0.0, vtt)

    def produce(j, s_ref):
        k0 = pl.multiple_of(j * tq, tq)
        lhs = jnp.concatenate([k_ref[pl.ds(k0, tq), :], kaug_sc[...]], axis=1)
        s_ref[...] = lax.dot_general(lhs, qq_sc[...], nt, preferred_element_type=F32)

    def consume(j, s_ref):
        kappa = slope * (j * tq - q0).astype(F32)
        vtj = vt_ref[j]
        for st in range(2 * tq // w):
            softmax_update(st, s_ref[:, st * w:(st + 1) * w], kappa, vtj)

    @pl.when(i >= 1)
    def _():
        produce(0, sa_sc)

    def two_tiles(t, carry):
        j = 2 * t
        produce(j + 1, sb_sc)
        consume(j, sa_sc)
        produce(j + 2, sa_sc)
        consume(j + 1, sb_sc)
        return carry

    lax.fori_loop(0, i // 2, two_tiles, 0)

    @pl.when(i % 2 == 1)
    def _():
        consume(i - 1, sa_sc)

    lam = _lambda_value(lam_ref, lam0)
    inv_l = 1.0 / l_sc[...]
    acc = acc_sc[...] * inv_l
    o = acc[:, :tq] - lam * acc[:, tq:]
    ms = jnp.mean(o * o, axis=0, keepdims=True)
    o = (o * lax.rsqrt(ms + EPS)).T
    o_ref[...] = (o * g_ref[...] * (1.0 - lam0)).astype(o_ref.dtype)


def _attn_prompt(q, kb, vt, lam_rows, subln_g, slopes, lam0, tq):
    t = q.shape[0]
    nt = t // tq
    return pl.pallas_call(
        functools.partial(_attn_prompt_kernel, tq=tq, lam0=lam0),
        grid=(A_HEADS, nt),
        in_specs=[pl.BlockSpec(memory_space=pltpu.SMEM),
                  pl.BlockSpec((8, LANES), lambda h, i: (0, 0)),
                  pl.BlockSpec((1, A_VDIM), lambda h, i: (0, 0)),
                  pl.BlockSpec((tq, A_VDIM), lambda h, i: (i, h)),
                  pl.BlockSpec((t, A_VDIM), lambda h, i: (0, h)),
                  pl.BlockSpec((nt, A_VDIM, tq), lambda h, i: (0, h, 0))],
        out_specs=pl.BlockSpec((tq, A_VDIM), lambda h, i: (i, h)),
        out_shape=jax.ShapeDtypeStruct((t, A_HEADS * A_VDIM), BF16),
        scratch_shapes=[pltpu.VMEM((2 * tq, 2 * LANES), BF16),
                        pltpu.VMEM((tq, LANES), BF16),
                        pltpu.VMEM((tq, 2 * tq), F32),
                        pltpu.VMEM((tq, 2 * tq), F32),
                        pltpu.VMEM((1, 2 * tq), F32),
                        pltpu.VMEM((1, 2 * tq), F32),
                        pltpu.VMEM((A_VDIM, 2 * tq), F32)],
        compiler_params=_cparams(2),
        name="attn_prompt",
    )(slopes, lam_rows, subln_g, q, kb, vt)


def _attn_sample_kernel(slopes_ref, lam_ref, g_ref, q_ref, kn_ref, vn_ref, kc_ref, vc_ref,
                        o_ref, *, past, lam0):
    h = pl.program_id(1)
    slope = slopes_ref[h]
    tq = q_ref.shape[0]
    qq = _stack_maps(q_ref[...])
    dn = (((1,), (1,)), ((), ()))
    s_c = lax.dot_general(qq, kc_ref[...].astype(BF16), dn, preferred_element_type=F32)
    kpos = lax.broadcasted_iota(jnp.int32, (1, past), 1)
    s_c = s_c + slope * (kpos - past).astype(F32)
    s_n = lax.dot_general(qq, kn_ref[...], dn, preferred_element_type=F32)
    r = lax.broadcasted_iota(jnp.int32, (2 * tq, tq), 0)
    c = lax.broadcasted_iota(jnp.int32, (2 * tq, tq), 1)
    r = jnp.where(r >= tq, r - tq, r)
    s_n = s_n + slope * jnp.minimum(c, 2 * r - c).astype(F32)
    m = jnp.maximum(jnp.max(s_c, axis=-1, keepdims=True), jnp.max(s_n, axis=-1, keepdims=True))
    p_c = jnp.exp2(s_c - m)
    p_n = jnp.exp2(s_n - m)
    l = jnp.sum(p_c, axis=-1, keepdims=True) + jnp.sum(p_n, axis=-1, keepdims=True)
    acc = (jnp.dot(p_c.astype(BF16), vc_ref[...].astype(BF16), preferred_element_type=F32)
           + jnp.dot(p_n.astype(BF16), vn_ref[...], preferred_element_type=F32))
    acc = acc * (1.0 / l)
    lam = _lambda_value(lam_ref, lam0)
    o = acc[:tq] - lam * acc[tq:]
    ms = jnp.mean(o * o, axis=-1, keepdims=True)
    o_ref[...] = (o * lax.rsqrt(ms + EPS) * g_ref[...] * (1.0 - lam0)).astype(o_ref.dtype)


def _attn_sample(q, kb, vb, cache_k, cache_v, layer, lam_rows, subln_g, slopes, lam0, tq):
    nb, past = cache_k.shape[1], cache_k.shape[2]
    return pl.pallas_call(
        functools.partial(_attn_sample_kernel, past=past, lam0=lam0),
        grid=(nb, A_HEADS),
        in_specs=[pl.BlockSpec(memory_space=pltpu.SMEM),
                  pl.BlockSpec((8, LANES), lambda b, h: (0, 0)),
                  pl.BlockSpec((1, A_VDIM), lambda b, h: (0, 0)),
                  pl.BlockSpec((tq, A_VDIM), lambda b, h: (b, h)),
                  pl.BlockSpec((tq, A_VDIM), lambda b, h: (b, h)),
                  pl.BlockSpec((tq, A_VDIM), lambda b, h: (b, h)),
                  pl.BlockSpec((None, None, past, A_VDIM), lambda b, h: (layer, b, 0, h)),
                  pl.BlockSpec((None, None, past, A_VDIM), lambda b, h: (layer, b, 0, h))],
        out_specs=pl.BlockSpec((tq, A_VDIM), lambda b, h: (b, h)),
        out_shape=jax.ShapeDtypeStruct((nb * tq, A_HEADS * A_VDIM), BF16),
        compiler_params=_cparams(2),
        name="attn_sample",
    )(slopes, lam_rows, subln_g, q, kb, vb, cache_k, cache_v)


_HI = lax.Precision.HIGHEST


def _dot(a, b):
    return jnp.dot(a, b, preferred_element_type=F32)


def _split(a):
    hi = a.astype(BF16)
    return hi, (a - hi.astype(F32)).astype(BF16)


def _block_diag2(a, left):
    zero = jnp.zeros_like(a)
    return jnp.concatenate([jnp.where(left, a, zero), jnp.where(left, zero, a)], axis=0)


def _pair_prod(a_parts, b_parts, left):
    ah, bh = a_parts[0], b_parts[0]
    bdh = _block_diag2(bh, left)
    if len(a_parts) == 1 and len(b_parts) == 1:
        return _dot(ah, bdh)
    if len(a_parts) == 1:
        return _dot(jnp.concatenate([ah, ah], axis=1),
                    jnp.concatenate([bdh, _block_diag2(b_parts[1], left)], axis=0))
    r = _dot(jnp.concatenate([ah, a_parts[1]], axis=1), jnp.concatenate([bdh, bdh], axis=0))
    if len(b_parts) == 1:
        return r
    return r + _dot(ah, _block_diag2(b_parts[1], left))


def _inverse_masks(ri, cj):
    n = ri.shape[0]
    masks = [(ri // 2) == (cj // 2)]
    b = 2
    while b < n:
        same_big = ((ri // (2 * b)) == (cj // (2 * b))).astype(jnp.int32)
        same_small = ((ri // b) == (cj // b)).astype(jnp.int32)
        masks.append((same_big - same_small) > 0)
        b *= 2
    return masks


def _unit_lower_inverse(lows, masks, eye, left):
    xs = [eye - jnp.where(masks[0], low, 0.0) for low in lows]
    for mask in masks[1:]:
        xparts = [_split(x) for x in xs]
        ys = [_pair_prod((jnp.where(mask, low, 0.0).astype(BF16),), xp, left)
              for low, xp in zip(lows, xparts)]
        xs = [x - _pair_prod(xp, _split(y), left) for x, xp, y in zip(xs, xparts, ys)]
    return xs


def _gdn_kernel(x_ref, buf_ref, ba_ref, z_ref, s0_ref, cw_ref, par_ref, ng_ref,
                o_ref, s_out_ref, buf_out_ref, xp_sc, s_sc):
    c_idx = pl.program_id(1)
    nc = pl.num_programs(1)
    C = x_ref.shape[0]
    nq = B_HEADS * B_DK

    @pl.when(c_idx == 0)
    def _():
        xp_sc[HIST - (GDN_CONV - 1):HIST, :] = buf_ref[...]
        s_sc[...] = s0_ref[...].astype(F32)

    xp_sc[HIST:HIST + C, :] = x_ref[...]
    conv = xp_sc[HIST:HIST + C, :] * cw_ref[GDN_CONV - 1:GDN_CONV, :]
    for j in range(GDN_CONV - 1):
        off = HIST - (GDN_CONV - 1) + j
        conv = conv + xp_sc[off:off + C, :] * cw_ref[j:j + 1, :]
    conv = conv * jax.nn.sigmoid(conv)
    tail = xp_sc[C:C + HIST, :]
    xp_sc[0:HIST, :] = tail
    buf_out_ref[...] = tail[HIST - (GDN_CONV - 1):HIST, :]

    ba = ba_ref[...]
    lane = lax.broadcasted_iota(jnp.int32, ba.shape, 1)
    beta_all = jax.nn.sigmoid(ba)
    g_all = jnp.where((lane >= B_HEADS) & (lane < 2 * B_HEADS),
                      -jnp.exp(par_ref[0:1, :]) * jax.nn.softplus(ba + par_ref[1:2, :]), 0.0)
    r64 = lax.broadcasted_iota(jnp.int32, (C, C), 0)
    c64 = lax.broadcasted_iota(jnp.int32, (C, C), 1)
    tri = jnp.where(r64 >= c64, 1.0, 0.0).astype(F32)
    gcum_all = jnp.dot(tri, g_all, precision=_HI, preferred_element_type=F32)

    ri = lax.broadcasted_iota(jnp.int32, (C, 2 * C), 0)
    cl = lax.broadcasted_iota(jnp.int32, (C, 2 * C), 1)
    left = cl < C
    cj = jnp.where(left, cl, cl - C)
    incl = ri >= cj
    strict = ri > cj
    diag = ri == cj
    eye = jnp.where(diag, 1.0, 0.0).astype(F32)
    inv_masks = _inverse_masks(ri, cj)
    ones3 = jnp.ones((C, 3 * C), BF16)
    zk = jnp.zeros((C, B_DK), F32)
    z2 = jnp.zeros((C, B_DV + B_DK), F32)
    cat = jnp.concatenate

    pairs = range(B_HEADS // 2)
    q, k, v, kb, beta, gc, eg = [], [], [], [], [], [], []
    for h in range(B_HEADS):
        qh = conv[:, h * B_DK:(h + 1) * B_DK]
        kh = conv[:, nq + h * B_DK:nq + (h + 1) * B_DK]
        q.append(qh * lax.rsqrt(jnp.sum(qh * qh, axis=-1, keepdims=True) + EPS)
                 * (B_DK ** -0.5))
        k.append(kh * lax.rsqrt(jnp.sum(kh * kh, axis=-1, keepdims=True) + EPS))
        v.append(conv[:, 2 * nq + h * B_DV:2 * nq + (h + 1) * B_DV])
        beta.append(beta_all[:, h:h + 1])
        gc.append(gcum_all[:, B_HEADS + h:B_HEADS + h + 1])
        kb.append(k[-1] * beta[-1])
        eg.append(jnp.exp(gc[-1]))

    lows, attns = [], []
    for p in pairs:
        a, b = 2 * p, 2 * p + 1
        g_col = jnp.where(left, gc[a], gc[b])
        d0 = jnp.where(diag, g_col, 0.0)
        d1 = d0.astype(BF16)
        rem = d0 - d1.astype(F32)
        d2 = rem.astype(BF16)
        d3 = (rem - d2.astype(F32)).astype(BF16)
        g_row = _dot(ones3, cat([d1, d2, d3], axis=0))
        decay = jnp.where(incl, jnp.exp(jnp.where(incl, g_col - g_row, 0.0)), 0.0)
        lhs = cat([cat([kb[a], kb[b]], axis=1), cat([q[a], q[b]], axis=1)], axis=0)
        rhs = cat([cat([k[a], zk], axis=1), cat([zk, k[b]], axis=1)], axis=0)
        m1 = lax.dot_general(lhs.astype(BF16), rhs.astype(BF16), NT_DIMS,
                             preferred_element_type=F32)
        lows.append(jnp.where(strict, m1[:C] * decay, 0.0).astype(BF16).astype(F32))
        attns.append(m1[C:] * decay)

    tinvs = _unit_lower_inverse(lows, inv_masks, eye, left)

    sols = []
    for p in pairs:
        a, b = 2 * p, 2 * p + 1
        rhs_a = cat([v[a] * beta[a], kb[a] * eg[a]], axis=1)
        rhs_b = cat([v[b] * beta[b], kb[b] * eg[b]], axis=1)
        rhs_bd = cat([cat([rhs_a, z2], axis=1), cat([z2, rhs_b], axis=1)], axis=0)
        sols.append(_dot(tinvs[p].astype(BF16), rhs_bd.astype(BF16)))

    s_old = [s_sc[h] for h in range(B_HEADS)]
    v_new, q_s = [], []
    for h in range(B_HEADS):
        base = (h % 2) * (B_DV + B_DK)
        u = sols[h // 2][:, base:base + B_DV]
        w = sols[h // 2][:, base + B_DV:base + B_DV + B_DK]
        m2 = _dot(cat([w, q[h] * eg[h]], axis=0).astype(BF16), s_old[h].astype(BF16))
        v_new.append(u - m2[:C])
        q_s.append(m2[C:])

    zv = jnp.zeros((C, B_DV), F32)
    g_last = [g[C - 1:C, :] for g in gc]
    o_pairs, upds = [], []
    for p in pairs:
        a, b = 2 * p, 2 * p + 1
        vn_bd = cat([cat([v_new[a], zv], axis=1), cat([zv, v_new[b]], axis=1)],
                    axis=0).astype(BF16)
        o_pairs.append(_dot(attns[p].astype(BF16), vn_bd))
        kd_t = cat([k[a] * jnp.exp(g_last[a] - gc[a]),
                    k[b] * jnp.exp(g_last[b] - gc[b])], axis=0).T
        upds.append(_dot(kd_t.astype(BF16), vn_bd))

    for h in range(B_HEADS):
        lo = (h % 2) * B_DV
        s_sc[h] = s_old[h] * jnp.exp(g_last[h]) + upds[h // 2][:, lo:lo + B_DV]
        o = q_s[h] + o_pairs[h // 2][:, lo:lo + B_DV]
        zh = z_ref[:, h * B_DV:(h + 1) * B_DV]
        on = o * lax.rsqrt(jnp.mean(o * o, axis=-1, keepdims=True) + EPS) * ng_ref[...]
        o_ref[:, h * B_DV:(h + 1) * B_DV] = (on * (zh * jax.nn.sigmoid(zh))).astype(o_ref.dtype)

    @pl.when(c_idx == nc - 1)
    def _():
        s_out_ref[...] = s_sc[...].astype(s_out_ref.dtype)


def _gdn(qkv, ba, z, s0, buf, conv_w, par, norm_g, n_seq, rows_per_seq):
    nc = rows_per_seq // CHUNK
    nqkv = qkv.shape[1]
    row = lambda b, c: (b * nc + c, 0)
    return pl.pallas_call(
        _gdn_kernel,
        grid=(n_seq, nc),
        in_specs=[pl.BlockSpec((CHUNK, nqkv), row),
                  pl.BlockSpec((None, GDN_CONV - 1, nqkv), lambda b, c: (b, 0, 0)),
                  pl.BlockSpec((CHUNK, LANES), row),
                  pl.BlockSpec((CHUNK, B_HEADS * B_DV), row),
                  pl.BlockSpec((None, B_HEADS, B_DK, B_DV), lambda b, c: (b, 0, 0, 0)),
                  pl.BlockSpec((GDN_CONV, nqkv), lambda b, c: (0, 0)),
                  pl.BlockSpec((8, LANES), lambda b, c: (0, 0)),
                  pl.BlockSpec((1, B_DV), lambda b, c: (0, 0))],
        out_specs=[pl.BlockSpec((CHUNK, B_HEADS * B_DV), row),
                   pl.BlockSpec((None, B_HEADS, B_DK, B_DV), lambda b, c: (b, 0, 0, 0)),
                   pl.BlockSpec((None, GDN_CONV - 1, nqkv), lambda b, c: (b, 0, 0))],
        out_shape=[jax.ShapeDtypeStruct((n_seq * rows_per_seq, B_HEADS * B_DV), BF16),
                   jax.ShapeDtypeStruct((n_seq, B_HEADS, B_DK, B_DV), F32),
                   jax.ShapeDtypeStruct((n_seq, GDN_CONV - 1, nqkv), F32)],
        scratch_shapes=[pltpu.VMEM((HIST + CHUNK, nqkv), F32),
                        pltpu.VMEM((B_HEADS, B_DK, B_DV), F32)],
        compiler_params=_cparams(2),
        name="gdn",
    )(qkv, buf, ba, z, s0, conv_w, par, norm_g)


def _merge_kernel(x_ref, oa_ref, ob_ref, ga_ref, gb_ref, wa_ref, wb_ref, wo_ref, h_ref):
    a = jnp.dot(oa_ref[...], wa_ref[...], preferred_element_type=F32)
    b = jnp.dot(ob_ref[...], wb_ref[...], preferred_element_type=F32)
    merged = jax.nn.sigmoid(ga_ref[...]) * a + jax.nn.sigmoid(gb_ref[...]) * b
    h_ref[...] = x_ref[...] + jnp.dot(merged.astype(BF16), wo_ref[...],
                                      preferred_element_type=F32)


def _merge(x, oa, ob, ga, gb, wa, wb, wo, tm):
    rows, d = x.shape
    rowspec = pl.BlockSpec((tm, d), lambda i: (i, 0))
    wspec = pl.BlockSpec((d, d), lambda i: (0, 0))
    return pl.pallas_call(
        _merge_kernel,
        grid=(rows // tm,),
        in_specs=[rowspec] * 5 + [wspec] * 3,
        out_specs=rowspec,
        out_shape=jax.ShapeDtypeStruct((rows, d), F32),
        compiler_params=_cparams(1),
        name="merge",
    )(x, oa, ob, ga, gb, wa, wb, wo)


def _ffn_kernel(h_ref, g_ref, buf_g_ref, buf_v_ref, wg_ref, wv_ref, cwg_ref, cwv_ref,
                cbg_ref, cbv_ref, wd_ref, fg_ref, y_ref, nbuf_g_ref, nbuf_v_ref,
                hn_sc, ug_sc, uv_sc, acc_sc, *, tiles_per_seq, final):
    i = pl.program_id(0)
    j = pl.program_id(1)
    nj = pl.num_programs(1)
    tm = h_ref.shape[0]
    first = (i % tiles_per_seq) == 0
    nh = FFN_CONV - 1

    @pl.when(j == 0)
    def _():
        x = h_ref[...]
        ms = jnp.mean(x * x, axis=-1, keepdims=True)
        hn_sc[...] = (x * lax.rsqrt(ms + EPS) * g_ref[...]).astype(BF16)
        acc_sc[...] = jnp.zeros(acc_sc.shape, F32)

    def conv_half(u_sc, w_ref, cw_ref, cb_ref, buf_ref, nbuf_ref):
        @pl.when(first)
        def _():
            u_sc[j, HIST - nh:HIST, :] = buf_ref[...]
        u_sc[j, HIST:HIST + tm, :] = jnp.dot(hn_sc[...], w_ref[...], preferred_element_type=F32)
        out = u_sc[j, HIST:HIST + tm, :] * cw_ref[nh:nh + 1, :] + cb_ref[...]
        for t in range(nh):
            off = HIST - nh + t
            out = out + u_sc[j, off:off + tm, :] * cw_ref[t:t + 1, :]
        tail = u_sc[j, tm:tm + HIST, :]
        u_sc[j, 0:HIST, :] = tail
        nbuf_ref[j] = tail[HIST - nh:HIST, :]
        return out

    gate = conv_half(ug_sc, wg_ref, cwg_ref, cbg_ref, buf_g_ref, nbuf_g_ref)
    val = conv_half(uv_sc, wv_ref, cwv_ref, cbv_ref, buf_v_ref, nbuf_v_ref)
    act = (gate * jax.nn.sigmoid(gate) * val).astype(BF16)
    acc_sc[...] += jnp.dot(act, wd_ref[...], preferred_element_type=F32)

    @pl.when(j == nj - 1)
    def _():
        y = h_ref[...] + acc_sc[...]
        if final:
            ms = jnp.mean(y * y, axis=-1, keepdims=True)
            y = y * lax.rsqrt(ms + EPS) * fg_ref[...]
        y_ref[...] = y


def _ffn(h, g, buf, w_up, conv_w, conv_b, w_down, final_g, n_seq, rows_per_seq, tm, cw, final):
    rows, d = h.shape
    d_ff = w_down.shape[0]
    assert d_ff % cw == 0 and rows_per_seq % tm == 0
    nj = d_ff // cw
    tps = rows_per_seq // tm
    nh = FFN_CONV - 1
    gate_col = lambda i, j: (0, j)
    val_col = lambda i, j: (0, nj + j)
    seq_gate = lambda i, j: (i // tps, 0, j)
    seq_val = lambda i, j: (i // tps, 0, nj + j)
    y, nbg, nbv = pl.pallas_call(
        functools.partial(_ffn_kernel, tiles_per_seq=tps, final=final),
        grid=(rows // tm, nj),
        in_specs=[pl.BlockSpec((tm, d), lambda i, j: (i, 0)),
                  pl.BlockSpec((1, d), lambda i, j: (0, 0)),
                  pl.BlockSpec((None, nh, cw), seq_gate),
                  pl.BlockSpec((None, nh, cw), seq_val),
                  pl.BlockSpec((d, cw), gate_col),
                  pl.BlockSpec((d, cw), val_col),
                  pl.BlockSpec((FFN_CONV, cw), gate_col),
                  pl.BlockSpec((FFN_CONV, cw), val_col),
                  pl.BlockSpec((1, cw), gate_col),
                  pl.BlockSpec((1, cw), val_col),
                  pl.BlockSpec((cw, d), lambda i, j: (j, 0)),
                  pl.BlockSpec((1, d), lambda i, j: (0, 0))],
        out_specs=[pl.BlockSpec((tm, d), lambda i, j: (i, 0)),
                   pl.BlockSpec((None, nj, nh, cw), lambda i, j: (i // tps, 0, 0, 0)),
                   pl.BlockSpec((None, nj, nh, cw), lambda i, j: (i // tps, 0, 0, 0))],
        out_shape=[jax.ShapeDtypeStruct((rows, d), F32),
                   jax.ShapeDtypeStruct((n_seq, nj, nh, cw), F32),
                   jax.ShapeDtypeStruct((n_seq, nj, nh, cw), F32)],
        scratch_shapes=[pltpu.VMEM((tm, d), BF16),
                        pltpu.VMEM((nj, HIST + tm, cw), F32),
                        pltpu.VMEM((nj, HIST + tm, cw), F32),
                        pltpu.VMEM((tm, d), F32)],
        compiler_params=_cparams(2),
        name="ffn",
    )(h, g, buf, buf, w_up, w_up, conv_w, conv_w, conv_b, conv_b, w_down, final_g)
    unchunk = lambda a: a.transpose(0, 2, 1, 3).reshape(n_seq, nh, d_ff)
    return y, jnp.concatenate([unchunk(nbg), unchunk(nbv)], axis=-1)


def _prep_layer_weights(l, P):
    w_in = P["w_in"][l]
    d = w_in.shape[0]
    aw = A_HEADS * A_VDIM
    nqkv = B_HEADS * (2 * B_DK + B_DV)
    c_qkvb = 3 * aw
    c_beta = c_qkvb + nqkv
    c_z = c_beta + 2 * B_HEADS
    bw = B_HEADS * B_DV
    w_q = w_in[:, :aw] * (A_HEAD_DIM ** -0.5 * LOG2E)
    w_a = jnp.concatenate([w_q, w_in[:, aw:3 * aw]], axis=1).astype(BF16)
    w_b = jnp.concatenate([w_in[:, c_qkvb:c_beta + 2 * B_HEADS],
                           jnp.zeros((d, LANES - 2 * B_HEADS), F32)], axis=1).astype(BF16)
    w_c = w_in[:, c_z:c_z + bw + 2 * d].astype(BF16)
    lam_rows = jnp.zeros((8, LANES), F32)
    for r, name in enumerate(("lambda_q1", "lambda_k1", "lambda_q2", "lambda_k2")):
        lam_rows = lam_rows.at[r, :A_HEAD_DIM].set(P[name][l])
    par = jnp.zeros((8, LANES), F32)
    par = par.at[0, B_HEADS:2 * B_HEADS].set(P["gdn_a_log"][l])
    par = par.at[1, B_HEADS:2 * B_HEADS].set(P["gdn_dt_bias"][l])
    return dict(
        w_a=w_a, w_b=w_b, w_c=w_c, lam_rows=lam_rows, par=par,
        norm_mix_g=P["norm_mix_g"][l][None, :],
        subln_g=P["subln_g"][l][None, :],
        gdn_conv_w=P["gdn_conv_w"][l],
        gdn_norm_g=P["gdn_norm_g"][l][None, :],
        w_proj_a=P["w_proj_a"][l].astype(BF16),
        w_proj_b=P["w_proj_b"][l].astype(BF16),
        w_out=P["w_out"][l].astype(BF16),
        norm_ffn_g=P["norm_ffn_g"][l][None, :],
        w_up=P["w_up"][l].astype(BF16),
        ffn_conv_w=P["ffn_conv_w"][l],
        ffn_conv_b=P["ffn_conv_b"][l][None, :],
        w_down=P["w_down"][l].astype(BF16),
        final_g=P["final_norm_g"][None, :],
    )


def _layer(x, l, W, n_seq, rows_per_seq, cache_k, cache_v, s0, gbuf, fbuf, final, tm, tq):
    rows, d = x.shape
    tm = min(tm, rows)
    aw = A_HEADS * A_VDIM
    nqkv = B_HEADS * (2 * B_DK + B_DV)
    lam0 = _lambda_init(l)
    slopes = (2.0 ** (-8.0 * jnp.arange(1, A_HEADS + 1, dtype=F32) / A_HEADS)) * LOG2E
    prompt = cache_k is None

    v_kinds = ("h", "t") if prompt else ("h", BF16)
    q, k, kb, v, vx = _norm_proj(
        x, W["norm_mix_g"], W["w_a"],
        ((0, aw, (BF16,)), (aw, aw, ("h", BF16)), (2 * aw, aw, v_kinds)), tm if not prompt else tq)
    qkvb, ba = _norm_proj(x, W["norm_mix_g"], W["w_b"],
                          ((0, nqkv, (F32,)), (nqkv, LANES, (F32,))), tm)
    z, ga, gb = _norm_proj(x, W["norm_mix_g"], W["w_c"],
                           ((0, aw, (F32,)), (aw, d, (F32,)), (aw + d, d, (F32,))), tm)

    if prompt:
        o_a = _attn_prompt(q, kb, vx, W["lam_rows"], W["subln_g"], slopes, lam0, tq)
    else:
        o_a = _attn_sample(q, kb, vx, cache_k, cache_v, l, W["lam_rows"], W["subln_g"],
                           slopes, lam0, rows_per_seq)

    o_b, s_new, gbuf_new = _gdn(qkvb, ba, z, s0, gbuf, W["gdn_conv_w"], W["par"],
                                W["gdn_norm_g"], n_seq, rows_per_seq)

    h = _merge(x, o_a, o_b, ga, gb, W["w_proj_a"], W["w_proj_b"], W["w_out"], tm)
    d_ff = W["w_down"].shape[0]
    y, fbuf_new = _ffn(h, W["norm_ffn_g"], fbuf, W["w_up"], W["ffn_conv_w"], W["ffn_conv_b"],
                       W["w_down"], W["final_g"], n_seq, rows_per_seq,
                       min(tm, rows_per_seq), d_ff // 2, final)
    return y, k, v, s_new, gbuf_new, fbuf_new


def kernel(x_prompt, x_sample, cache_k, cache_v, state_gdn, state_gdn_conv, state_ffn_conv, norm_mix_g, w_in, lambda_q1, lambda_k1, lambda_q2, lambda_k2, subln_g, gdn_conv_w, gdn_a_log, gdn_dt_bias, gdn_norm_g, w_proj_a, w_proj_b, w_out, norm_ffn_g, w_up, ffn_conv_w, ffn_conv_b, w_down, final_norm_g):
    P = dict(norm_mix_g=norm_mix_g, w_in=w_in, lambda_q1=lambda_q1, lambda_k1=lambda_k1,
             lambda_q2=lambda_q2, lambda_k2=lambda_k2, subln_g=subln_g, gdn_conv_w=gdn_conv_w,
             gdn_a_log=gdn_a_log, gdn_dt_bias=gdn_dt_bias, gdn_norm_g=gdn_norm_g,
             w_proj_a=w_proj_a, w_proj_b=w_proj_b, w_out=w_out, norm_ffn_g=norm_ffn_g,
             w_up=w_up, ffn_conv_w=ffn_conv_w, ffn_conv_b=ffn_conv_b, w_down=w_down,
             final_norm_g=final_norm_g)
    depth = w_in.shape[0]
    bp, tp, d = x_prompt.shape
    bs, ts, _ = x_sample.shape
    past = cache_k.shape[2]
    assert ts == CHUNK and past % CHUNK == 0 and tp % CHUNK == 0
    nqkv = state_gdn_conv.shape[-1]
    d_ff2 = state_ffn_conv.shape[-1]
    weights = [_prep_layer_weights(l, P) for l in range(depth)]
    ck = cache_k.reshape(depth, bs, past, A_HEADS * A_VDIM)
    cv = cache_v.reshape(depth, bs, past, A_HEADS * A_VDIM)

    tq = min(512, tp)
    tm = 256

    def run(x3, nseq, rps, sample):
        assert sample or nseq == 1
        x = x3.reshape(nseq * rps, d)
        ks, vs, ss, gcs, fcs = [], [], [], [], []
        for l in range(depth):
            if sample:
                s0, gbuf, fbuf = state_gdn[l], state_gdn_conv[l], state_ffn_conv[l]
                c_k, c_v = ck, cv
            else:
                s0 = jnp.zeros((nseq, B_HEADS, B_DK, B_DV), F32)
                gbuf = jnp.zeros((nseq, GDN_CONV - 1, nqkv), F32)
                fbuf = jnp.zeros((nseq, FFN_CONV - 1, d_ff2), F32)
                c_k = c_v = None
            x, k, v, s_new, g_new, f_new = _layer(
                x, l, weights[l], nseq, rps, c_k, c_v, s0, gbuf, fbuf,
                l == depth - 1, tm, tq)
            ks.append(k.reshape(nseq, rps, A_HEADS, A_VDIM))
            vs.append(v.reshape(nseq, rps, A_HEADS, A_VDIM))
            ss.append(s_new)
            gcs.append(g_new)
            fcs.append(f_new)
        return (x.reshape(nseq, rps, d), jnp.stack(ks), jnp.stack(vs), jnp.stack(ss),
                jnp.stack(gcs), jnp.stack(fcs))

    y_p, k_p, v_p, s_p, gc_p, fc_p = run(x_prompt, bp, tp, False)
    y_s, k_s, v_s, s_s, gc_s, fc_s = run(x_sample, bs, ts, True)
    return (y_p, y_s, k_p, v_p, s_p, gc_p, fc_p, k_s, v_s, s_s, gc_s, fc_s)
```

```python
import functools
import math

import jax
import jax.numpy as jnp
from jax import lax
from jax.experimental import pallas as pl
from jax.experimental.pallas import tpu as pltpu

F32 = jnp.float32
BF16 = jnp.bfloat16

EPS = 1e-6
NEG = -1e30
LOG2E = 1.4426950408889634

CHUNK = 64
A_HEADS = 8
A_HEAD_DIM = 64
A_VDIM = 128
B_HEADS = 8
B_DK = 128
B_DV = 128
GDN_CONV = 4
FFN_CONV = 3
LANES = 128
HIST = 8

VMEM_LIMIT = 56 * 1024 * 1024


def _cparams(n_axes):
    return pltpu.CompilerParams(
        dimension_semantics=("arbitrary",) * n_axes,
        vmem_limit_bytes=VMEM_LIMIT)


def _lambda_init(layer):
    return 0.8 - 0.6 * math.exp(-0.3 * layer)


def _norm_proj_kernel(x_ref, g_ref, w_ref, *out_refs, segs):
    x = x_ref[...]
    ms = jnp.mean(x * x, axis=-1, keepdims=True)
    xn = (x * lax.rsqrt(ms + EPS) * g_ref[...]).astype(BF16)
    k = 0
    for start, width, kinds in segs:
        w = w_ref[:, start:start + width]
        r = None
        for kind in kinds:
            o_ref = out_refs[k]
            k += 1
            if r is None:
                r = jnp.dot(xn, w, preferred_element_type=F32)
            if kind == "t":
                o_ref[...] = r.T.astype(o_ref.dtype)
            elif kind == "h":
                for hd in range(width // LANES):
                    o_ref[:, hd, :] = r[:, hd * LANES:(hd + 1) * LANES]
            else:
                o_ref[...] = r.astype(o_ref.dtype)


def _norm_proj(x, g, w, segs, tm):
    rows, d = x.shape
    assert rows % tm == 0
    out_shape, out_specs, ksegs = [], [], []
    for start, width, kinds in segs:
        kk = []
        for kind in kinds:
            if kind == "t":
                out_shape.append(jax.ShapeDtypeStruct((rows // tm, width, tm), BF16))
                out_specs.append(pl.BlockSpec((None, width, tm), lambda i: (i, 0, 0)))
                kk.append("t")
            elif kind == "h":
                out_shape.append(jax.ShapeDtypeStruct((rows, width // LANES, LANES), F32))
                out_specs.append(pl.BlockSpec((tm, width // LANES, LANES), lambda i: (i, 0, 0)))
                kk.append("h")
            else:
                out_shape.append(jax.ShapeDtypeStruct((rows, width), kind))
                out_specs.append(pl.BlockSpec((tm, width), lambda i: (i, 0)))
                kk.append("n")
        ksegs.append((start, width, tuple(kk)))
    return pl.pallas_call(
        functools.partial(_norm_proj_kernel, segs=tuple(ksegs)),
        grid=(rows // tm,),
        in_specs=[pl.BlockSpec((tm, d), lambda i: (i, 0)),
                  pl.BlockSpec((1, d), lambda i: (0, 0)),
                  pl.BlockSpec(w.shape, lambda i: (0, 0))],
        out_specs=out_specs,
        out_shape=out_shape,
        compiler_params=_cparams(1),
        name="norm_proj",
    )(x, g, w)


def _lambda_value(lam_ref, lam0):
    a = jnp.sum(lam_ref[0:1, :] * lam_ref[1:2, :], axis=-1, keepdims=True)
    b = jnp.sum(lam_ref[2:3, :] * lam_ref[3:4, :], axis=-1, keepdims=True)
    return jnp.exp(a) - jnp.exp(b) + lam0


def _stack_maps(q):
    lane = lax.broadcasted_iota(jnp.int32, q.shape, 1)
    zero = jnp.zeros_like(q)
    return jnp.concatenate([jnp.where(lane < A_HEAD_DIM, q, zero),
                            jnp.where(lane >= A_HEAD_DIM, q, zero)], axis=0)


ATTN_STRIP = 512
NT_DIMS = (((1,), (1,)), ((), ()))


def _attn_prompt_kernel(slopes_ref, lam_ref, g_ref, q_ref, k_ref, vt_ref, o_ref,
                        qq_sc, kaug_sc, sa_sc, sb_sc, m_sc, l_sc, acc_sc, *, tq, lam0):
    h = pl.program_id(0)
    i = pl.program_id(1)
    slope = slopes_ref[h]
    q0 = i * tq
    nt = NT_DIMS
    w = ATTN_STRIP

    qq_sc[:, :A_VDIM] = _stack_maps(q_ref[...])
    lane_q = lax.broadcasted_iota(jnp.int32, (2 * tq, LANES), 1)
    qq_sc[:, A_VDIM:] = jnp.where(lane_q < 3, 1.0, 0.0).astype(BF16)
    b = slope * lax.broadcasted_iota(jnp.int32, (tq, LANES), 0).astype(F32)
    b1 = b.astype(BF16).astype(F32)
    b2 = (b - b1).astype(BF16).astype(F32)
    b3 = b - b1 - b2
    lane_k = lax.broadcasted_iota(jnp.int32, (tq, LANES), 1)
    kaug_sc[...] = jnp.where(lane_k == 0, b1, jnp.where(
        lane_k == 1, b2, jnp.where(lane_k == 2, b3, 0.0))).astype(BF16)

    m_sc[...] = jnp.full(m_sc.shape, NEG, F32)
    l_sc[...] = jnp.zeros(l_sc.shape, F32)
    acc_sc[...] = jnp.zeros(acc_sc.shape, F32)

    groups = 8

    def key_reduce(op, x):
        part = op(x.reshape(groups, tq // groups, x.shape[-1]), axis=0)
        return op(part, axis=0, keepdims=True)

    def softmax_update(st, s, kappa, vtt):
        sl = slice(st * w, (st + 1) * w)
        m_prev = m_sc[:, sl]
        m_new = jnp.maximum(m_prev, key_reduce(jnp.max, s) + kappa)
        alpha = jnp.exp2(m_prev - m_new)
        p = jnp.exp2(s - (m_new - kappa))
        l_sc[:, sl] = alpha * l_sc[:, sl] + key_reduce(jnp.sum, p)
        acc_sc[:, sl] = alpha * acc_sc[:, sl] + jnp.dot(
            vtt, p.astype(BF16), preferred_element_type=F32)
        m_sc[:, sl] = m_new

    def produce(j, s_ref):
        k0 = pl.multiple_of(j * tq, tq)
        lhs = jnp.concatenate([k_ref[pl.ds(k0, tq), :], kaug_sc[...]], axis=1)
        s_ref[...] = lax.dot_general(lhs, qq_sc[...], nt, preferred_element_type=F32)

    def consume(j, s_ref):
        kappa = slope * (j * tq - q0).astype(F32)
        vtj = vt_ref[j]
        for st in range(2 * tq // w):
            softmax_update(st, s_ref[:, st * w:(st + 1) * w], kappa, vtj)

    def consume_diag(s_ref):
        vti = vt_ref[i]
        for st in range(2 * tq // w):
            c = lax.broadcasted_iota(jnp.int32, (tq, w), 0)
            r = lax.broadcasted_iota(jnp.int32, (tq, w), 1) + (st * w) % tq
            fix = slope * jnp.minimum(2 * (r - c), 0).astype(F32)
            visible = (c // CHUNK) <= (r // CHUNK)
            s = jnp.where(visible, s_ref[:, st * w:(st + 1) * w] + fix, NEG)
            softmax_update(st, s, 0.0, vti)

    produce(0, sa_sc)

    def two_tiles(t, carry):
        j = 2 * t
        produce(j + 1, sb_sc)
        consume(j, sa_sc)
        produce(j + 2, sa_sc)
        consume(j + 1, sb_sc)
        return carry

    lax.fori_loop(0, i // 2, two_tiles, 0)

    @pl.when(i % 2 == 0)
    def _():
        consume_diag(sa_sc)

    @pl.when(i % 2 == 1)
    def _():
        produce(i, sb_sc)
        consume(i - 1, sa_sc)
        consume_diag(sb_sc)

    lam = _lambda_value(lam_ref, lam0)
    inv_l = 1.0 / l_sc[...]
    acc = acc_sc[...] * inv_l
    o = acc[:, :tq] - lam * acc[:, tq:]
    ms = jnp.mean(o * o, axis=0, keepdims=True)
    o = (o * lax.rsqrt(ms + EPS)).T
    o_ref[...] = (o * g_ref[...] * (1.0 - lam0)).astype(o_ref.dtype)


def _attn_prompt(q, kb, vt, lam_rows, subln_g, slopes, lam0, tq):
    t = q.shape[0]
    nt = t // tq
    return pl.pallas_call(
        functools.partial(_attn_prompt_kernel, tq=tq, lam0=lam0),
        grid=(A_HEADS, nt),
        in_specs=[pl.BlockSpec(memory_space=pltpu.SMEM),
                  pl.BlockSpec((8, LANES), lambda h, i: (0, 0)),
                  pl.BlockSpec((1, A_VDIM), lambda h, i: (0, 0)),
                  pl.BlockSpec((tq, A_VDIM), lambda h, i: (i, h)),
                  pl.BlockSpec((t, A_VDIM), lambda h, i: (0, h)),
                  pl.BlockSpec((nt, A_VDIM, tq), lambda h, i: (0, h, 0))],
        out_specs=pl.BlockSpec((tq, A_VDIM), lambda h, i: (i, h)),
        out_shape=jax.ShapeDtypeStruct((t, A_HEADS * A_VDIM), BF16),
        scratch_shapes=[pltpu.VMEM((2 * tq, 2 * LANES), BF16),
                        pltpu.VMEM((tq, LANES), BF16),
                        pltpu.VMEM((tq, 2 * tq), F32),
                        pltpu.VMEM((tq, 2 * tq), F32),
                        pltpu.VMEM((1, 2 * tq), F32),
                        pltpu.VMEM((1, 2 * tq), F32),
                        pltpu.VMEM((A_VDIM, 2 * tq), F32)],
        compiler_params=_cparams(2),
        name="attn_prompt",
    )(slopes, lam_rows, subln_g, q, kb, vt)


SAMPLE_KEY_TILE = 1024


def _attn_sample_kernel(slopes_ref, lam_ref, g_ref, q_ref, kn_ref, vn_ref, kc_ref, vc_ref,
                        o_ref, m_sc, l_sc, acc_sc, *, past, lam0):
    t = pl.program_id(1)
    nt = pl.num_programs(1)
    tq = q_ref.shape[0]
    tk = kc_ref.shape[0] // A_HEADS

    @pl.when(t == 0)
    def _():
        m_sc[...] = jnp.full(m_sc.shape, NEG, F32)
        l_sc[...] = jnp.zeros(l_sc.shape, F32)
        acc_sc[...] = jnp.zeros(acc_sc.shape, F32)

    def update(h, s, v):
        m_prev = m_sc[h]
        m_new = jnp.maximum(m_prev, jnp.max(s, axis=-1, keepdims=True))
        alpha = jnp.exp2(m_prev - m_new)
        p = jnp.exp2(s - m_new)
        l_sc[h] = alpha * l_sc[h] + jnp.sum(p, axis=-1, keepdims=True)
        acc_sc[h] = alpha * acc_sc[h] + jnp.dot(p.astype(BF16), v, preferred_element_type=F32)
        m_sc[h] = m_new

    kpos = t * tk + lax.broadcasted_iota(jnp.int32, (1, tk), 1)
    rel = (kpos - past).astype(F32)
    qqs = []
    for h in range(A_HEADS):
        qq = _stack_maps(q_ref[:, h * A_VDIM:(h + 1) * A_VDIM])
        qqs.append(qq)
        head_rows = pl.ds(h, tk, stride=A_HEADS)
        s = lax.dot_general(qq, kc_ref[head_rows, :].astype(BF16), NT_DIMS,
                            preferred_element_type=F32)
        update(h, s + slopes_ref[h] * rel, vc_ref[head_rows, :].astype(BF16))

    @pl.when(t == nt - 1)
    def _():
        lam = _lambda_value(lam_ref, lam0)
        r = lax.broadcasted_iota(jnp.int32, (2 * tq, tq), 0)
        c = lax.broadcasted_iota(jnp.int32, (2 * tq, tq), 1)
        r = jnp.where(r >= tq, r - tq, r)
        own = jnp.minimum(c, 2 * r - c).astype(F32)
        for h in range(A_HEADS):
            cols = slice(h * A_VDIM, (h + 1) * A_VDIM)
            s = lax.dot_general(qqs[h], kn_ref[:, cols], NT_DIMS, preferred_element_type=F32)
            update(h, s + slopes_ref[h] * own, vn_ref[:, cols])
            acc = acc_sc[h] * (1.0 / l_sc[h])
            o = acc[:tq] - lam * acc[tq:]
            ms = jnp.mean(o * o, axis=-1, keepdims=True)
            o_ref[:, cols] = (o * lax.rsqrt(ms + EPS) * g_ref[...]
                              * (1.0 - lam0)).astype(o_ref.dtype)


def _attn_sample(q, kb, vb, cache_k, cache_v, layer, lam_rows, subln_g, slopes, lam0, tq):
    nb, past = cache_k.shape[1], cache_k.shape[2] // A_HEADS
    tk = min(SAMPLE_KEY_TILE, past)
    assert past % tk == 0
    aw = A_HEADS * A_VDIM
    rows = pl.BlockSpec((tq, aw), lambda b, t: (b, 0))
    cache = pl.BlockSpec((None, None, tk * A_HEADS, A_VDIM), lambda b, t: (layer, b, t, 0))
    return pl.pallas_call(
        functools.partial(_attn_sample_kernel, past=past, lam0=lam0),
        grid=(nb, past // tk),
        in_specs=[pl.BlockSpec(memory_space=pltpu.SMEM),
                  pl.BlockSpec((8, LANES), lambda b, t: (0, 0)),
                  pl.BlockSpec((1, A_VDIM), lambda b, t: (0, 0)),
                  rows, rows, rows, cache, cache],
        out_specs=rows,
        out_shape=jax.ShapeDtypeStruct((nb * tq, aw), BF16),
        scratch_shapes=[pltpu.VMEM((A_HEADS, 2 * tq, 1), F32),
                        pltpu.VMEM((A_HEADS, 2 * tq, 1), F32),
                        pltpu.VMEM((A_HEADS, 2 * tq, A_VDIM), F32)],
        compiler_params=_cparams(2),
        name="attn_sample",
    )(slopes, lam_rows, subln_g, q, kb, vb, cache_k, cache_v)


_HI = lax.Precision.HIGHEST
GDN_ROWS_PER_STEP = 4 * CHUNK


def _dot(a, b):
    return jnp.dot(a, b, preferred_element_type=F32)


def _split(a):
    hi = a.astype(BF16)
    return hi, (a - hi.astype(F32)).astype(BF16)


def _block_diag2(a, left):
    zero = jnp.zeros_like(a)
    return jnp.concatenate([jnp.where(left, a, zero), jnp.where(left, zero, a)], axis=0)


def _pair_prod(a_parts, b_parts, left):
    ah, bh = a_parts[0], b_parts[0]
    bdh = _block_diag2(bh, left)
    if len(a_parts) == 1 and len(b_parts) == 1:
        return _dot(ah, bdh)
    if len(a_parts) == 1:
        return _dot(jnp.concatenate([ah, ah], axis=1),
                    jnp.concatenate([bdh, _block_diag2(b_parts[1], left)], axis=0))
    r = _dot(jnp.concatenate([ah, a_parts[1]], axis=1), jnp.concatenate([bdh, bdh], axis=0))
    if len(b_parts) == 1:
        return r
    return r + _dot(ah, _block_diag2(b_parts[1], left))


def _inverse_masks(ri, cj):
    n = ri.shape[0]
    masks = [(ri // 2) == (cj // 2)]
    b = 2
    while b < n:
        same_big = ((ri // (2 * b)) == (cj // (2 * b))).astype(jnp.int32)
        same_small = ((ri // b) == (cj // b)).astype(jnp.int32)
        masks.append((same_big - same_small) > 0)
        b *= 2
    return masks


def _unit_lower_inverse(lows, masks, eye, left):
    xs = [eye - jnp.where(masks[0], low, 0.0) for low in lows]
    for mask in masks[1:]:
        xparts = [_split(x) for x in xs]
        ys = [_pair_prod((jnp.where(mask, low, 0.0).astype(BF16),), xp, left)
              for low, xp in zip(lows, xparts)]
        xs = [x - _pair_prod(xp, _split(y), left) for x, xp, y in zip(xs, xparts, ys)]
    return xs


def _gdn_kernel(x_ref, buf_ref, ba_ref, z_ref, s0_ref, cw_ref, par_ref, ng_ref,
                o_ref, s_out_ref, buf_out_ref, xp_sc, s_sc):
    c_idx = pl.program_id(1)
    nc = pl.num_programs(1)
    R = x_ref.shape[0]
    C = CHUNK
    n_sub = R // C
    nq = B_HEADS * B_DK
    cat = jnp.concatenate

    @pl.when(c_idx == 0)
    def _():
        xp_sc[HIST - (GDN_CONV - 1):HIST, :] = buf_ref[...]
        s_sc[...] = s0_ref[...].astype(F32)

    xp_sc[HIST:HIST + R, :] = x_ref[...]
    xp = xp_sc[...]
    conv = xp[HIST:] * cw_ref[GDN_CONV - 1:GDN_CONV, :]
    for j in range(GDN_CONV - 1):
        shifted = pltpu.roll(xp, GDN_CONV - 1 - j, axis=0)
        conv = conv + shifted[HIST:] * cw_ref[j:j + 1, :]
    conv = conv * jax.nn.sigmoid(conv)
    tail = xp_sc[R:R + HIST, :]
    xp_sc[0:HIST, :] = tail
    buf_out_ref[...] = tail[HIST - (GDN_CONV - 1):HIST, :]

    ba = ba_ref[...]
    lane = lax.broadcasted_iota(jnp.int32, ba.shape, 1)
    beta_all = jax.nn.sigmoid(ba)
    g_all = jnp.where((lane >= B_HEADS) & (lane < 2 * B_HEADS),
                      -jnp.exp(par_ref[0:1, :]) * jax.nn.softplus(ba + par_ref[1:2, :]), 0.0)
    rr = lax.broadcasted_iota(jnp.int32, (R, R), 0)
    cc = lax.broadcasted_iota(jnp.int32, (R, R), 1)
    same_chunk = ((rr // C) - (cc // C)) == 0
    tri = jnp.where(same_chunk, jnp.where(rr >= cc, 1.0, 0.0), 0.0).astype(F32)
    gcum_all = jnp.dot(tri, g_all, precision=_HI, preferred_element_type=F32)

    ri = lax.broadcasted_iota(jnp.int32, (C, 2 * C), 0)
    cl = lax.broadcasted_iota(jnp.int32, (C, 2 * C), 1)
    left = cl < C
    cj = jnp.where(left, cl, cl - C)
    incl = ri >= cj
    strict = ri > cj
    eye = jnp.where(ri == cj, 1.0, 0.0).astype(F32)
    inv_masks = _inverse_masks(ri, cj)
    zk = jnp.zeros((C, B_DK), F32)
    z2 = jnp.zeros((C, B_DV + B_DK), F32)
    zv = jnp.zeros((C, B_DV), F32)

    subs = range(n_sub)
    pairs = range(B_HEADS // 2)
    q, k, v, kb, beta, gc, eg, g_last, gcum_t = [], [], [], [], [], [], [], [], []
    for c in subs:
        rows = slice(c * C, (c + 1) * C)
        gsub = gcum_all[rows]
        gcum_t.append(cat([gsub, gsub], axis=0).T)
        qc, kc, vc, kbc, bc, gcc, egc = [], [], [], [], [], [], []
        for h in range(B_HEADS):
            qh = conv[rows, h * B_DK:(h + 1) * B_DK]
            kh = conv[rows, nq + h * B_DK:nq + (h + 1) * B_DK]
            qc.append(qh * lax.rsqrt(jnp.sum(qh * qh, axis=-1, keepdims=True) + EPS)
                      * (B_DK ** -0.5))
            kc.append(kh * lax.rsqrt(jnp.sum(kh * kh, axis=-1, keepdims=True) + EPS))
            vc.append(conv[rows, 2 * nq + h * B_DV:2 * nq + (h + 1) * B_DV])
            bc.append(beta_all[rows, h:h + 1])
            gcc.append(gsub[:, B_HEADS + h:B_HEADS + h + 1])
            kbc.append(kc[-1] * bc[-1])
            egc.append(jnp.exp(gcc[-1]))
        q.append(qc)
        k.append(kc)
        v.append(vc)
        kb.append(kbc)
        beta.append(bc)
        gc.append(gcc)
        eg.append(egc)
        g_last.append([g[C - 1:C, :] for g in gcc])

    lows, attns = [], []
    for c in subs:
        for p in pairs:
            a, b = 2 * p, 2 * p + 1
            g_col = jnp.where(left, gc[c][a], gc[c][b])
            g_row = jnp.where(left[0:1], gcum_t[c][B_HEADS + a:B_HEADS + a + 1, :],
                              gcum_t[c][B_HEADS + b:B_HEADS + b + 1, :])
            decay = jnp.where(incl, jnp.exp(jnp.where(incl, g_col - g_row, 0.0)), 0.0)
            lhs = cat([cat([kb[c][a], kb[c][b]], axis=1), cat([q[c][a], q[c][b]], axis=1)],
                      axis=0)
            rhs = cat([cat([k[c][a], zk], axis=1), cat([zk, k[c][b]], axis=1)], axis=0)
            m1 = lax.dot_general(lhs.astype(BF16), rhs.astype(BF16), NT_DIMS,
                                 preferred_element_type=F32)
            lows.append(jnp.where(strict, m1[:C] * decay, 0.0).astype(BF16).astype(F32))
            attns.append(m1[C:] * decay)

    tinvs = _unit_lower_inverse(lows, inv_masks, eye, left)

    sols = []
    for c in subs:
        for p in pairs:
            a, b = 2 * p, 2 * p + 1
            rhs_a = cat([v[c][a] * beta[c][a], kb[c][a] * eg[c][a]], axis=1)
            rhs_b = cat([v[c][b] * beta[c][b], kb[c][b] * eg[c][b]], axis=1)
            rhs_bd = cat([cat([rhs_a, z2], axis=1), cat([z2, rhs_b], axis=1)], axis=0)
            sols.append(_dot(tinvs[c * len(pairs) + p].astype(BF16), rhs_bd.astype(BF16)))

    s_cur = [s_sc[h] for h in range(B_HEADS)]
    for c in subs:
        rows = slice(c * C, (c + 1) * C)
        v_new, q_s = [], []
        for h in range(B_HEADS):
            sol = sols[c * len(pairs) + h // 2]
            base = (h % 2) * (B_DV + B_DK)
            u = sol[:, base:base + B_DV]
            w = sol[:, base + B_DV:base + B_DV + B_DK]
            m2 = _dot(cat([w, q[c][h] * eg[c][h]], axis=0).astype(BF16), s_cur[h].astype(BF16))
            v_new.append(u - m2[:C])
            q_s.append(m2[C:])
        o_pairs, upds = [], []
        for p in pairs:
            a, b = 2 * p, 2 * p + 1
            vn_bd = cat([cat([v_new[a], zv], axis=1), cat([zv, v_new[b]], axis=1)],
                        axis=0).astype(BF16)
            o_pairs.append(_dot(attns[c * len(pairs) + p].astype(BF16), vn_bd))
            kd_t = cat([k[c][a] * jnp.exp(g_last[c][a] - gc[c][a]),
                        k[c][b] * jnp.exp(g_last[c][b] - gc[c][b])], axis=0).T
            upds.append(_dot(kd_t.astype(BF16), vn_bd))
        for h in range(B_HEADS):
            lo = (h % 2) * B_DV
            s_cur[h] = s_cur[h] * jnp.exp(g_last[c][h]) + upds[h // 2][:, lo:lo + B_DV]
            o = q_s[h] + o_pairs[h // 2][:, lo:lo + B_DV]
            zh = z_ref[rows, h * B_DV:(h + 1) * B_DV]
            on = o * lax.rsqrt(jnp.mean(o * o, axis=-1, keepdims=True) + EPS) * ng_ref[...]
            o_ref[rows, h * B_DV:(h + 1) * B_DV] = (
                on * (zh * jax.nn.sigmoid(zh))).astype(o_ref.dtype)
    for h in range(B_HEADS):
        s_sc[h] = s_cur[h]

    @pl.when(c_idx == nc - 1)
    def _():
        s_out_ref[...] = s_sc[...].astype(s_out_ref.dtype)


def _gdn(qkv, ba, z, s0, buf, conv_w, par, norm_g, n_seq, rows_per_seq, rows_per_step):
    assert rows_per_seq % rows_per_step == 0 and rows_per_step % CHUNK == 0
    nc = rows_per_seq // rows_per_step
    nqkv = qkv.shape[1]
    row = lambda b, c: (b * nc + c, 0)
    return pl.pallas_call(
        _gdn_kernel,
        grid=(n_seq, nc),
        in_specs=[pl.BlockSpec((rows_per_step, nqkv), row),
                  pl.BlockSpec((None, GDN_CONV - 1, nqkv), lambda b, c: (b, 0, 0)),
                  pl.BlockSpec((rows_per_step, LANES), row),
                  pl.BlockSpec((rows_per_step, B_HEADS * B_DV), row),
                  pl.BlockSpec((None, B_HEADS, B_DK, B_DV), lambda b, c: (b, 0, 0, 0)),
                  pl.BlockSpec((GDN_CONV, nqkv), lambda b, c: (0, 0)),
                  pl.BlockSpec((8, LANES), lambda b, c: (0, 0)),
                  pl.BlockSpec((1, B_DV), lambda b, c: (0, 0))],
        out_specs=[pl.BlockSpec((rows_per_step, B_HEADS * B_DV), row),
                   pl.BlockSpec((None, B_HEADS, B_DK, B_DV), lambda b, c: (b, 0, 0, 0)),
                   pl.BlockSpec((None, GDN_CONV - 1, nqkv), lambda b, c: (b, 0, 0))],
        out_shape=[jax.ShapeDtypeStruct((n_seq * rows_per_seq, B_HEADS * B_DV), BF16),
                   jax.ShapeDtypeStruct((n_seq, B_HEADS, B_DK, B_DV), F32),
                   jax.ShapeDtypeStruct((n_seq, GDN_CONV - 1, nqkv), F32)],
        scratch_shapes=[pltpu.VMEM((HIST + rows_per_step, nqkv), F32),
                        pltpu.VMEM((B_HEADS, B_DK, B_DV), F32)],
        compiler_params=_cparams(2),
        name="gdn",
    )(qkv, buf, ba, z, s0, conv_w, par, norm_g)


def _merge_kernel(x_ref, oa_ref, ob_ref, ga_ref, gb_ref, wa_ref, wb_ref, wo_ref, h_ref):
    a = jnp.dot(oa_ref[...], wa_ref[...], preferred_element_type=F32)
    b = jnp.dot(ob_ref[...], wb_ref[...], preferred_element_type=F32)
    merged = jax.nn.sigmoid(ga_ref[...]) * a + jax.nn.sigmoid(gb_ref[...]) * b
    h_ref[...] = x_ref[...] + jnp.dot(merged.astype(BF16), wo_ref[...],
                                      preferred_element_type=F32)


def _merge(x, oa, ob, ga, gb, wa, wb, wo, tm):
    rows, d = x.shape
    rowspec = pl.BlockSpec((tm, d), lambda i: (i, 0))
    wspec = pl.BlockSpec((d, d), lambda i: (0, 0))
    return pl.pallas_call(
        _merge_kernel,
        grid=(rows // tm,),
        in_specs=[rowspec] * 5 + [wspec] * 3,
        out_specs=rowspec,
        out_shape=jax.ShapeDtypeStruct((rows, d), F32),
        compiler_params=_cparams(1),
        name="merge",
    )(x, oa, ob, ga, gb, wa, wb, wo)


def _ffn_kernel(h_ref, g_ref, buf_g_ref, buf_v_ref, wg_ref, wv_ref, cwg_ref, cwv_ref,
                cbg_ref, cbv_ref, wd_ref, fg_ref, y_ref, nbuf_g_ref, nbuf_v_ref,
                hn_sc, ug_sc, uv_sc, acc_sc, *, tiles_per_seq, final):
    i = pl.program_id(0)
    j = pl.program_id(1)
    nj = pl.num_programs(1)
    tm = h_ref.shape[0]
    first = (i % tiles_per_seq) == 0
    nh = FFN_CONV - 1

    @pl.when(j == 0)
    def _():
        x = h_ref[...]
        ms = jnp.mean(x * x, axis=-1, keepdims=True)
        hn_sc[...] = (x * lax.rsqrt(ms + EPS) * g_ref[...]).astype(BF16)
        acc_sc[...] = jnp.zeros(acc_sc.shape, F32)

    def conv_half(u_sc, w_ref, cw_ref, cb_ref, buf_ref, nbuf_ref):
        @pl.when(first)
        def _():
            u_sc[j, HIST - nh:HIST, :] = buf_ref[...]
        u_sc[j, HIST:HIST + tm, :] = jnp.dot(hn_sc[...], w_ref[...], preferred_element_type=F32)
        out = u_sc[j, HIST:HIST + tm, :] * cw_ref[nh:nh + 1, :] + cb_ref[...]
        for t in range(nh):
            off = HIST - nh + t
            out = out + u_sc[j, off:off + tm, :] * cw_ref[t:t + 1, :]
        tail = u_sc[j, tm:tm + HIST, :]
        u_sc[j, 0:HIST, :] = tail
        nbuf_ref[j] = tail[HIST - nh:HIST, :]
        return out

    gate = conv_half(ug_sc, wg_ref, cwg_ref, cbg_ref, buf_g_ref, nbuf_g_ref)
    val = conv_half(uv_sc, wv_ref, cwv_ref, cbv_ref, buf_v_ref, nbuf_v_ref)
    act = (gate * jax.nn.sigmoid(gate) * val).astype(BF16)
    acc_sc[...] += jnp.dot(act, wd_ref[...], preferred_element_type=F32)

    @pl.when(j == nj - 1)
    def _():
        y = h_ref[...] + acc_sc[...]
        if final:
            ms = jnp.mean(y * y, axis=-1, keepdims=True)
            y = y * lax.rsqrt(ms + EPS) * fg_ref[...]
        y_ref[...] = y


def _ffn(h, g, buf, w_up, conv_w, conv_b, w_down, final_g, n_seq, rows_per_seq, tm, cw, final):
    rows, d = h.shape
    d_ff = w_down.shape[0]
    assert d_ff % cw == 0 and rows_per_seq % tm == 0
    nj = d_ff // cw
    tps = rows_per_seq // tm
    nh = FFN_CONV - 1
    gate_col = lambda i, j: (0, j)
    val_col = lambda i, j: (0, nj + j)
    seq_gate = lambda i, j: (i // tps, 0, j)
    seq_val = lambda i, j: (i // tps, 0, nj + j)
    y, nbg, nbv = pl.pallas_call(
        functools.partial(_ffn_kernel, tiles_per_seq=tps, final=final),
        grid=(rows // tm, nj),
        in_specs=[pl.BlockSpec((tm, d), lambda i, j: (i, 0)),
                  pl.BlockSpec((1, d), lambda i, j: (0, 0)),
                  pl.BlockSpec((None, nh, cw), seq_gate),
                  pl.BlockSpec((None, nh, cw), seq_val),
                  pl.BlockSpec((d, cw), gate_col),
                  pl.BlockSpec((d, cw), val_col),
                  pl.BlockSpec((FFN_CONV, cw), gate_col),
                  pl.BlockSpec((FFN_CONV, cw), val_col),
                  pl.BlockSpec((1, cw), gate_col),
                  pl.BlockSpec((1, cw), val_col),
                  pl.BlockSpec((cw, d), lambda i, j: (j, 0)),
                  pl.BlockSpec((1, d), lambda i, j: (0, 0))],
        out_specs=[pl.BlockSpec((tm, d), lambda i, j: (i, 0)),
                   pl.BlockSpec((None, nj, nh, cw), lambda i, j: (i // tps, 0, 0, 0)),
                   pl.BlockSpec((None, nj, nh, cw), lambda i, j: (i // tps, 0, 0, 0))],
        out_shape=[jax.ShapeDtypeStruct((rows, d), F32),
                   jax.ShapeDtypeStruct((n_seq, nj, nh, cw), F32),
                   jax.ShapeDtypeStruct((n_seq, nj, nh, cw), F32)],
        scratch_shapes=[pltpu.VMEM((tm, d), BF16),
                        pltpu.VMEM((nj, HIST + tm, cw), F32),
                        pltpu.VMEM((nj, HIST + tm, cw), F32),
                        pltpu.VMEM((tm, d), F32)],
        compiler_params=_cparams(2),
        name="ffn",
    )(h, g, buf, buf, w_up, w_up, conv_w, conv_w, conv_b, conv_b, w_down, final_g)
    unchunk = lambda a: a.transpose(0, 2, 1, 3).reshape(n_seq, nh, d_ff)
    return y, jnp.concatenate([unchunk(nbg), unchunk(nbv)], axis=-1)


def _prep_layer_weights(l, P):
    w_in = P["w_in"][l]
    d = w_in.shape[0]
    aw = A_HEADS * A_VDIM
    nqkv = B_HEADS * (2 * B_DK + B_DV)
    c_qkvb = 3 * aw
    c_beta = c_qkvb + nqkv
    c_z = c_beta + 2 * B_HEADS
    bw = B_HEADS * B_DV
    w_q = w_in[:, :aw] * (A_HEAD_DIM ** -0.5 * LOG2E)
    w_a = jnp.concatenate([w_q, w_in[:, aw:3 * aw]], axis=1).astype(BF16)
    w_b = jnp.concatenate([w_in[:, c_qkvb:c_beta + 2 * B_HEADS],
                           jnp.zeros((d, LANES - 2 * B_HEADS), F32)], axis=1).astype(BF16)
    w_c = w_in[:, c_z:c_z + bw + 2 * d].astype(BF16)
    lam_rows = jnp.zeros((8, LANES), F32)
    for r, name in enumerate(("lambda_q1", "lambda_k1", "lambda_q2", "lambda_k2")):
        lam_rows = lam_rows.at[r, :A_HEAD_DIM].set(P[name][l])
    par = jnp.zeros((8, LANES), F32)
    par = par.at[0, B_HEADS:2 * B_HEADS].set(P["gdn_a_log"][l])
    par = par.at[1, B_HEADS:2 * B_HEADS].set(P["gdn_dt_bias"][l])
    return dict(
        w_a=w_a, w_b=w_b, w_c=w_c, lam_rows=lam_rows, par=par,
        norm_mix_g=P["norm_mix_g"][l][None, :],
        subln_g=P["subln_g"][l][None, :],
        gdn_conv_w=P["gdn_conv_w"][l],
        gdn_norm_g=P["gdn_norm_g"][l][None, :],
        w_proj_a=P["w_proj_a"][l].astype(BF16),
        w_proj_b=P["w_proj_b"][l].astype(BF16),
        w_out=P["w_out"][l].astype(BF16),
        norm_ffn_g=P["norm_ffn_g"][l][None, :],
        w_up=P["w_up"][l].astype(BF16),
        ffn_conv_w=P["ffn_conv_w"][l],
        ffn_conv_b=P["ffn_conv_b"][l][None, :],
        w_down=P["w_down"][l].astype(BF16),
        final_g=P["final_norm_g"][None, :],
    )


def _layer(x, l, W, n_seq, rows_per_seq, cache_k, cache_v, s0, gbuf, fbuf, final, tm, tq):
    rows, d = x.shape
    tm = min(tm, rows)
    aw = A_HEADS * A_VDIM
    nqkv = B_HEADS * (2 * B_DK + B_DV)
    lam0 = _lambda_init(l)
    slopes = (2.0 ** (-8.0 * jnp.arange(1, A_HEADS + 1, dtype=F32) / A_HEADS)) * LOG2E
    prompt = cache_k is None

    v_kinds = ("h", "t") if prompt else ("h", BF16)
    q, k, kb, v, vx = _norm_proj(
        x, W["norm_mix_g"], W["w_a"],
        ((0, aw, (BF16,)), (aw, aw, ("h", BF16)), (2 * aw, aw, v_kinds)), tm if not prompt else tq)
    qkvb, ba = _norm_proj(x, W["norm_mix_g"], W["w_b"],
                          ((0, nqkv, (F32,)), (nqkv, LANES, (F32,))), tm)
    z, ga, gb = _norm_proj(x, W["norm_mix_g"], W["w_c"],
                           ((0, aw, (F32,)), (aw, d, (F32,)), (aw + d, d, (F32,))), tm)

    if prompt:
        o_a = _attn_prompt(q, kb, vx, W["lam_rows"], W["subln_g"], slopes, lam0, tq)
    else:
        o_a = _attn_sample(q, kb, vx, cache_k, cache_v, l, W["lam_rows"], W["subln_g"],
                           slopes, lam0, rows_per_seq)

    o_b, s_new, gbuf_new = _gdn(qkvb, ba, z, s0, gbuf, W["gdn_conv_w"], W["par"],
                                W["gdn_norm_g"], n_seq, rows_per_seq,
                                min(GDN_ROWS_PER_STEP, rows_per_seq))

    h = _merge(x, o_a, o_b, ga, gb, W["w_proj_a"], W["w_proj_b"], W["w_out"], tm)
    d_ff = W["w_down"].shape[0]
    y, fbuf_new = _ffn(h, W["norm_ffn_g"], fbuf, W["w_up"], W["ffn_conv_w"], W["ffn_conv_b"],
                       W["w_down"], W["final_g"], n_seq, rows_per_seq,
                       min(tm, rows_per_seq), d_ff // 2, final)
    return y, k, v, s_new, gbuf_new, fbuf_new


def kernel(x_prompt, x_sample, cache_k, cache_v, state_gdn, state_gdn_conv, state_ffn_conv, norm_mix_g, w_in, lambda_q1, lambda_k1, lambda_q2, lambda_k2, subln_g, gdn_conv_w, gdn_a_log, gdn_dt_bias, gdn_norm_g, w_proj_a, w_proj_b, w_out, norm_ffn_g, w_up, ffn_conv_w, ffn_conv_b, w_down, final_norm_g):
    P = dict(norm_mix_g=norm_mix_g, w_in=w_in, lambda_q1=lambda_q1, lambda_k1=lambda_k1,
             lambda_q2=lambda_q2, lambda_k2=lambda_k2, subln_g=subln_g, gdn_conv_w=gdn_conv_w,
             gdn_a_log=gdn_a_log, gdn_dt_bias=gdn_dt_bias, gdn_norm_g=gdn_norm_g,
             w_proj_a=w_proj_a, w_proj_b=w_proj_b, w_out=w_out, norm_ffn_g=norm_ffn_g,
             w_up=w_up, ffn_conv_w=ffn_conv_w, ffn_conv_b=ffn_conv_b, w_down=w_down,
             final_norm_g=final_norm_g)
    depth = w_in.shape[0]
    bp, tp, d = x_prompt.shape
    bs, ts, _ = x_sample.shape
    past = cache_k.shape[2]
    assert ts == CHUNK and past % CHUNK == 0 and tp % CHUNK == 0
    nqkv = state_gdn_conv.shape[-1]
    d_ff2 = state_ffn_conv.shape[-1]
    weights = [_prep_layer_weights(l, P) for l in range(depth)]
    ck = cache_k.reshape(depth, bs, past * A_HEADS, A_VDIM)
    cv = cache_v.reshape(depth, bs, past * A_HEADS, A_VDIM)

    tq = min(512, tp)
    tm = 256

    def run(x3, nseq, rps, sample):
        assert sample or nseq == 1
        x = x3.reshape(nseq * rps, d)
        ks, vs, ss, gcs, fcs = [], [], [], [], []
        for l in range(depth):
            if sample:
                s0, gbuf, fbuf = state_gdn[l], state_gdn_conv[l], state_ffn_conv[l]
                c_k, c_v = ck, cv
            else:
                s0 = jnp.zeros((nseq, B_HEADS, B_DK, B_DV), F32)
                gbuf = jnp.zeros((nseq, GDN_CONV - 1, nqkv), F32)
                fbuf = jnp.zeros((nseq, FFN_CONV - 1, d_ff2), F32)
                c_k = c_v = None
            x, k, v, s_new, g_new, f_new = _layer(
                x, l, weights[l], nseq, rps, c_k, c_v, s0, gbuf, fbuf,
                l == depth - 1, tm, tq)
            ks.append(k.reshape(nseq, rps, A_HEADS, A_VDIM))
            vs.append(v.reshape(nseq, rps, A_HEADS, A_VDIM))
            ss.append(s_new)
            gcs.append(g_new)
            fcs.append(f_new)
        return (x.reshape(nseq, rps, d), jnp.stack(ks), jnp.stack(vs), jnp.stack(ss),
                jnp.stack(gcs), jnp.stack(fcs))

    y_p, k_p, v_p, s_p, gc_p, fc_p = run(x_prompt, bp, tp, False)
    y_s, k_s, v_s, s_s, gc_s, fc_s = run(x_sample, bs, ts, True)
    return (y_p, y_s, k_p, v_p, s_p, gc_p, fc_p, k_s, v_s, s_s, gc_s, fc_s)
```

```python
import functools
import math

import jax
import jax.numpy as jnp
from jax import lax
from jax.experimental import pallas as pl
from jax.experimental.pallas import tpu as pltpu

F32 = jnp.float32
BF16 = jnp.bfloat16

EPS = 1e-6
NEG = -1e30
LOG2E = 1.4426950408889634

CHUNK = 64
A_HEADS = 8
A_HEAD_DIM = 64
A_VDIM = 128
B_HEADS = 8
B_DK = 128
B_DV = 128
GDN_CONV = 4
FFN_CONV = 3
LANES = 128
HIST = 8

VMEM_LIMIT = 56 * 1024 * 1024


def _cparams(n_axes):
    return pltpu.CompilerParams(
        dimension_semantics=("arbitrary",) * n_axes,
        vmem_limit_bytes=VMEM_LIMIT)


def _lambda_init(layer):
    return 0.8 - 0.6 * math.exp(-0.3 * layer)


def _norm_proj_kernel(x_ref, g_ref, w_ref, *refs, segs, n_carried):
    out_refs = refs[n_carried:]
    x = x_ref[...]
    ms = jnp.mean(x * x, axis=-1, keepdims=True)
    xn = (x * lax.rsqrt(ms + EPS) * g_ref[...]).astype(BF16)
    k = 0
    for start, width, kinds in segs:
        w = w_ref[:, start:start + width]
        r = None
        for kind in kinds:
            o_ref = out_refs[k]
            k += 1
            if r is None:
                r = jnp.dot(xn, w, preferred_element_type=F32)
            if kind == "t":
                o_ref[...] = r.T.astype(o_ref.dtype)
            elif kind == "h":
                for hd in range(width // LANES):
                    o_ref[:, hd, :] = r[:, hd * LANES:(hd + 1) * LANES]
            else:
                o_ref[...] = r.astype(o_ref.dtype)


def _norm_proj(x, g, w, segs, tm, layer=0, depth=1, carried=()):
    rows, d = x.shape
    assert rows % tm == 0
    out_shape, out_specs, ksegs, h_outs = [], [], [], []
    for start, width, kinds in segs:
        kk = []
        for kind in kinds:
            if kind == "t":
                out_shape.append(jax.ShapeDtypeStruct((rows // tm, width, tm), BF16))
                out_specs.append(pl.BlockSpec((None, width, tm), lambda i: (i, 0, 0)))
                kk.append("t")
            elif kind == "h":
                h_outs.append(len(out_shape))
                out_shape.append(jax.ShapeDtypeStruct(
                    (depth, rows, width // LANES, LANES), F32))
                out_specs.append(pl.BlockSpec((None, tm, width // LANES, LANES),
                                              lambda i: (layer, i, 0, 0)))
                kk.append("h")
            else:
                out_shape.append(jax.ShapeDtypeStruct((rows, width), kind))
                out_specs.append(pl.BlockSpec((tm, width), lambda i: (i, 0)))
                kk.append("n")
        ksegs.append((start, width, tuple(kk)))
    assert len(carried) in (0, len(h_outs))
    return pl.pallas_call(
        functools.partial(_norm_proj_kernel, segs=tuple(ksegs), n_carried=len(carried)),
        grid=(rows // tm,),
        in_specs=[pl.BlockSpec((tm, d), lambda i: (i, 0)),
                  pl.BlockSpec((1, d), lambda i: (0, 0)),
                  pl.BlockSpec(w.shape, lambda i: (0, 0))]
                 + [pl.BlockSpec(memory_space=pl.ANY)] * len(carried),
        out_specs=out_specs,
        out_shape=out_shape,
        input_output_aliases={3 + n: h_outs[n] for n in range(len(carried))},
        compiler_params=_cparams(1),
        name="norm_proj",
    )(x, g, w, *carried)


def _lambda_value(lam_ref, lam0):
    a = jnp.sum(lam_ref[0:1, :] * lam_ref[1:2, :], axis=-1, keepdims=True)
    b = jnp.sum(lam_ref[2:3, :] * lam_ref[3:4, :], axis=-1, keepdims=True)
    return jnp.exp(a) - jnp.exp(b) + lam0


def _stack_maps(q):
    lane = lax.broadcasted_iota(jnp.int32, q.shape, 1)
    zero = jnp.zeros_like(q)
    return jnp.concatenate([jnp.where(lane < A_HEAD_DIM, q, zero),
                            jnp.where(lane >= A_HEAD_DIM, q, zero)], axis=0)


ATTN_STRIP = 512
NT_DIMS = (((1,), (1,)), ((), ()))


def _attn_prompt_kernel(slopes_ref, lam_ref, g_ref, q_ref, k_ref, vt_ref, o_ref,
                        qq_sc, kaug_sc, dfix_sc, sa_sc, sb_sc, m_sc, l_sc, acc_sc, *, tq, lam0):
    h = pl.program_id(0)
    i = pl.program_id(1)
    slope = slopes_ref[h]
    q0 = i * tq
    nt = NT_DIMS
    w = ATTN_STRIP

    qq_sc[:, :A_VDIM] = _stack_maps(q_ref[...])
    lane_q = lax.broadcasted_iota(jnp.int32, (2 * tq, LANES), 1)
    qq_sc[:, A_VDIM:] = jnp.where(lane_q < 3, 1.0, 0.0).astype(BF16)

    @pl.when(i == 0)
    def _():
        b = slope * lax.broadcasted_iota(jnp.int32, (tq, LANES), 0).astype(F32)
        b1 = b.astype(BF16).astype(F32)
        b2 = (b - b1).astype(BF16).astype(F32)
        b3 = b - b1 - b2
        lane_k = lax.broadcasted_iota(jnp.int32, (tq, LANES), 1)
        kaug_sc[...] = jnp.where(lane_k == 0, b1, jnp.where(
            lane_k == 1, b2, jnp.where(lane_k == 2, b3, 0.0))).astype(BF16)
        for st in range(2 * tq // w):
            c = lax.broadcasted_iota(jnp.int32, (tq, w), 0)
            r = lax.broadcasted_iota(jnp.int32, (tq, w), 1) + (st * w) % tq
            fix = slope * jnp.minimum(2 * (r - c), 0).astype(F32)
            dfix_sc[:, st * w:(st + 1) * w] = jnp.where((c // CHUNK) <= (r // CHUNK), fix, NEG)

    m_sc[...] = jnp.full(m_sc.shape, NEG, F32)
    l_sc[...] = jnp.zeros(l_sc.shape, F32)
    acc_sc[...] = jnp.zeros(acc_sc.shape, F32)

    groups = 8

    def key_reduce(op, x):
        part = op(x.reshape(groups, tq // groups, x.shape[-1]), axis=0)
        return op(part, axis=0, keepdims=True)

    def softmax_update(st, s, kappa, vtt):
        sl = slice(st * w, (st + 1) * w)
        m_prev = m_sc[:, sl]
        m_new = jnp.maximum(m_prev, key_reduce(jnp.max, s) + kappa)
        alpha = jnp.exp2(m_prev - m_new)
        p = jnp.exp2(s - (m_new - kappa))
        l_sc[:, sl] = alpha * l_sc[:, sl] + key_reduce(jnp.sum, p)
        acc_sc[:, sl] = alpha * acc_sc[:, sl] + jnp.dot(
            vtt, p.astype(BF16), preferred_element_type=F32)
        m_sc[:, sl] = m_new

    def produce(j, s_ref):
        k0 = pl.multiple_of(j * tq, tq)
        lhs = jnp.concatenate([k_ref[pl.ds(k0, tq), :], kaug_sc[...]], axis=1)
        s_ref[...] = lax.dot_general(lhs, qq_sc[...], nt, preferred_element_type=F32)

    def consume(j, s_ref):
        kappa = slope * (j * tq - q0).astype(F32)
        vtj = vt_ref[j]
        for st in range(2 * tq // w):
            softmax_update(st, s_ref[:, st * w:(st + 1) * w], kappa, vtj)

    def consume_diag(s_ref):
        vti = vt_ref[i]
        for st in range(2 * tq // w):
            sl = slice(st * w, (st + 1) * w)
            softmax_update(st, s_ref[:, sl] + dfix_sc[:, sl], 0.0, vti)

    produce(0, sa_sc)

    def two_tiles(j):
        produce(j + 1, sb_sc)
        consume(j, sa_sc)
        produce(j + 2, sa_sc)
        consume(j + 1, sb_sc)

    def four_tiles(t, carry):
        two_tiles(4 * t)
        two_tiles(4 * t + 2)
        return carry

    lax.fori_loop(0, i // 4, four_tiles, 0)

    @pl.when(i % 4 >= 2)
    def _():
        two_tiles((i // 4) * 4)

    @pl.when(i % 2 == 0)
    def _():
        consume_diag(sa_sc)

    @pl.when(i % 2 == 1)
    def _():
        produce(i, sb_sc)
        consume(i - 1, sa_sc)
        consume_diag(sb_sc)

    lam = _lambda_value(lam_ref, lam0)
    inv_l = 1.0 / l_sc[...]
    acc = acc_sc[...] * inv_l
    o = acc[:, :tq] - lam * acc[:, tq:]
    ms = jnp.mean(o * o, axis=0, keepdims=True)
    o = (o * lax.rsqrt(ms + EPS)).T
    o_ref[...] = (o * g_ref[...] * (1.0 - lam0)).astype(o_ref.dtype)


def _attn_prompt(q, kb, vt, lam_rows, subln_g, slopes, lam0, tq):
    t = q.shape[0]
    nt = t // tq
    return pl.pallas_call(
        functools.partial(_attn_prompt_kernel, tq=tq, lam0=lam0),
        grid=(A_HEADS, nt),
        in_specs=[pl.BlockSpec(memory_space=pltpu.SMEM),
                  pl.BlockSpec((8, LANES), lambda h, i: (0, 0)),
                  pl.BlockSpec((1, A_VDIM), lambda h, i: (0, 0)),
                  pl.BlockSpec((tq, A_VDIM), lambda h, i: (i, h)),
                  pl.BlockSpec((t, A_VDIM), lambda h, i: (0, h)),
                  pl.BlockSpec((nt, A_VDIM, tq), lambda h, i: (0, h, 0))],
        out_specs=pl.BlockSpec((tq, A_VDIM), lambda h, i: (i, h)),
        out_shape=jax.ShapeDtypeStruct((t, A_HEADS * A_VDIM), BF16),
        scratch_shapes=[pltpu.VMEM((2 * tq, 2 * LANES), BF16),
                        pltpu.VMEM((tq, LANES), BF16),
                        pltpu.VMEM((tq, 2 * tq), F32),
                        pltpu.VMEM((tq, 2 * tq), F32),
                        pltpu.VMEM((tq, 2 * tq), F32),
                        pltpu.VMEM((1, 2 * tq), F32),
                        pltpu.VMEM((1, 2 * tq), F32),
                        pltpu.VMEM((A_VDIM, 2 * tq), F32)],
        compiler_params=_cparams(2),
        name="attn_prompt",
    )(slopes, lam_rows, subln_g, q, kb, vt)


SAMPLE_KEY_TILE = 1024


def _attn_sample_kernel(slopes_ref, lam_ref, g_ref, q_ref, kn_ref, vn_ref, kc_ref, vc_ref,
                        o_ref, m_sc, l_sc, acc_sc, *, past, lam0):
    t = pl.program_id(1)
    nt = pl.num_programs(1)
    tq = q_ref.shape[0]
    tk = kc_ref.shape[0] // A_HEADS

    @pl.when(t == 0)
    def _():
        m_sc[...] = jnp.full(m_sc.shape, NEG, F32)
        l_sc[...] = jnp.zeros(l_sc.shape, F32)
        acc_sc[...] = jnp.zeros(acc_sc.shape, F32)

    def update(h, s, v):
        m_prev = m_sc[h]
        m_new = jnp.maximum(m_prev, jnp.max(s, axis=-1, keepdims=True))
        alpha = jnp.exp2(m_prev - m_new)
        p = jnp.exp2(s - m_new)
        l_sc[h] = alpha * l_sc[h] + jnp.sum(p, axis=-1, keepdims=True)
        acc_sc[h] = alpha * acc_sc[h] + jnp.dot(p.astype(BF16), v, preferred_element_type=F32)
        m_sc[h] = m_new

    kpos = t * tk + lax.broadcasted_iota(jnp.int32, (1, tk), 1)
    rel = (kpos - past).astype(F32)
    qqs = []
    for h in range(A_HEADS):
        qq = _stack_maps(q_ref[:, h * A_VDIM:(h + 1) * A_VDIM])
        qqs.append(qq)
        head_rows = pl.ds(h, tk, stride=A_HEADS)
        s = lax.dot_general(qq, kc_ref[head_rows, :].astype(BF16), NT_DIMS,
                            preferred_element_type=F32)
        update(h, s + slopes_ref[h] * rel, vc_ref[head_rows, :].astype(BF16))

    @pl.when(t == nt - 1)
    def _():
        lam = _lambda_value(lam_ref, lam0)
        r = lax.broadcasted_iota(jnp.int32, (2 * tq, tq), 0)
        c = lax.broadcasted_iota(jnp.int32, (2 * tq, tq), 1)
        r = jnp.where(r >= tq, r - tq, r)
        own = jnp.minimum(c, 2 * r - c).astype(F32)
        for h in range(A_HEADS):
            cols = slice(h * A_VDIM, (h + 1) * A_VDIM)
            s = lax.dot_general(qqs[h], kn_ref[:, cols], NT_DIMS, preferred_element_type=F32)
            update(h, s + slopes_ref[h] * own, vn_ref[:, cols])
            acc = acc_sc[h] * (1.0 / l_sc[h])
            o = acc[:tq] - lam * acc[tq:]
            ms = jnp.mean(o * o, axis=-1, keepdims=True)
            o_ref[:, cols] = (o * lax.rsqrt(ms + EPS) * g_ref[...]
                              * (1.0 - lam0)).astype(o_ref.dtype)


def _attn_sample(q, kb, vb, cache_k, cache_v, layer, lam_rows, subln_g, slopes, lam0, tq):
    nb, past = cache_k.shape[1], cache_k.shape[2] // A_HEADS
    tk = min(SAMPLE_KEY_TILE, past)
    assert past % tk == 0
    aw = A_HEADS * A_VDIM
    rows = pl.BlockSpec((tq, aw), lambda b, t: (b, 0))
    cache = pl.BlockSpec((None, None, tk * A_HEADS, A_VDIM), lambda b, t: (layer, b, t, 0))
    return pl.pallas_call(
        functools.partial(_attn_sample_kernel, past=past, lam0=lam0),
        grid=(nb, past // tk),
        in_specs=[pl.BlockSpec(memory_space=pltpu.SMEM),
                  pl.BlockSpec((8, LANES), lambda b, t: (0, 0)),
                  pl.BlockSpec((1, A_VDIM), lambda b, t: (0, 0)),
                  rows, rows, rows, cache, cache],
        out_specs=rows,
        out_shape=jax.ShapeDtypeStruct((nb * tq, aw), BF16),
        scratch_shapes=[pltpu.VMEM((A_HEADS, 2 * tq, 1), F32),
                        pltpu.VMEM((A_HEADS, 2 * tq, 1), F32),
                        pltpu.VMEM((A_HEADS, 2 * tq, A_VDIM), F32)],
        compiler_params=_cparams(2),
        name="attn_sample",
    )(slopes, lam_rows, subln_g, q, kb, vb, cache_k, cache_v)


_HI = lax.Precision.HIGHEST
GDN_ROWS_PER_STEP = 4 * CHUNK


def _dot(a, b):
    return jnp.dot(a, b, preferred_element_type=F32)


def _split(a):
    hi = a.astype(BF16)
    return hi, (a - hi.astype(F32)).astype(BF16)


def _block_diag2(a, left):
    zero = jnp.zeros_like(a)
    return jnp.concatenate([jnp.where(left, a, zero), jnp.where(left, zero, a)], axis=0)


def _pair_prod(a_parts, b_parts, left):
    ah, bh = a_parts[0], b_parts[0]
    bdh = _block_diag2(bh, left)
    if len(a_parts) == 1 and len(b_parts) == 1:
        return _dot(ah, bdh)
    if len(a_parts) == 1:
        return _dot(jnp.concatenate([ah, ah], axis=1),
                    jnp.concatenate([bdh, _block_diag2(b_parts[1], left)], axis=0))
    r = _dot(jnp.concatenate([ah, a_parts[1]], axis=1), jnp.concatenate([bdh, bdh], axis=0))
    if len(b_parts) == 1:
        return r
    return r + _dot(ah, _block_diag2(b_parts[1], left))


def _inverse_masks(ri, cj):
    n = ri.shape[0]
    masks = [(ri // 2) == (cj // 2)]
    b = 2
    while b < n:
        same_big = ((ri // (2 * b)) == (cj // (2 * b))).astype(jnp.int32)
        same_small = ((ri // b) == (cj // b)).astype(jnp.int32)
        masks.append((same_big - same_small) > 0)
        b *= 2
    return masks


def _unit_lower_inverse(lows, masks, eye, left):
    xs = [eye - jnp.where(masks[0], low, 0.0) for low in lows]
    for mask in masks[1:]:
        xparts = [_split(x) for x in xs]
        ys = [_pair_prod((jnp.where(mask, low, 0.0).astype(BF16),), xp, left)
              for low, xp in zip(lows, xparts)]
        xs = [x - _pair_prod(xp, _split(y), left) for x, xp, y in zip(xs, xparts, ys)]
    return xs


def _gdn_kernel(x_ref, buf_ref, ba_ref, z_ref, s0_ref, cw_ref, par_ref, ng_ref,
                o_ref, s_out_ref, buf_out_ref, xp_sc, s_sc):
    c_idx = pl.program_id(1)
    nc = pl.num_programs(1)
    R = x_ref.shape[0]
    C = CHUNK
    n_sub = R // C
    nq = B_HEADS * B_DK
    cat = jnp.concatenate

    @pl.when(c_idx == 0)
    def _():
        xp_sc[HIST - (GDN_CONV - 1):HIST, :] = buf_ref[...]
        s_sc[...] = s0_ref[...].astype(F32)

    xp_sc[HIST:HIST + R, :] = x_ref[...]
    xp = xp_sc[...]
    conv = xp[HIST:] * cw_ref[GDN_CONV - 1:GDN_CONV, :]
    for j in range(GDN_CONV - 1):
        shifted = pltpu.roll(xp, GDN_CONV - 1 - j, axis=0)
        conv = conv + shifted[HIST:] * cw_ref[j:j + 1, :]
    conv = conv * jax.nn.sigmoid(conv)
    tail = xp_sc[R:R + HIST, :]
    xp_sc[0:HIST, :] = tail
    buf_out_ref[...] = tail[HIST - (GDN_CONV - 1):HIST, :]

    ba = ba_ref[...]
    lane = lax.broadcasted_iota(jnp.int32, ba.shape, 1)
    beta_all = jax.nn.sigmoid(ba)
    g_all = jnp.where((lane >= B_HEADS) & (lane < 2 * B_HEADS),
                      -jnp.exp(par_ref[0:1, :]) * jax.nn.softplus(ba + par_ref[1:2, :]), 0.0)
    rr = lax.broadcasted_iota(jnp.int32, (R, R), 0)
    cc = lax.broadcasted_iota(jnp.int32, (R, R), 1)
    same_chunk = ((rr // C) - (cc // C)) == 0
    tri = jnp.where(same_chunk, jnp.where(rr >= cc, 1.0, 0.0), 0.0).astype(F32)
    gcum_all = jnp.dot(tri, g_all, precision=_HI, preferred_element_type=F32)

    ri = lax.broadcasted_iota(jnp.int32, (C, 2 * C), 0)
    cl = lax.broadcasted_iota(jnp.int32, (C, 2 * C), 1)
    left = cl < C
    cj = jnp.where(left, cl, cl - C)
    incl = ri >= cj
    strict = ri > cj
    eye = jnp.where(ri == cj, 1.0, 0.0).astype(F32)
    inv_masks = _inverse_masks(ri, cj)
    zk = jnp.zeros((C, B_DK), F32)
    z2 = jnp.zeros((C, B_DV + B_DK), F32)
    zv = jnp.zeros((C, B_DV), F32)

    subs = range(n_sub)
    pairs = range(B_HEADS // 2)
    q, k, v, kb, beta, gc, eg, g_last, gcum_t = [], [], [], [], [], [], [], [], []
    for c in subs:
        rows = slice(c * C, (c + 1) * C)
        gsub = gcum_all[rows]
        gcum_t.append(cat([gsub, gsub], axis=0).T)
        qc, kc, vc, kbc, bc, gcc, egc = [], [], [], [], [], [], []
        for h in range(B_HEADS):
            qh = conv[rows, h * B_DK:(h + 1) * B_DK]
            kh = conv[rows, nq + h * B_DK:nq + (h + 1) * B_DK]
            qc.append(qh * lax.rsqrt(jnp.sum(qh * qh, axis=-1, keepdims=True) + EPS)
                      * (B_DK ** -0.5))
            kc.append(kh * lax.rsqrt(jnp.sum(kh * kh, axis=-1, keepdims=True) + EPS))
            vc.append(conv[rows, 2 * nq + h * B_DV:2 * nq + (h + 1) * B_DV])
            bc.append(beta_all[rows, h:h + 1])
            gcc.append(gsub[:, B_HEADS + h:B_HEADS + h + 1])
            kbc.append(kc[-1] * bc[-1])
            egc.append(jnp.exp(gcc[-1]))
        q.append(qc)
        k.append(kc)
        v.append(vc)
        kb.append(kbc)
        beta.append(bc)
        gc.append(gcc)
        eg.append(egc)
        g_last.append([g[C - 1:C, :] for g in gcc])

    lows, attns = [], []
    for c in subs:
        for p in pairs:
            a, b = 2 * p, 2 * p + 1
            g_col = jnp.where(left, gc[c][a], gc[c][b])
            g_row = jnp.where(left[0:1], gcum_t[c][B_HEADS + a:B_HEADS + a + 1, :],
                              gcum_t[c][B_HEADS + b:B_HEADS + b + 1, :])
            decay = jnp.where(incl, jnp.exp(jnp.where(incl, g_col - g_row, 0.0)), 0.0)
            lhs = cat([cat([kb[c][a], kb[c][b]], axis=1), cat([q[c][a], q[c][b]], axis=1)],
                      axis=0)
            rhs = cat([cat([k[c][a], zk], axis=1), cat([zk, k[c][b]], axis=1)], axis=0)
            m1 = lax.dot_general(lhs.astype(BF16), rhs.astype(BF16), NT_DIMS,
                                 preferred_element_type=F32)
            lows.append(jnp.where(strict, m1[:C] * decay, 0.0).astype(BF16).astype(F32))
            attns.append(m1[C:] * decay)

    tinvs = _unit_lower_inverse(lows, inv_masks, eye, left)

    sols = []
    for c in subs:
        for p in pairs:
            a, b = 2 * p, 2 * p + 1
            rhs_a = cat([v[c][a] * beta[c][a], kb[c][a] * eg[c][a]], axis=1)
            rhs_b = cat([v[c][b] * beta[c][b], kb[c][b] * eg[c][b]], axis=1)
            rhs_bd = cat([cat([rhs_a, z2], axis=1), cat([z2, rhs_b], axis=1)], axis=0)
            sols.append(_dot(tinvs[c * len(pairs) + p].astype(BF16), rhs_bd.astype(BF16)))

    s_cur = [s_sc[h] for h in range(B_HEADS)]
    for c in subs:
        rows = slice(c * C, (c + 1) * C)
        v_new, q_s = [], []
        for h in range(B_HEADS):
            sol = sols[c * len(pairs) + h // 2]
            base = (h % 2) * (B_DV + B_DK)
            u = sol[:, base:base + B_DV]
            w = sol[:, base + B_DV:base + B_DV + B_DK]
            m2 = _dot(cat([w, q[c][h] * eg[c][h]], axis=0).astype(BF16), s_cur[h].astype(BF16))
            v_new.append(u - m2[:C])
            q_s.append(m2[C:])
        o_pairs, upds = [], []
        for p in pairs:
            a, b = 2 * p, 2 * p + 1
            vn_bd = cat([cat([v_new[a], zv], axis=1), cat([zv, v_new[b]], axis=1)],
                        axis=0).astype(BF16)
            o_pairs.append(_dot(attns[c * len(pairs) + p].astype(BF16), vn_bd))
            kd_t = cat([k[c][a] * jnp.exp(g_last[c][a] - gc[c][a]),
                        k[c][b] * jnp.exp(g_last[c][b] - gc[c][b])], axis=0).T
            upds.append(_dot(kd_t.astype(BF16), vn_bd))
        for h in range(B_HEADS):
            lo = (h % 2) * B_DV
            s_cur[h] = s_cur[h] * jnp.exp(g_last[c][h]) + upds[h // 2][:, lo:lo + B_DV]
            o = q_s[h] + o_pairs[h // 2][:, lo:lo + B_DV]
            zh = z_ref[rows, h * B_DV:(h + 1) * B_DV]
            on = o * lax.rsqrt(jnp.mean(o * o, axis=-1, keepdims=True) + EPS) * ng_ref[...]
            o_ref[rows, h * B_DV:(h + 1) * B_DV] = (
                on * (zh * jax.nn.sigmoid(zh))).astype(o_ref.dtype)
    for h in range(B_HEADS):
        s_sc[h] = s_cur[h]

    @pl.when(c_idx == nc - 1)
    def _():
        s_out_ref[...] = s_sc[...].astype(s_out_ref.dtype)


def _gdn(qkv, ba, z, s0, buf, conv_w, par, norm_g, n_seq, rows_per_seq, rows_per_step):
    assert rows_per_seq % rows_per_step == 0 and rows_per_step % CHUNK == 0
    nc = rows_per_seq // rows_per_step
    nqkv = qkv.shape[1]
    row = lambda b, c: (b * nc + c, 0)
    return pl.pallas_call(
        _gdn_kernel,
        grid=(n_seq, nc),
        in_specs=[pl.BlockSpec((rows_per_step, nqkv), row),
                  pl.BlockSpec((None, GDN_CONV - 1, nqkv), lambda b, c: (b, 0, 0)),
                  pl.BlockSpec((rows_per_step, LANES), row),
                  pl.BlockSpec((rows_per_step, B_HEADS * B_DV), row),
                  pl.BlockSpec((None, B_HEADS, B_DK, B_DV), lambda b, c: (b, 0, 0, 0)),
                  pl.BlockSpec((GDN_CONV, nqkv), lambda b, c: (0, 0)),
                  pl.BlockSpec((8, LANES), lambda b, c: (0, 0)),
                  pl.BlockSpec((1, B_DV), lambda b, c: (0, 0))],
        out_specs=[pl.BlockSpec((rows_per_step, B_HEADS * B_DV), row),
                   pl.BlockSpec((None, B_HEADS, B_DK, B_DV), lambda b, c: (b, 0, 0, 0)),
                   pl.BlockSpec((None, GDN_CONV - 1, nqkv), lambda b, c: (b, 0, 0))],
        out_shape=[jax.ShapeDtypeStruct((n_seq * rows_per_seq, B_HEADS * B_DV), BF16),
                   jax.ShapeDtypeStruct((n_seq, B_HEADS, B_DK, B_DV), F32),
                   jax.ShapeDtypeStruct((n_seq, GDN_CONV - 1, nqkv), F32)],
        scratch_shapes=[pltpu.VMEM((HIST + rows_per_step, nqkv), F32),
                        pltpu.VMEM((B_HEADS, B_DK, B_DV), F32)],
        compiler_params=_cparams(2),
        name="gdn",
    )(qkv, buf, ba, z, s0, conv_w, par, norm_g)


def _merge_kernel(x_ref, oa_ref, ob_ref, ga_ref, gb_ref, wa_ref, wb_ref, wo_ref, h_ref):
    a = jnp.dot(oa_ref[...], wa_ref[...], preferred_element_type=F32)
    b = jnp.dot(ob_ref[...], wb_ref[...], preferred_element_type=F32)
    merged = jax.nn.sigmoid(ga_ref[...]) * a + jax.nn.sigmoid(gb_ref[...]) * b
    h_ref[...] = x_ref[...] + jnp.dot(merged.astype(BF16), wo_ref[...],
                                      preferred_element_type=F32)


def _merge(x, oa, ob, ga, gb, wa, wb, wo, tm):
    rows, d = x.shape
    rowspec = pl.BlockSpec((tm, d), lambda i: (i, 0))
    wspec = pl.BlockSpec((d, d), lambda i: (0, 0))
    return pl.pallas_call(
        _merge_kernel,
        grid=(rows // tm,),
        in_specs=[rowspec] * 5 + [wspec] * 3,
        out_specs=rowspec,
        out_shape=jax.ShapeDtypeStruct((rows, d), F32),
        compiler_params=_cparams(1),
        name="merge",
    )(x, oa, ob, ga, gb, wa, wb, wo)


FFN_ROW_TILE = 512


def _ffn_kernel(h_ref, g_ref, buf_g_ref, buf_v_ref, wg_ref, wv_ref, cwg_ref, cwv_ref,
                cbg_ref, cbv_ref, wd_ref, fg_ref, y_ref, nbuf_g_ref, nbuf_v_ref,
                hn_sc, ug_sc, uv_sc, acc_sc, *, tiles_per_seq, final):
    i = pl.program_id(0)
    j = pl.program_id(1)
    nj = pl.num_programs(1)
    tm = h_ref.shape[0]
    first = (i % tiles_per_seq) == 0
    nh = FFN_CONV - 1

    @pl.when(j == 0)
    def _():
        x = h_ref[...]
        ms = jnp.mean(x * x, axis=-1, keepdims=True)
        hn_sc[...] = (x * lax.rsqrt(ms + EPS) * g_ref[...]).astype(BF16)
        acc_sc[...] = jnp.zeros(acc_sc.shape, F32)

    def conv_half(u_sc, w_ref, cw_ref, cb_ref, buf_ref, nbuf_ref):
        @pl.when(first)
        def _():
            u_sc[j, HIST - nh:HIST, :] = buf_ref[...]
        u_sc[j, HIST:HIST + tm, :] = jnp.dot(hn_sc[...], w_ref[...], preferred_element_type=F32)
        out = u_sc[j, HIST:HIST + tm, :] * cw_ref[nh:nh + 1, :] + cb_ref[...]
        for t in range(nh):
            off = HIST - nh + t
            out = out + u_sc[j, off:off + tm, :] * cw_ref[t:t + 1, :]
        tail = u_sc[j, tm:tm + HIST, :]
        u_sc[j, 0:HIST, :] = tail
        nbuf_ref[j] = tail[HIST - nh:HIST, :]
        return out

    gate = conv_half(ug_sc, wg_ref, cwg_ref, cbg_ref, buf_g_ref, nbuf_g_ref)
    val = conv_half(uv_sc, wv_ref, cwv_ref, cbv_ref, buf_v_ref, nbuf_v_ref)
    act = (gate * jax.nn.sigmoid(gate) * val).astype(BF16)
    acc_sc[...] += jnp.dot(act, wd_ref[...], preferred_element_type=F32)

    @pl.when(j == nj - 1)
    def _():
        y = h_ref[...] + acc_sc[...]
        if final:
            ms = jnp.mean(y * y, axis=-1, keepdims=True)
            y = y * lax.rsqrt(ms + EPS) * fg_ref[...]
        y_ref[...] = y


def _ffn(h, g, buf, w_up, conv_w, conv_b, w_down, final_g, n_seq, rows_per_seq, tm, cw, final):
    rows, d = h.shape
    d_ff = w_down.shape[0]
    assert d_ff % cw == 0 and rows_per_seq % tm == 0
    nj = d_ff // cw
    tps = rows_per_seq // tm
    nh = FFN_CONV - 1
    gate_col = lambda i, j: (0, j)
    val_col = lambda i, j: (0, nj + j)
    seq_gate = lambda i, j: (i // tps, 0, j)
    seq_val = lambda i, j: (i // tps, 0, nj + j)
    y, nbg, nbv = pl.pallas_call(
        functools.partial(_ffn_kernel, tiles_per_seq=tps, final=final),
        grid=(rows // tm, nj),
        in_specs=[pl.BlockSpec((tm, d), lambda i, j: (i, 0)),
                  pl.BlockSpec((1, d), lambda i, j: (0, 0)),
                  pl.BlockSpec((None, nh, cw), seq_gate),
                  pl.BlockSpec((None, nh, cw), seq_val),
                  pl.BlockSpec((d, cw), gate_col),
                  pl.BlockSpec((d, cw), val_col),
                  pl.BlockSpec((FFN_CONV, cw), gate_col),
                  pl.BlockSpec((FFN_CONV, cw), val_col),
                  pl.BlockSpec((1, cw), gate_col),
                  pl.BlockSpec((1, cw), val_col),
                  pl.BlockSpec((cw, d), lambda i, j: (j, 0)),
                  pl.BlockSpec((1, d), lambda i, j: (0, 0))],
        out_specs=[pl.BlockSpec((tm, d), lambda i, j: (i, 0)),
                   pl.BlockSpec((None, nj, nh, cw), lambda i, j: (i // tps, 0, 0, 0)),
                   pl.BlockSpec((None, nj, nh, cw), lambda i, j: (i // tps, 0, 0, 0))],
        out_shape=[jax.ShapeDtypeStruct((rows, d), F32),
                   jax.ShapeDtypeStruct((n_seq, nj, nh, cw), F32),
                   jax.ShapeDtypeStruct((n_seq, nj, nh, cw), F32)],
        scratch_shapes=[pltpu.VMEM((tm, d), BF16),
                        pltpu.VMEM((nj, HIST + tm, cw), F32),
                        pltpu.VMEM((nj, HIST + tm, cw), F32),
                        pltpu.VMEM((tm, d), F32)],
        compiler_params=_cparams(2),
        name="ffn",
    )(h, g, buf, buf, w_up, w_up, conv_w, conv_w, conv_b, conv_b, w_down, final_g)
    unchunk = lambda a: a.transpose(0, 2, 1, 3).reshape(n_seq, nh, d_ff)
    return y, jnp.concatenate([unchunk(nbg), unchunk(nbv)], axis=-1)


def _prep_layer_weights(l, P):
    w_in = P["w_in"][l]
    d = w_in.shape[0]
    aw = A_HEADS * A_VDIM
    nqkv = B_HEADS * (2 * B_DK + B_DV)
    c_qkvb = 3 * aw
    c_beta = c_qkvb + nqkv
    c_z = c_beta + 2 * B_HEADS
    bw = B_HEADS * B_DV
    w_q = w_in[:, :aw] * (A_HEAD_DIM ** -0.5 * LOG2E)
    w_a = jnp.concatenate([w_q, w_in[:, aw:3 * aw]], axis=1).astype(BF16)
    w_b = jnp.concatenate([w_in[:, c_qkvb:c_beta + 2 * B_HEADS],
                           jnp.zeros((d, LANES - 2 * B_HEADS), F32)], axis=1).astype(BF16)
    w_c = w_in[:, c_z:c_z + bw + 2 * d].astype(BF16)
    lam_rows = jnp.zeros((8, LANES), F32)
    for r, name in enumerate(("lambda_q1", "lambda_k1", "lambda_q2", "lambda_k2")):
        lam_rows = lam_rows.at[r, :A_HEAD_DIM].set(P[name][l])
    par = jnp.zeros((8, LANES), F32)
    par = par.at[0, B_HEADS:2 * B_HEADS].set(P["gdn_a_log"][l])
    par = par.at[1, B_HEADS:2 * B_HEADS].set(P["gdn_dt_bias"][l])
    return dict(
        w_a=w_a, w_b=w_b, w_c=w_c, lam_rows=lam_rows, par=par,
        norm_mix_g=P["norm_mix_g"][l][None, :],
        subln_g=P["subln_g"][l][None, :],
        gdn_conv_w=P["gdn_conv_w"][l],
        gdn_norm_g=P["gdn_norm_g"][l][None, :],
        w_proj_a=P["w_proj_a"][l].astype(BF16),
        w_proj_b=P["w_proj_b"][l].astype(BF16),
        w_out=P["w_out"][l].astype(BF16),
        norm_ffn_g=P["norm_ffn_g"][l][None, :],
        w_up=P["w_up"][l].astype(BF16),
        ffn_conv_w=P["ffn_conv_w"][l],
        ffn_conv_b=P["ffn_conv_b"][l][None, :],
        w_down=P["w_down"][l].astype(BF16),
        final_g=P["final_norm_g"][None, :],
    )


def _layer(x, l, depth, kv_prev, W, n_seq, rows_per_seq, cache_k, cache_v, s0, gbuf, fbuf,
           final, tm, tq):
    rows, d = x.shape
    tm = min(tm, rows)
    aw = A_HEADS * A_VDIM
    nqkv = B_HEADS * (2 * B_DK + B_DV)
    lam0 = _lambda_init(l)
    slopes = (2.0 ** (-8.0 * jnp.arange(1, A_HEADS + 1, dtype=F32) / A_HEADS)) * LOG2E
    prompt = cache_k is None

    v_kinds = ("h", "t") if prompt else ("h", BF16)
    q, k, kb, v, vx = _norm_proj(
        x, W["norm_mix_g"], W["w_a"],
        ((0, aw, (BF16,)), (aw, aw, ("h", BF16)), (2 * aw, aw, v_kinds)), tm if not prompt else tq,
        layer=l, depth=depth, carried=kv_prev)
    qkvb, ba = _norm_proj(x, W["norm_mix_g"], W["w_b"],
                          ((0, nqkv, (F32,)), (nqkv, LANES, (F32,))), tm)
    z, ga, gb = _norm_proj(x, W["norm_mix_g"], W["w_c"],
                           ((0, aw, (F32,)), (aw, d, (F32,)), (aw + d, d, (F32,))), tm)

    if prompt:
        o_a = _attn_prompt(q, kb, vx, W["lam_rows"], W["subln_g"], slopes, lam0, tq)
    else:
        o_a = _attn_sample(q, kb, vx, cache_k, cache_v, l, W["lam_rows"], W["subln_g"],
                           slopes, lam0, rows_per_seq)

    o_b, s_new, gbuf_new = _gdn(qkvb, ba, z, s0, gbuf, W["gdn_conv_w"], W["par"],
                                W["gdn_norm_g"], n_seq, rows_per_seq,
                                min(GDN_ROWS_PER_STEP, rows_per_seq))

    h = _merge(x, o_a, o_b, ga, gb, W["w_proj_a"], W["w_proj_b"], W["w_out"], tm)
    d_ff = W["w_down"].shape[0]
    y, fbuf_new = _ffn(h, W["norm_ffn_g"], fbuf, W["w_up"], W["ffn_conv_w"], W["ffn_conv_b"],
                       W["w_down"], W["final_g"], n_seq, rows_per_seq,
                       min(FFN_ROW_TILE, rows_per_seq), d_ff // 2, final)
    return y, k, v, s_new, gbuf_new, fbuf_new


def kernel(x_prompt, x_sample, cache_k, cache_v, state_gdn, state_gdn_conv, state_ffn_conv, norm_mix_g, w_in, lambda_q1, lambda_k1, lambda_q2, lambda_k2, subln_g, gdn_conv_w, gdn_a_log, gdn_dt_bias, gdn_norm_g, w_proj_a, w_proj_b, w_out, norm_ffn_g, w_up, ffn_conv_w, ffn_conv_b, w_down, final_norm_g):
    P = dict(norm_mix_g=norm_mix_g, w_in=w_in, lambda_q1=lambda_q1, lambda_k1=lambda_k1,
             lambda_q2=lambda_q2, lambda_k2=lambda_k2, subln_g=subln_g, gdn_conv_w=gdn_conv_w,
             gdn_a_log=gdn_a_log, gdn_dt_bias=gdn_dt_bias, gdn_norm_g=gdn_norm_g,
             w_proj_a=w_proj_a, w_proj_b=w_proj_b, w_out=w_out, norm_ffn_g=norm_ffn_g,
             w_up=w_up, ffn_conv_w=ffn_conv_w, ffn_conv_b=ffn_conv_b, w_down=w_down,
             final_norm_g=final_norm_g)
    depth = w_in.shape[0]
    bp, tp, d = x_prompt.shape
    bs, ts, _ = x_sample.shape
    past = cache_k.shape[2]
    assert ts == CHUNK and past % CHUNK == 0 and tp % CHUNK == 0
    nqkv = state_gdn_conv.shape[-1]
    d_ff2 = state_ffn_conv.shape[-1]
    weights = [_prep_layer_weights(l, P) for l in range(depth)]
    ck = cache_k.reshape(depth, bs, past * A_HEADS, A_VDIM)
    cv = cache_v.reshape(depth, bs, past * A_HEADS, A_VDIM)

    tq = min(512, tp)
    tm = 256

    def run(x3, nseq, rps, sample):
        assert sample or nseq == 1
        x = x3.reshape(nseq * rps, d)
        kv, ss, gcs, fcs = (), [], [], []
        for l in range(depth):
            if sample:
                s0, gbuf, fbuf = state_gdn[l], state_gdn_conv[l], state_ffn_conv[l]
                c_k, c_v = ck, cv
            else:
                s0 = jnp.zeros((nseq, B_HEADS, B_DK, B_DV), F32)
                gbuf = jnp.zeros((nseq, GDN_CONV - 1, nqkv), F32)
                fbuf = jnp.zeros((nseq, FFN_CONV - 1, d_ff2), F32)
                c_k = c_v = None
            x, k_all, v_all, s_new, g_new, f_new = _layer(
                x, l, depth, kv, weights[l], nseq, rps, c_k, c_v, s0, gbuf, fbuf,
                l == depth - 1, tm, tq)
            kv = (k_all, v_all)
            ss.append(s_new)
            gcs.append(g_new)
            fcs.append(f_new)
        kv_shape = (depth, nseq, rps, A_HEADS, A_VDIM)
        return (x.reshape(nseq, rps, d), kv[0].reshape(kv_shape), kv[1].reshape(kv_shape),
                jnp.stack(ss), jnp.stack(gcs), jnp.stack(fcs))

    y_p, k_p, v_p, s_p, gc_p, fc_p = run(x_prompt, bp, tp, False)
    y_s, k_s, v_s, s_s, gc_s, fc_s = run(x_sample, bs, ts, True)
    return (y_p, y_s, k_p, v_p, s_p, gc_p, fc_p, k_s, v_s, s_s, gc_s, fc_s)
```

```python
import functools
import math

import jax
import jax.numpy as jnp
from jax import lax
from jax.experimental import pallas as pl
from jax.experimental.pallas import tpu as pltpu

F32 = jnp.float32
BF16 = jnp.bfloat16

EPS = 1e-6
NEG = -1e30
LOG2E = 1.4426950408889634

CHUNK = 64
A_HEADS = 8
A_HEAD_DIM = 64
A_VDIM = 128
B_HEADS = 8
B_DK = 128
B_DV = 128
GDN_CONV = 4
FFN_CONV = 3
LANES = 128
HIST = 8

VMEM_LIMIT = 56 * 1024 * 1024


def _cparams(n_axes):
    return pltpu.CompilerParams(
        dimension_semantics=("arbitrary",) * n_axes,
        vmem_limit_bytes=VMEM_LIMIT)


def _lambda_init(layer):
    return 0.8 - 0.6 * math.exp(-0.3 * layer)


def _norm_proj_kernel(x_ref, g_ref, w_ref, *refs, segs, n_carried):
    out_refs = refs[n_carried:]
    x = x_ref[...]
    ms = jnp.mean(x * x, axis=-1, keepdims=True)
    xn = (x * lax.rsqrt(ms + EPS) * g_ref[...]).astype(BF16)
    k = 0
    for start, width, kinds in segs:
        w = w_ref[:, start:start + width]
        r = None
        for kind in kinds:
            o_ref = out_refs[k]
            k += 1
            if r is None:
                r = jnp.dot(xn, w, preferred_element_type=F32)
            if kind == "t":
                o_ref[...] = r.T.astype(o_ref.dtype)
            elif kind == "h":
                for hd in range(width // LANES):
                    o_ref[:, hd, :] = r[:, hd * LANES:(hd + 1) * LANES]
            else:
                o_ref[...] = r.astype(o_ref.dtype)


def _norm_proj(x, g, w, segs, tm, layer=0, depth=1, carried=()):
    rows, d = x.shape
    assert rows % tm == 0
    out_shape, out_specs, ksegs, h_outs = [], [], [], []
    for start, width, kinds in segs:
        kk = []
        for kind in kinds:
            if kind == "t":
                out_shape.append(jax.ShapeDtypeStruct((rows // tm, width, tm), BF16))
                out_specs.append(pl.BlockSpec((None, width, tm), lambda i: (i, 0, 0)))
                kk.append("t")
            elif kind == "h":
                h_outs.append(len(out_shape))
                out_shape.append(jax.ShapeDtypeStruct(
                    (depth, rows, width // LANES, LANES), F32))
                out_specs.append(pl.BlockSpec((None, tm, width // LANES, LANES),
                                              lambda i: (layer, i, 0, 0)))
                kk.append("h")
            else:
                out_shape.append(jax.ShapeDtypeStruct((rows, width), kind))
                out_specs.append(pl.BlockSpec((tm, width), lambda i: (i, 0)))
                kk.append("n")
        ksegs.append((start, width, tuple(kk)))
    assert len(carried) in (0, len(h_outs))
    return pl.pallas_call(
        functools.partial(_norm_proj_kernel, segs=tuple(ksegs), n_carried=len(carried)),
        grid=(rows // tm,),
        in_specs=[pl.BlockSpec((tm, d), lambda i: (i, 0)),
                  pl.BlockSpec((1, d), lambda i: (0, 0)),
                  pl.BlockSpec(w.shape, lambda i: (0, 0))]
                 + [pl.BlockSpec(memory_space=pl.ANY)] * len(carried),
        out_specs=out_specs,
        out_shape=out_shape,
        input_output_aliases={3 + n: h_outs[n] for n in range(len(carried))},
        compiler_params=_cparams(1),
        name="norm_proj",
    )(x, g, w, *carried)


def _lambda_value(lam_ref, lam0):
    a = jnp.sum(lam_ref[0:1, :] * lam_ref[1:2, :], axis=-1, keepdims=True)
    b = jnp.sum(lam_ref[2:3, :] * lam_ref[3:4, :], axis=-1, keepdims=True)
    return jnp.exp(a) - jnp.exp(b) + lam0


def _stack_maps(q):
    lane = lax.broadcasted_iota(jnp.int32, q.shape, 1)
    zero = jnp.zeros_like(q)
    return jnp.concatenate([jnp.where(lane < A_HEAD_DIM, q, zero),
                            jnp.where(lane >= A_HEAD_DIM, q, zero)], axis=0)


NT_DIMS = (((1,), (1,)), ((), ()))
ATTN_KEY_TILE = 512
ATTN_QUERY_BLOCK = 1024


def _attn_prompt_kernel(slopes_ref, lam_ref, g_ref, q_ref, k_ref, vt_ref, o_ref,
                        qq_sc, kaug_sc, dfix_sc, sa_sc, sb_sc, m_sc, l_sc, acc_sc,
                        *, tq, tk, lam0):
    h = pl.program_id(0)
    i = pl.program_id(1)
    slope = slopes_ref[h]
    q0 = i * tq
    nt = NT_DIMS
    w = tk
    nd = tq // tk
    n_off = i * nd
    n_strips = 2 * tq // w

    qq_sc[:, :A_VDIM] = _stack_maps(q_ref[...])
    lane_q = lax.broadcasted_iota(jnp.int32, (2 * tq, LANES), 1)
    qq_sc[:, A_VDIM:] = jnp.where(lane_q < 3, 1.0, 0.0).astype(BF16)

    @pl.when(i == 0)
    def _():
        b = slope * lax.broadcasted_iota(jnp.int32, (tk, LANES), 0).astype(F32)
        b1 = b.astype(BF16).astype(F32)
        b2 = (b - b1).astype(BF16).astype(F32)
        b3 = b - b1 - b2
        lane_k = lax.broadcasted_iota(jnp.int32, (tk, LANES), 1)
        kaug_sc[...] = jnp.where(lane_k == 0, b1, jnp.where(
            lane_k == 1, b2, jnp.where(lane_k == 2, b3, 0.0))).astype(BF16)
        c = lax.broadcasted_iota(jnp.int32, (tk, w), 0)
        r = lax.broadcasted_iota(jnp.int32, (tk, w), 1)
        fix = slope * jnp.minimum(2 * (r - c), 0).astype(F32)
        dfix_sc[...] = jnp.where((c // CHUNK) <= (r // CHUNK), fix, NEG)

    m_sc[...] = jnp.full(m_sc.shape, NEG, F32)
    l_sc[...] = jnp.zeros(l_sc.shape, F32)
    acc_sc[...] = jnp.zeros(acc_sc.shape, F32)

    groups = 8

    def key_reduce(op, x):
        part = op(x.reshape(groups, tk // groups, x.shape[-1]), axis=0)
        return op(part, axis=0, keepdims=True)

    def softmax_update(st, s_ref, kappa, vtt):
        sl = slice(st * w, (st + 1) * w)
        m_prev = m_sc[:, sl]
        m_new = jnp.maximum(m_prev, key_reduce(jnp.max, s_ref[:, sl]) + kappa)
        alpha = jnp.exp2(m_prev - m_new)
        p = jnp.exp2(s_ref[:, sl] - (m_new - kappa))
        l_sc[:, sl] = alpha * l_sc[:, sl] + key_reduce(jnp.sum, p)
        acc_sc[:, sl] = alpha * acc_sc[:, sl] + jnp.dot(
            vtt, p.astype(BF16), preferred_element_type=F32)
        m_sc[:, sl] = m_new

    def produce(j, s_ref):
        k0 = pl.multiple_of(j * tk, tk)
        lhs = jnp.concatenate([k_ref[pl.ds(k0, tk), :], kaug_sc[...]], axis=1)
        s_ref[...] = lax.dot_general(lhs, qq_sc[...], nt, preferred_element_type=F32)

    def consume(j, s_ref):
        kappa = slope * (j * tk - q0).astype(F32)
        vtj = vt_ref[j]
        for st in range(n_strips):
            softmax_update(st, s_ref, kappa, vtj)

    def consume_overlap(d, s_ref):
        kappa = slope * (d * tk)
        vtd = vt_ref[n_off + d]
        for st in range(n_strips):
            q_lo = (st * w) % tq
            if q_lo < d * tk:
                continue
            if q_lo == d * tk:
                sl = slice(st * w, (st + 1) * w)
                s_ref[:, sl] = s_ref[:, sl] + dfix_sc[...]
            softmax_update(st, s_ref, kappa, vtd)

    def overlap_tiles(first_ref, second_ref):
        for d in range(nd):
            cur, nxt = (first_ref, second_ref) if d % 2 == 0 else (second_ref, first_ref)
            if d + 1 < nd:
                produce(n_off + d + 1, nxt)
            consume_overlap(d, cur)

    produce(0, sa_sc)

    def two_tiles(j):
        produce(j + 1, sb_sc)
        consume(j, sa_sc)
        produce(j + 2, sa_sc)
        consume(j + 1, sb_sc)

    def four_tiles(t, carry):
        two_tiles(4 * t)
        two_tiles(4 * t + 2)
        return carry

    lax.fori_loop(0, n_off // 4, four_tiles, 0)

    @pl.when(n_off % 4 >= 2)
    def _():
        two_tiles((n_off // 4) * 4)

    if nd % 2 == 0:
        overlap_tiles(sa_sc, sb_sc)
    else:
        @pl.when(n_off % 2 == 0)
        def _():
            overlap_tiles(sa_sc, sb_sc)

        @pl.when(n_off % 2 == 1)
        def _():
            produce(n_off, sb_sc)
            consume(n_off - 1, sa_sc)
            overlap_tiles(sb_sc, sa_sc)

    lam = _lambda_value(lam_ref, lam0)
    inv_l = 1.0 / l_sc[...]
    acc = acc_sc[...] * inv_l
    o = acc[:, :tq] - lam * acc[:, tq:]
    ms = jnp.mean(o * o, axis=0, keepdims=True)
    o = (o * lax.rsqrt(ms + EPS)).T
    o_ref[...] = (o * g_ref[...] * (1.0 - lam0)).astype(o_ref.dtype)


def _attn_prompt(q, kb, vt, lam_rows, subln_g, slopes, lam0, tq, tk):
    t = q.shape[0]
    assert t % tq == 0 and tq % tk == 0
    return pl.pallas_call(
        functools.partial(_attn_prompt_kernel, tq=tq, tk=tk, lam0=lam0),
        grid=(A_HEADS, t // tq),
        in_specs=[pl.BlockSpec(memory_space=pltpu.SMEM),
                  pl.BlockSpec((8, LANES), lambda h, i: (0, 0)),
                  pl.BlockSpec((1, A_VDIM), lambda h, i: (0, 0)),
                  pl.BlockSpec((tq, A_VDIM), lambda h, i: (i, h)),
                  pl.BlockSpec((t, A_VDIM), lambda h, i: (0, h)),
                  pl.BlockSpec((t // tk, A_VDIM, tk), lambda h, i: (0, h, 0))],
        out_specs=pl.BlockSpec((tq, A_VDIM), lambda h, i: (i, h)),
        out_shape=jax.ShapeDtypeStruct((t, A_HEADS * A_VDIM), BF16),
        scratch_shapes=[pltpu.VMEM((2 * tq, 2 * LANES), BF16),
                        pltpu.VMEM((tk, LANES), BF16),
                        pltpu.VMEM((tk, tk), F32),
                        pltpu.VMEM((tk, 2 * tq), F32),
                        pltpu.VMEM((tk, 2 * tq), F32),
                        pltpu.VMEM((1, 2 * tq), F32),
                        pltpu.VMEM((1, 2 * tq), F32),
                        pltpu.VMEM((A_VDIM, 2 * tq), F32)],
        compiler_params=_cparams(2),
        name="attn_prompt",
    )(slopes, lam_rows, subln_g, q, kb, vt)


SAMPLE_KEY_TILE = 1024


def _attn_sample_kernel(slopes_ref, lam_ref, g_ref, q_ref, kn_ref, vn_ref, kc_ref, vc_ref,
                        o_ref, m_sc, l_sc, acc_sc, *, past, lam0):
    t = pl.program_id(1)
    nt = pl.num_programs(1)
    tq = q_ref.shape[0]
    tk = kc_ref.shape[0] // A_HEADS

    @pl.when(t == 0)
    def _():
        m_sc[...] = jnp.full(m_sc.shape, NEG, F32)
        l_sc[...] = jnp.zeros(l_sc.shape, F32)
        acc_sc[...] = jnp.zeros(acc_sc.shape, F32)

    def update(h, s, v):
        m_prev = m_sc[h]
        m_new = jnp.maximum(m_prev, jnp.max(s, axis=-1, keepdims=True))
        alpha = jnp.exp2(m_prev - m_new)
        p = jnp.exp2(s - m_new)
        l_sc[h] = alpha * l_sc[h] + jnp.sum(p, axis=-1, keepdims=True)
        acc_sc[h] = alpha * acc_sc[h] + jnp.dot(p.astype(BF16), v, preferred_element_type=F32)
        m_sc[h] = m_new

    kpos = t * tk + lax.broadcasted_iota(jnp.int32, (1, tk), 1)
    rel = (kpos - past).astype(F32)
    qqs = []
    for h in range(A_HEADS):
        qq = _stack_maps(q_ref[:, h * A_VDIM:(h + 1) * A_VDIM])
        qqs.append(qq)
        head_rows = pl.ds(h, tk, stride=A_HEADS)
        s = lax.dot_general(qq, kc_ref[head_rows, :].astype(BF16), NT_DIMS,
                            preferred_element_type=F32)
        update(h, s + slopes_ref[h] * rel, vc_ref[head_rows, :].astype(BF16))

    @pl.when(t == nt - 1)
    def _():
        lam = _lambda_value(lam_ref, lam0)
        r = lax.broadcasted_iota(jnp.int32, (2 * tq, tq), 0)
        c = lax.broadcasted_iota(jnp.int32, (2 * tq, tq), 1)
        r = jnp.where(r >= tq, r - tq, r)
        own = jnp.minimum(c, 2 * r - c).astype(F32)
        for h in range(A_HEADS):
            cols = slice(h * A_VDIM, (h + 1) * A_VDIM)
            s = lax.dot_general(qqs[h], kn_ref[:, cols], NT_DIMS, preferred_element_type=F32)
            update(h, s + slopes_ref[h] * own, vn_ref[:, cols])
            acc = acc_sc[h] * (1.0 / l_sc[h])
            o = acc[:tq] - lam * acc[tq:]
            ms = jnp.mean(o * o, axis=-1, keepdims=True)
            o_ref[:, cols] = (o * lax.rsqrt(ms + EPS) * g_ref[...]
                              * (1.0 - lam0)).astype(o_ref.dtype)


def _attn_sample(q, kb, vb, cache_k, cache_v, layer, lam_rows, subln_g, slopes, lam0, tq):
    nb, past = cache_k.shape[1], cache_k.shape[2] // A_HEADS
    tk = min(SAMPLE_KEY_TILE, past)
    assert past % tk == 0
    aw = A_HEADS * A_VDIM
    rows = pl.BlockSpec((tq, aw), lambda b, t: (b, 0))
    cache = pl.BlockSpec((None, None, tk * A_HEADS, A_VDIM), lambda b, t: (layer, b, t, 0))
    return pl.pallas_call(
        functools.partial(_attn_sample_kernel, past=past, lam0=lam0),
        grid=(nb, past // tk),
        in_specs=[pl.BlockSpec(memory_space=pltpu.SMEM),
                  pl.BlockSpec((8, LANES), lambda b, t: (0, 0)),
                  pl.BlockSpec((1, A_VDIM), lambda b, t: (0, 0)),
                  rows, rows, rows, cache, cache],
        out_specs=rows,
        out_shape=jax.ShapeDtypeStruct((nb * tq, aw), BF16),
        scratch_shapes=[pltpu.VMEM((A_HEADS, 2 * tq, 1), F32),
                        pltpu.VMEM((A_HEADS, 2 * tq, 1), F32),
                        pltpu.VMEM((A_HEADS, 2 * tq, A_VDIM), F32)],
        compiler_params=_cparams(2),
        name="attn_sample",
    )(slopes, lam_rows, subln_g, q, kb, vb, cache_k, cache_v)


_HI = lax.Precision.HIGHEST
GDN_ROWS_PER_STEP = 4 * CHUNK


def _dot(a, b):
    return jnp.dot(a, b, preferred_element_type=F32)


def _split(a):
    hi = a.astype(BF16)
    return hi, (a - hi.astype(F32)).astype(BF16)


def _block_diag2(a, left):
    zero = jnp.zeros_like(a)
    return jnp.concatenate([jnp.where(left, a, zero), jnp.where(left, zero, a)], axis=0)


def _pair_prod(a_parts, b_parts, left):
    ah, bh = a_parts[0], b_parts[0]
    bdh = _block_diag2(bh, left)
    if len(a_parts) == 1 and len(b_parts) == 1:
        return _dot(ah, bdh)
    if len(a_parts) == 1:
        return _dot(jnp.concatenate([ah, ah], axis=1),
                    jnp.concatenate([bdh, _block_diag2(b_parts[1], left)], axis=0))
    r = _dot(jnp.concatenate([ah, a_parts[1]], axis=1), jnp.concatenate([bdh, bdh], axis=0))
    if len(b_parts) == 1:
        return r
    return r + _dot(ah, _block_diag2(b_parts[1], left))


def _inverse_masks(ri, cj):
    n = ri.shape[0]
    masks = [(ri // 2) == (cj // 2)]
    b = 2
    while b < n:
        same_big = ((ri // (2 * b)) == (cj // (2 * b))).astype(jnp.int32)
        same_small = ((ri // b) == (cj // b)).astype(jnp.int32)
        masks.append((same_big - same_small) > 0)
        b *= 2
    return masks


def _unit_lower_inverse(lows, masks, eye, left):
    xs = [eye - jnp.where(masks[0], low, 0.0) for low in lows]
    for mask in masks[1:]:
        xparts = [_split(x) for x in xs]
        ys = [_pair_prod((jnp.where(mask, low, 0.0).astype(BF16),), xp, left)
              for low, xp in zip(lows, xparts)]
        xs = [x - _pair_prod(xp, _split(y), left) for x, xp, y in zip(xs, xparts, ys)]
    return xs


def _gdn_kernel(x_ref, buf_ref, ba_ref, z_ref, s0_ref, cw_ref, par_ref, ng_ref,
                o_ref, s_out_ref, buf_out_ref, xp_sc, s_sc):
    c_idx = pl.program_id(1)
    nc = pl.num_programs(1)
    R = x_ref.shape[0]
    C = CHUNK
    n_sub = R // C
    nq = B_HEADS * B_DK
    cat = jnp.concatenate

    @pl.when(c_idx == 0)
    def _():
        xp_sc[HIST - (GDN_CONV - 1):HIST, :] = buf_ref[...]
        s_sc[...] = s0_ref[...].astype(F32)

    xp_sc[HIST:HIST + R, :] = x_ref[...]
    xp = xp_sc[...]
    conv = xp[HIST:] * cw_ref[GDN_CONV - 1:GDN_CONV, :]
    for j in range(GDN_CONV - 1):
        shifted = pltpu.roll(xp, GDN_CONV - 1 - j, axis=0)
        conv = conv + shifted[HIST:] * cw_ref[j:j + 1, :]
    conv = conv * jax.nn.sigmoid(conv)
    tail = xp_sc[R:R + HIST, :]
    xp_sc[0:HIST, :] = tail
    buf_out_ref[...] = tail[HIST - (GDN_CONV - 1):HIST, :]

    ba = ba_ref[...]
    lane = lax.broadcasted_iota(jnp.int32, ba.shape, 1)
    beta_all = jax.nn.sigmoid(ba)
    g_all = jnp.where((lane >= B_HEADS) & (lane < 2 * B_HEADS),
                      -jnp.exp(par_ref[0:1, :]) * jax.nn.softplus(ba + par_ref[1:2, :]), 0.0)
    rr = lax.broadcasted_iota(jnp.int32, (R, R), 0)
    cc = lax.broadcasted_iota(jnp.int32, (R, R), 1)
    same_chunk = ((rr // C) - (cc // C)) == 0
    tri = jnp.where(same_chunk, jnp.where(rr >= cc, 1.0, 0.0), 0.0).astype(F32)
    gcum_all = jnp.dot(tri, g_all, precision=_HI, preferred_element_type=F32)

    ri = lax.broadcasted_iota(jnp.int32, (C, 2 * C), 0)
    cl = lax.broadcasted_iota(jnp.int32, (C, 2 * C), 1)
    left = cl < C
    cj = jnp.where(left, cl, cl - C)
    incl = ri >= cj
    strict = ri > cj
    eye = jnp.where(ri == cj, 1.0, 0.0).astype(F32)
    inv_masks = _inverse_masks(ri, cj)
    zk = jnp.zeros((C, B_DK), F32)
    z2 = jnp.zeros((C, B_DV + B_DK), F32)
    zv = jnp.zeros((C, B_DV), F32)

    subs = range(n_sub)
    pairs = range(B_HEADS // 2)
    q, k, v, kb, beta, gc, eg, g_last, gcum_t = [], [], [], [], [], [], [], [], []
    for c in subs:
        rows = slice(c * C, (c + 1) * C)
        gsub = gcum_all[rows]
        gcum_t.append(cat([gsub, gsub], axis=0).T)
        qc, kc, vc, kbc, bc, gcc, egc = [], [], [], [], [], [], []
        for h in range(B_HEADS):
            qh = conv[rows, h * B_DK:(h + 1) * B_DK]
            kh = conv[rows, nq + h * B_DK:nq + (h + 1) * B_DK]
            qc.append(qh * lax.rsqrt(jnp.sum(qh * qh, axis=-1, keepdims=True) + EPS)
                      * (B_DK ** -0.5))
            kc.append(kh * lax.rsqrt(jnp.sum(kh * kh, axis=-1, keepdims=True) + EPS))
            vc.append(conv[rows, 2 * nq + h * B_DV:2 * nq + (h + 1) * B_DV])
            bc.append(beta_all[rows, h:h + 1])
            gcc.append(gsub[:, B_HEADS + h:B_HEADS + h + 1])
            kbc.append(kc[-1] * bc[-1])
            egc.append(jnp.exp(gcc[-1]))
        q.append(qc)
        k.append(kc)
        v.append(vc)
        kb.append(kbc)
        beta.append(bc)
        gc.append(gcc)
        eg.append(egc)
        g_last.append([g[C - 1:C, :] for g in gcc])

    lows, attns = [], []
    for c in subs:
        for p in pairs:
            a, b = 2 * p, 2 * p + 1
            g_col = jnp.where(left, gc[c][a], gc[c][b])
            g_row = jnp.where(left[0:1], gcum_t[c][B_HEADS + a:B_HEADS + a + 1, :],
                              gcum_t[c][B_HEADS + b:B_HEADS + b + 1, :])
            decay = jnp.where(incl, jnp.exp(jnp.where(incl, g_col - g_row, 0.0)), 0.0)
            lhs = cat([cat([kb[c][a], kb[c][b]], axis=1), cat([q[c][a], q[c][b]], axis=1)],
                      axis=0)
            rhs = cat([cat([k[c][a], zk], axis=1), cat([zk, k[c][b]], axis=1)], axis=0)
            m1 = lax.dot_general(lhs.astype(BF16), rhs.astype(BF16), NT_DIMS,
                                 preferred_element_type=F32)
            lows.append(jnp.where(strict, m1[:C] * decay, 0.0).astype(BF16).astype(F32))
            attns.append(m1[C:] * decay)

    tinvs = _unit_lower_inverse(lows, inv_masks, eye, left)

    sols = []
    for c in subs:
        for p in pairs:
            a, b = 2 * p, 2 * p + 1
            rhs_a = cat([v[c][a] * beta[c][a], kb[c][a] * eg[c][a]], axis=1)
            rhs_b = cat([v[c][b] * beta[c][b], kb[c][b] * eg[c][b]], axis=1)
            rhs_bd = cat([cat([rhs_a, z2], axis=1), cat([z2, rhs_b], axis=1)], axis=0)
            sols.append(_dot(tinvs[c * len(pairs) + p].astype(BF16), rhs_bd.astype(BF16)))

    s_cur = [s_sc[h] for h in range(B_HEADS)]
    for c in subs:
        rows = slice(c * C, (c + 1) * C)
        v_new, q_s = [], []
        for h in range(B_HEADS):
            sol = sols[c * len(pairs) + h // 2]
            base = (h % 2) * (B_DV + B_DK)
            u = sol[:, base:base + B_DV]
            w = sol[:, base + B_DV:base + B_DV + B_DK]
            m2 = _dot(cat([w, q[c][h] * eg[c][h]], axis=0).astype(BF16), s_cur[h].astype(BF16))
            v_new.append(u - m2[:C])
            q_s.append(m2[C:])
        o_pairs, upds = [], []
        for p in pairs:
            a, b = 2 * p, 2 * p + 1
            vn_bd = cat([cat([v_new[a], zv], axis=1), cat([zv, v_new[b]], axis=1)],
                        axis=0).astype(BF16)
            o_pairs.append(_dot(attns[c * len(pairs) + p].astype(BF16), vn_bd))
            kd_t = cat([k[c][a] * jnp.exp(g_last[c][a] - gc[c][a]),
                        k[c][b] * jnp.exp(g_last[c][b] - gc[c][b])], axis=0).T
            upds.append(_dot(kd_t.astype(BF16), vn_bd))
        for h in range(B_HEADS):
            lo = (h % 2) * B_DV
            s_cur[h] = s_cur[h] * jnp.exp(g_last[c][h]) + upds[h // 2][:, lo:lo + B_DV]
            o = q_s[h] + o_pairs[h // 2][:, lo:lo + B_DV]
            zh = z_ref[rows, h * B_DV:(h + 1) * B_DV]
            on = o * lax.rsqrt(jnp.mean(o * o, axis=-1, keepdims=True) + EPS) * ng_ref[...]
            o_ref[rows, h * B_DV:(h + 1) * B_DV] = (
                on * (zh * jax.nn.sigmoid(zh))).astype(o_ref.dtype)
    for h in range(B_HEADS):
        s_sc[h] = s_cur[h]

    @pl.when(c_idx == nc - 1)
    def _():
        s_out_ref[...] = s_sc[...].astype(s_out_ref.dtype)


def _gdn(qkv, ba, z, s0, buf, conv_w, par, norm_g, n_seq, rows_per_seq, rows_per_step):
    assert rows_per_seq % rows_per_step == 0 and rows_per_step % CHUNK == 0
    nc = rows_per_seq // rows_per_step
    nqkv = qkv.shape[1]
    row = lambda b, c: (b * nc + c, 0)
    return pl.pallas_call(
        _gdn_kernel,
        grid=(n_seq, nc),
        in_specs=[pl.BlockSpec((rows_per_step, nqkv), row),
                  pl.BlockSpec((None, GDN_CONV - 1, nqkv), lambda b, c: (b, 0, 0)),
                  pl.BlockSpec((rows_per_step, LANES), row),
                  pl.BlockSpec((rows_per_step, B_HEADS * B_DV), row),
                  pl.BlockSpec((None, B_HEADS, B_DK, B_DV), lambda b, c: (b, 0, 0, 0)),
                  pl.BlockSpec((GDN_CONV, nqkv), lambda b, c: (0, 0)),
                  pl.BlockSpec((8, LANES), lambda b, c: (0, 0)),
                  pl.BlockSpec((1, B_DV), lambda b, c: (0, 0))],
        out_specs=[pl.BlockSpec((rows_per_step, B_HEADS * B_DV), row),
                   pl.BlockSpec((None, B_HEADS, B_DK, B_DV), lambda b, c: (b, 0, 0, 0)),
                   pl.BlockSpec((None, GDN_CONV - 1, nqkv), lambda b, c: (b, 0, 0))],
        out_shape=[jax.ShapeDtypeStruct((n_seq * rows_per_seq, B_HEADS * B_DV), BF16),
                   jax.ShapeDtypeStruct((n_seq, B_HEADS, B_DK, B_DV), F32),
                   jax.ShapeDtypeStruct((n_seq, GDN_CONV - 1, nqkv), F32)],
        scratch_shapes=[pltpu.VMEM((HIST + rows_per_step, nqkv), F32),
                        pltpu.VMEM((B_HEADS, B_DK, B_DV), F32)],
        compiler_params=_cparams(2),
        name="gdn",
    )(qkv, buf, ba, z, s0, conv_w, par, norm_g)


def _merge_kernel(x_ref, oa_ref, ob_ref, ga_ref, gb_ref, wa_ref, wb_ref, wo_ref, h_ref):
    a = jnp.dot(oa_ref[...], wa_ref[...], preferred_element_type=F32)
    b = jnp.dot(ob_ref[...], wb_ref[...], preferred_element_type=F32)
    merged = jax.nn.sigmoid(ga_ref[...]) * a + jax.nn.sigmoid(gb_ref[...]) * b
    h_ref[...] = x_ref[...] + jnp.dot(merged.astype(BF16), wo_ref[...],
                                      preferred_element_type=F32)


def _merge(x, oa, ob, ga, gb, wa, wb, wo, tm):
    rows, d = x.shape
    rowspec = pl.BlockSpec((tm, d), lambda i: (i, 0))
    wspec = pl.BlockSpec((d, d), lambda i: (0, 0))
    return pl.pallas_call(
        _merge_kernel,
        grid=(rows // tm,),
        in_specs=[rowspec] * 5 + [wspec] * 3,
        out_specs=rowspec,
        out_shape=jax.ShapeDtypeStruct((rows, d), F32),
        compiler_params=_cparams(1),
        name="merge",
    )(x, oa, ob, ga, gb, wa, wb, wo)


FFN_ROW_TILE = 512


def _ffn_kernel(h_ref, g_ref, buf_g_ref, buf_v_ref, wg_ref, wv_ref, cwg_ref, cwv_ref,
                cbg_ref, cbv_ref, wd_ref, fg_ref, y_ref, nbuf_g_ref, nbuf_v_ref,
                hn_sc, ug_sc, uv_sc, acc_sc, *, tiles_per_seq, final):
    i = pl.program_id(0)
    j = pl.program_id(1)
    nj = pl.num_programs(1)
    tm = h_ref.shape[0]
    first = (i % tiles_per_seq) == 0
    nh = FFN_CONV - 1

    @pl.when(j == 0)
    def _():
        x = h_ref[...]
        ms = jnp.mean(x * x, axis=-1, keepdims=True)
        hn_sc[...] = (x * lax.rsqrt(ms + EPS) * g_ref[...]).astype(BF16)
        acc_sc[...] = jnp.zeros(acc_sc.shape, F32)

    def conv_half(u_sc, w_ref, cw_ref, cb_ref, buf_ref, nbuf_ref):
        @pl.when(first)
        def _():
            u_sc[j, HIST - nh:HIST, :] = buf_ref[...]
        u_sc[j, HIST:HIST + tm, :] = jnp.dot(hn_sc[...], w_ref[...], preferred_element_type=F32)
        out = u_sc[j, HIST:HIST + tm, :] * cw_ref[nh:nh + 1, :] + cb_ref[...]
        for t in range(nh):
            off = HIST - nh + t
            out = out + u_sc[j, off:off + tm, :] * cw_ref[t:t + 1, :]
        tail = u_sc[j, tm:tm + HIST, :]
        u_sc[j, 0:HIST, :] = tail
        nbuf_ref[j] = tail[HIST - nh:HIST, :]
        return out

    gate = conv_half(ug_sc, wg_ref, cwg_ref, cbg_ref, buf_g_ref, nbuf_g_ref)
    val = conv_half(uv_sc, wv_ref, cwv_ref, cbv_ref, buf_v_ref, nbuf_v_ref)
    act = (gate * jax.nn.sigmoid(gate) * val).astype(BF16)
    acc_sc[...] += jnp.dot(act, wd_ref[...], preferred_element_type=F32)

    @pl.when(j == nj - 1)
    def _():
        y = h_ref[...] + acc_sc[...]
        if final:
            ms = jnp.mean(y * y, axis=-1, keepdims=True)
            y = y * lax.rsqrt(ms + EPS) * fg_ref[...]
        y_ref[...] = y


def _ffn(h, g, buf, w_up, conv_w, conv_b, w_down, final_g, n_seq, rows_per_seq, tm, cw, final):
    rows, d = h.shape
    d_ff = w_down.shape[0]
    assert d_ff % cw == 0 and rows_per_seq % tm == 0
    nj = d_ff // cw
    tps = rows_per_seq // tm
    nh = FFN_CONV - 1
    gate_col = lambda i, j: (0, j)
    val_col = lambda i, j: (0, nj + j)
    seq_gate = lambda i, j: (i // tps, 0, j)
    seq_val = lambda i, j: (i // tps, 0, nj + j)
    y, nbg, nbv = pl.pallas_call(
        functools.partial(_ffn_kernel, tiles_per_seq=tps, final=final),
        grid=(rows // tm, nj),
        in_specs=[pl.BlockSpec((tm, d), lambda i, j: (i, 0)),
                  pl.BlockSpec((1, d), lambda i, j: (0, 0)),
                  pl.BlockSpec((None, nh, cw), seq_gate),
                  pl.BlockSpec((None, nh, cw), seq_val),
                  pl.BlockSpec((d, cw), gate_col),
                  pl.BlockSpec((d, cw), val_col),
                  pl.BlockSpec((FFN_CONV, cw), gate_col),
                  pl.BlockSpec((FFN_CONV, cw), val_col),
                  pl.BlockSpec((1, cw), gate_col),
                  pl.BlockSpec((1, cw), val_col),
                  pl.BlockSpec((cw, d), lambda i, j: (j, 0)),
                  pl.BlockSpec((1, d), lambda i, j: (0, 0))],
        out_specs=[pl.BlockSpec((tm, d), lambda i, j: (i, 0)),
                   pl.BlockSpec((None, nj, nh, cw), lambda i, j: (i // tps, 0, 0, 0)),
                   pl.BlockSpec((None, nj, nh, cw), lambda i, j: (i // tps, 0, 0, 0))],
        out_shape=[jax.ShapeDtypeStruct((rows, d), F32),
                   jax.ShapeDtypeStruct((n_seq, nj, nh, cw), F32),
                   jax.ShapeDtypeStruct((n_seq, nj, nh, cw), F32)],
        scratch_shapes=[pltpu.VMEM((tm, d), BF16),
                        pltpu.VMEM((nj, HIST + tm, cw), F32),
                        pltpu.VMEM((nj, HIST + tm, cw), F32),
                        pltpu.VMEM((tm, d), F32)],
        compiler_params=_cparams(2),
        name="ffn",
    )(h, g, buf, buf, w_up, w_up, conv_w, conv_w, conv_b, conv_b, w_down, final_g)
    unchunk = lambda a: a.transpose(0, 2, 1, 3).reshape(n_seq, nh, d_ff)
    return y, jnp.concatenate([unchunk(nbg), unchunk(nbv)], axis=-1)


def _prep_layer_weights(l, P):
    w_in = P["w_in"][l]
    d = w_in.shape[0]
    aw = A_HEADS * A_VDIM
    nqkv = B_HEADS * (2 * B_DK + B_DV)
    c_qkvb = 3 * aw
    c_beta = c_qkvb + nqkv
    c_z = c_beta + 2 * B_HEADS
    bw = B_HEADS * B_DV
    w_q = w_in[:, :aw] * (A_HEAD_DIM ** -0.5 * LOG2E)
    w_a = jnp.concatenate([w_q, w_in[:, aw:3 * aw]], axis=1).astype(BF16)
    w_b = jnp.concatenate([w_in[:, c_qkvb:c_beta + 2 * B_HEADS],
                           jnp.zeros((d, LANES - 2 * B_HEADS), F32)], axis=1).astype(BF16)
    w_c = w_in[:, c_z:c_z + bw + 2 * d].astype(BF16)
    lam_rows = jnp.zeros((8, LANES), F32)
    for r, name in enumerate(("lambda_q1", "lambda_k1", "lambda_q2", "lambda_k2")):
        lam_rows = lam_rows.at[r, :A_HEAD_DIM].set(P[name][l])
    par = jnp.zeros((8, LANES), F32)
    par = par.at[0, B_HEADS:2 * B_HEADS].set(P["gdn_a_log"][l])
    par = par.at[1, B_HEADS:2 * B_HEADS].set(P["gdn_dt_bias"][l])
    return dict(
        w_a=w_a, w_b=w_b, w_c=w_c, lam_rows=lam_rows, par=par,
        norm_mix_g=P["norm_mix_g"][l][None, :],
        subln_g=P["subln_g"][l][None, :],
        gdn_conv_w=P["gdn_conv_w"][l],
        gdn_norm_g=P["gdn_norm_g"][l][None, :],
        w_proj_a=P["w_proj_a"][l].astype(BF16),
        w_proj_b=P["w_proj_b"][l].astype(BF16),
        w_out=P["w_out"][l].astype(BF16),
        norm_ffn_g=P["norm_ffn_g"][l][None, :],
        w_up=P["w_up"][l].astype(BF16),
        ffn_conv_w=P["ffn_conv_w"][l],
        ffn_conv_b=P["ffn_conv_b"][l][None, :],
        w_down=P["w_down"][l].astype(BF16),
        final_g=P["final_norm_g"][None, :],
    )


def _layer(x, l, depth, kv_prev, W, n_seq, rows_per_seq, cache_k, cache_v, s0, gbuf, fbuf,
           final, tm, tq, tk):
    rows, d = x.shape
    tm = min(tm, rows)
    aw = A_HEADS * A_VDIM
    nqkv = B_HEADS * (2 * B_DK + B_DV)
    lam0 = _lambda_init(l)
    slopes = (2.0 ** (-8.0 * jnp.arange(1, A_HEADS + 1, dtype=F32) / A_HEADS)) * LOG2E
    prompt = cache_k is None

    v_kinds = ("h", "t") if prompt else ("h", BF16)
    q, k, kb, v, vx = _norm_proj(
        x, W["norm_mix_g"], W["w_a"],
        ((0, aw, (BF16,)), (aw, aw, ("h", BF16)), (2 * aw, aw, v_kinds)), tm if not prompt else tk,
        layer=l, depth=depth, carried=kv_prev)
    qkvb, ba = _norm_proj(x, W["norm_mix_g"], W["w_b"],
                          ((0, nqkv, (F32,)), (nqkv, LANES, (F32,))), tm)
    z, ga, gb = _norm_proj(x, W["norm_mix_g"], W["w_c"],
                           ((0, aw, (F32,)), (aw, d, (F32,)), (aw + d, d, (F32,))), tm)

    if prompt:
        o_a = _attn_prompt(q, kb, vx, W["lam_rows"], W["subln_g"], slopes, lam0, tq, tk)
    else:
        o_a = _attn_sample(q, kb, vx, cache_k, cache_v, l, W["lam_rows"], W["subln_g"],
                           slopes, lam0, rows_per_seq)

    o_b, s_new, gbuf_new = _gdn(qkvb, ba, z, s0, gbuf, W["gdn_conv_w"], W["par"],
                                W["gdn_norm_g"], n_seq, rows_per_seq,
                                min(GDN_ROWS_PER_STEP, rows_per_seq))

    h = _merge(x, o_a, o_b, ga, gb, W["w_proj_a"], W["w_proj_b"], W["w_out"], tm)
    d_ff = W["w_down"].shape[0]
    y, fbuf_new = _ffn(h, W["norm_ffn_g"], fbuf, W["w_up"], W["ffn_conv_w"], W["ffn_conv_b"],
                       W["w_down"], W["final_g"], n_seq, rows_per_seq,
                       min(FFN_ROW_TILE, rows_per_seq), d_ff // 2, final)
    return y, k, v, s_new, gbuf_new, fbuf_new


def kernel(x_prompt, x_sample, cache_k, cache_v, state_gdn, state_gdn_conv, state_ffn_conv, norm_mix_g, w_in, lambda_q1, lambda_k1, lambda_q2, lambda_k2, subln_g, gdn_conv_w, gdn_a_log, gdn_dt_bias, gdn_norm_g, w_proj_a, w_proj_b, w_out, norm_ffn_g, w_up, ffn_conv_w, ffn_conv_b, w_down, final_norm_g):
    P = dict(norm_mix_g=norm_mix_g, w_in=w_in, lambda_q1=lambda_q1, lambda_k1=lambda_k1,
             lambda_q2=lambda_q2, lambda_k2=lambda_k2, subln_g=subln_g, gdn_conv_w=gdn_conv_w,
             gdn_a_log=gdn_a_log, gdn_dt_bias=gdn_dt_bias, gdn_norm_g=gdn_norm_g,
             w_proj_a=w_proj_a, w_proj_b=w_proj_b, w_out=w_out, norm_ffn_g=norm_ffn_g,
             w_up=w_up, ffn_conv_w=ffn_conv_w, ffn_conv_b=ffn_conv_b, w_down=w_down,
             final_norm_g=final_norm_g)
    depth = w_in.shape[0]
    bp, tp, d = x_prompt.shape
    bs, ts, _ = x_sample.shape
    past = cache_k.shape[2]
    assert ts == CHUNK and past % CHUNK == 0 and tp % CHUNK == 0
    nqkv = state_gdn_conv.shape[-1]
    d_ff2 = state_ffn_conv.shape[-1]
    weights = [_prep_layer_weights(l, P) for l in range(depth)]
    ck = cache_k.reshape(depth, bs, past * A_HEADS, A_VDIM)
    cv = cache_v.reshape(depth, bs, past * A_HEADS, A_VDIM)

    tk = min(ATTN_KEY_TILE, tp)
    tq = min(ATTN_QUERY_BLOCK, tp)
    tm = 256

    def run(x3, nseq, rps, sample):
        assert sample or nseq == 1
        x = x3.reshape(nseq * rps, d)
        kv, ss, gcs, fcs = (), [], [], []
        for l in range(depth):
            if sample:
                s0, gbuf, fbuf = state_gdn[l], state_gdn_conv[l], state_ffn_conv[l]
                c_k, c_v = ck, cv
            else:
                s0 = jnp.zeros((nseq, B_HEADS, B_DK, B_DV), F32)
                gbuf = jnp.zeros((nseq, GDN_CONV - 1, nqkv), F32)
                fbuf = jnp.zeros((nseq, FFN_CONV - 1, d_ff2), F32)
                c_k = c_v = None
            x, k_all, v_all, s_new, g_new, f_new = _layer(
                x, l, depth, kv, weights[l], nseq, rps, c_k, c_v, s0, gbuf, fbuf,
                l == depth - 1, tm, tq, tk)
            kv = (k_all, v_all)
            ss.append(s_new)
            gcs.append(g_new)
            fcs.append(f_new)
        kv_shape = (depth, nseq, rps, A_HEADS, A_VDIM)
        return (x.reshape(nseq, rps, d), kv[0].reshape(kv_shape), kv[1].reshape(kv_shape),
                jnp.stack(ss), jnp.stack(gcs), jnp.stack(fcs))

    y_p, k_p, v_p, s_p, gc_p, fc_p = run(x_prompt, bp, tp, False)
    y_s, k_s, v_s, s_s, gc_s, fc_s = run(x_sample, bs, ts, True)
    return (y_p, y_s, k_p, v_p, s_p, gc_p, fc_p, k_s, v_s, s_s, gc_s, fc_s)
```

```python
import functools
import math

import jax
import jax.numpy as jnp
from jax import lax
from jax.experimental import pallas as pl
from jax.experimental.pallas import tpu as pltpu

F32 = jnp.float32
BF16 = jnp.bfloat16

EPS = 1e-6
NEG = -1e30
LOG2E = 1.4426950408889634

CHUNK = 64
A_HEADS = 8
A_HEAD_DIM = 64
A_VDIM = 128
B_HEADS = 8
B_DK = 128
B_DV = 128
GDN_CONV = 4
FFN_CONV = 3
LANES = 128
HIST = 8

VMEM_LIMIT = 56 * 1024 * 1024


def _cparams(n_axes):
    return pltpu.CompilerParams(
        dimension_semantics=("arbitrary",) * n_axes,
        vmem_limit_bytes=VMEM_LIMIT)


def _lambda_init(layer):
    return 0.8 - 0.6 * math.exp(-0.3 * layer)


def _norm_proj_kernel(x_ref, g_ref, w_ref, *refs, segs, n_carried):
    out_refs = refs[n_carried:]
    x = x_ref[...]
    ms = jnp.mean(x * x, axis=-1, keepdims=True)
    xn = (x * lax.rsqrt(ms + EPS) * g_ref[...]).astype(BF16)
    k = 0
    for start, width, kinds in segs:
        w = w_ref[:, start:start + width]
        r = None
        for kind in kinds:
            o_ref = out_refs[k]
            k += 1
            if r is None:
                r = jnp.dot(xn, w, preferred_element_type=F32)
            if kind == "t":
                o_ref[...] = r.T.astype(o_ref.dtype)
            elif kind == "h":
                for hd in range(width // LANES):
                    o_ref[:, hd, :] = r[:, hd * LANES:(hd + 1) * LANES]
            else:
                o_ref[...] = r.astype(o_ref.dtype)


def _norm_proj(x, g, w, segs, tm, layer=0, depth=1, carried=()):
    rows, d = x.shape
    assert rows % tm == 0
    out_shape, out_specs, ksegs, h_outs = [], [], [], []
    for start, width, kinds in segs:
        kk = []
        for kind in kinds:
            if kind == "t":
                out_shape.append(jax.ShapeDtypeStruct((rows // tm, width, tm), BF16))
                out_specs.append(pl.BlockSpec((None, width, tm), lambda i: (i, 0, 0)))
                kk.append("t")
            elif kind == "h":
                h_outs.append(len(out_shape))
                out_shape.append(jax.ShapeDtypeStruct(
                    (depth, rows, width // LANES, LANES), F32))
                out_specs.append(pl.BlockSpec((None, tm, width // LANES, LANES),
                                              lambda i: (layer, i, 0, 0)))
                kk.append("h")
            else:
                out_shape.append(jax.ShapeDtypeStruct((rows, width), kind))
                out_specs.append(pl.BlockSpec((tm, width), lambda i: (i, 0)))
                kk.append("n")
        ksegs.append((start, width, tuple(kk)))
    assert len(carried) in (0, len(h_outs))
    return pl.pallas_call(
        functools.partial(_norm_proj_kernel, segs=tuple(ksegs), n_carried=len(carried)),
        grid=(rows // tm,),
        in_specs=[pl.BlockSpec((tm, d), lambda i: (i, 0)),
                  pl.BlockSpec((1, d), lambda i: (0, 0)),
                  pl.BlockSpec(w.shape, lambda i: (0, 0))]
                 + [pl.BlockSpec(memory_space=pl.ANY)] * len(carried),
        out_specs=out_specs,
        out_shape=out_shape,
        input_output_aliases={3 + n: h_outs[n] for n in range(len(carried))},
        compiler_params=_cparams(1),
        name="norm_proj",
    )(x, g, w, *carried)


def _lambda_value(lam_ref, lam0):
    a = jnp.sum(lam_ref[0:1, :] * lam_ref[1:2, :], axis=-1, keepdims=True)
    b = jnp.sum(lam_ref[2:3, :] * lam_ref[3:4, :], axis=-1, keepdims=True)
    return jnp.exp(a) - jnp.exp(b) + lam0


def _stack_maps(q):
    lane = lax.broadcasted_iota(jnp.int32, q.shape, 1)
    zero = jnp.zeros_like(q)
    return jnp.concatenate([jnp.where(lane < A_HEAD_DIM, q, zero),
                            jnp.where(lane >= A_HEAD_DIM, q, zero)], axis=0)


NT_DIMS = (((1,), (1,)), ((), ()))
ATTN_KEY_TILE = 512
ATTN_QUERY_BLOCK = 1024


def _attn_prompt_kernel(slopes_ref, lam_ref, g_ref, q_ref, k_ref, vt_ref, o_ref,
                        qq_sc, kaug_sc, dfix_sc, sa_sc, sb_sc, m_sc, l_sc, acc_sc,
                        *, tq, tk, lam0):
    h = pl.program_id(0)
    i = pl.program_id(1)
    slope = slopes_ref[h]
    q0 = i * tq
    nt = NT_DIMS
    w = tk
    nd = tq // tk
    n_off = i * nd
    n_strips = 2 * tq // w

    qq_sc[:, :A_VDIM] = _stack_maps(q_ref[...])
    lane_q = lax.broadcasted_iota(jnp.int32, (2 * tq, LANES), 1)
    qq_sc[:, A_VDIM:] = jnp.where(lane_q < 3, 1.0, 0.0).astype(BF16)

    @pl.when(i == 0)
    def _():
        b = slope * lax.broadcasted_iota(jnp.int32, (tk, LANES), 0).astype(F32)
        b1 = b.astype(BF16).astype(F32)
        b2 = (b - b1).astype(BF16).astype(F32)
        b3 = b - b1 - b2
        lane_k = lax.broadcasted_iota(jnp.int32, (tk, LANES), 1)
        kaug_sc[...] = jnp.where(lane_k == 0, b1, jnp.where(
            lane_k == 1, b2, jnp.where(lane_k == 2, b3, 0.0))).astype(BF16)
        c = lax.broadcasted_iota(jnp.int32, (tk, w), 0)
        r = lax.broadcasted_iota(jnp.int32, (tk, w), 1)
        fix = slope * jnp.minimum(2 * (r - c), 0).astype(F32)
        dfix_sc[...] = jnp.where((c // CHUNK) <= (r // CHUNK), fix, NEG)

    m_sc[...] = jnp.full(m_sc.shape, NEG, F32)
    l_sc[...] = jnp.zeros(l_sc.shape, F32)
    acc_sc[...] = jnp.zeros(acc_sc.shape, F32)

    groups = 8

    def key_reduce(op, x):
        part = op(x.reshape(groups, tk // groups, x.shape[-1]), axis=0)
        return op(part, axis=0, keepdims=True)

    def softmax_update(st, s_ref, kappa, vtt):
        sl = slice(st * w, (st + 1) * w)
        m_prev = m_sc[:, sl]
        m_new = jnp.maximum(m_prev, key_reduce(jnp.max, s_ref[:, sl]) + kappa)
        alpha = jnp.exp2(m_prev - m_new)
        p = jnp.exp2(s_ref[:, sl] - (m_new - kappa))
        pv = jnp.dot(jnp.concatenate([vtt, jnp.ones((16, tk), BF16)], axis=0), p.astype(BF16),
                     preferred_element_type=F32)
        l_sc[:, sl] = alpha * l_sc[:, sl] + pv[A_VDIM:A_VDIM + 1]
        acc_sc[:, sl] = alpha * acc_sc[:, sl] + pv[:A_VDIM]
        m_sc[:, sl] = m_new

    def produce(j, s_ref):
        k0 = pl.multiple_of(j * tk, tk)
        lhs = jnp.concatenate([k_ref[pl.ds(k0, tk), :], kaug_sc[...]], axis=1)
        s_ref[...] = lax.dot_general(lhs, qq_sc[...], nt, preferred_element_type=F32)

    def consume(j, s_ref):
        kappa = slope * (j * tk - q0).astype(F32)
        vtj = vt_ref[j]
        for st in range(n_strips):
            softmax_update(st, s_ref, kappa, vtj)

    def consume_overlap(d, s_ref):
        kappa = slope * (d * tk)
        vtd = vt_ref[n_off + d]
        for st in range(n_strips):
            q_lo = (st * w) % tq
            if q_lo < d * tk:
                continue
            if q_lo == d * tk:
                sl = slice(st * w, (st + 1) * w)
                s_ref[:, sl] = s_ref[:, sl] + dfix_sc[...]
            softmax_update(st, s_ref, kappa, vtd)

    def overlap_tiles(first_ref, second_ref):
        for d in range(nd):
            cur, nxt = (first_ref, second_ref) if d % 2 == 0 else (second_ref, first_ref)
            if d + 1 < nd:
                produce(n_off + d + 1, nxt)
            consume_overlap(d, cur)

    produce(0, sa_sc)

    def two_tiles(j):
        produce(j + 1, sb_sc)
        consume(j, sa_sc)
        produce(j + 2, sa_sc)
        consume(j + 1, sb_sc)

    def four_tiles(t, carry):
        two_tiles(4 * t)
        two_tiles(4 * t + 2)
        return carry

    lax.fori_loop(0, n_off // 4, four_tiles, 0)

    @pl.when(n_off % 4 >= 2)
    def _():
        two_tiles((n_off // 4) * 4)

    if nd % 2 == 0:
        overlap_tiles(sa_sc, sb_sc)
    else:
        @pl.when(n_off % 2 == 0)
        def _():
            overlap_tiles(sa_sc, sb_sc)

        @pl.when(n_off % 2 == 1)
        def _():
            produce(n_off, sb_sc)
            consume(n_off - 1, sa_sc)
            overlap_tiles(sb_sc, sa_sc)

    lam = _lambda_value(lam_ref, lam0)
    inv_l = 1.0 / l_sc[...]
    acc = acc_sc[...] * inv_l
    o = acc[:, :tq] - lam * acc[:, tq:]
    ms = jnp.mean(o * o, axis=0, keepdims=True)
    o = (o * lax.rsqrt(ms + EPS)).T
    o_ref[...] = (o * g_ref[...] * (1.0 - lam0)).astype(o_ref.dtype)


def _attn_prompt(q, kb, vt, lam_rows, subln_g, slopes, lam0, tq, tk):
    t = q.shape[0]
    assert t % tq == 0 and tq % tk == 0
    return pl.pallas_call(
        functools.partial(_attn_prompt_kernel, tq=tq, tk=tk, lam0=lam0),
        grid=(A_HEADS, t // tq),
        in_specs=[pl.BlockSpec(memory_space=pltpu.SMEM),
                  pl.BlockSpec((8, LANES), lambda h, i: (0, 0)),
                  pl.BlockSpec((1, A_VDIM), lambda h, i: (0, 0)),
                  pl.BlockSpec((tq, A_VDIM), lambda h, i: (i, h)),
                  pl.BlockSpec((t, A_VDIM), lambda h, i: (0, h)),
                  pl.BlockSpec((t // tk, A_VDIM, tk), lambda h, i: (0, h, 0))],
        out_specs=pl.BlockSpec((tq, A_VDIM), lambda h, i: (i, h)),
        out_shape=jax.ShapeDtypeStruct((t, A_HEADS * A_VDIM), BF16),
        scratch_shapes=[pltpu.VMEM((2 * tq, 2 * LANES), BF16),
                        pltpu.VMEM((tk, LANES), BF16),
                        pltpu.VMEM((tk, tk), F32),
                        pltpu.VMEM((tk, 2 * tq), F32),
                        pltpu.VMEM((tk, 2 * tq), F32),
                        pltpu.VMEM((1, 2 * tq), F32),
                        pltpu.VMEM((1, 2 * tq), F32),
                        pltpu.VMEM((A_VDIM, 2 * tq), F32)],
        compiler_params=_cparams(2),
        name="attn_prompt",
    )(slopes, lam_rows, subln_g, q, kb, vt)


SAMPLE_KEY_TILE = 1024


def _attn_sample_kernel(slopes_ref, lam_ref, g_ref, q_ref, kn_ref, vn_ref, kc_ref, vc_ref,
                        o_ref, m_sc, l_sc, acc_sc, *, past, lam0):
    t = pl.program_id(1)
    nt = pl.num_programs(1)
    tq = q_ref.shape[0]
    tk = kc_ref.shape[0] // A_HEADS

    @pl.when(t == 0)
    def _():
        m_sc[...] = jnp.full(m_sc.shape, NEG, F32)
        l_sc[...] = jnp.zeros(l_sc.shape, F32)
        acc_sc[...] = jnp.zeros(acc_sc.shape, F32)

    def update(h, s, v):
        m_prev = m_sc[h]
        m_new = jnp.maximum(m_prev, jnp.max(s, axis=-1, keepdims=True))
        alpha = jnp.exp2(m_prev - m_new)
        p = jnp.exp2(s - m_new)
        l_sc[h] = alpha * l_sc[h] + jnp.sum(p, axis=-1, keepdims=True)
        acc_sc[h] = alpha * acc_sc[h] + jnp.dot(p.astype(BF16), v, preferred_element_type=F32)
        m_sc[h] = m_new

    kpos = t * tk + lax.broadcasted_iota(jnp.int32, (1, tk), 1)
    rel = (kpos - past).astype(F32)
    qqs = []
    for h in range(A_HEADS):
        qq = _stack_maps(q_ref[:, h * A_VDIM:(h + 1) * A_VDIM])
        qqs.append(qq)
        head_rows = pl.ds(h, tk, stride=A_HEADS)
        s = lax.dot_general(qq, kc_ref[head_rows, :].astype(BF16), NT_DIMS,
                            preferred_element_type=F32)
        update(h, s + slopes_ref[h] * rel, vc_ref[head_rows, :].astype(BF16))

    @pl.when(t == nt - 1)
    def _():
        lam = _lambda_value(lam_ref, lam0)
        r = lax.broadcasted_iota(jnp.int32, (2 * tq, tq), 0)
        c = lax.broadcasted_iota(jnp.int32, (2 * tq, tq), 1)
        r = jnp.where(r >= tq, r - tq, r)
        own = jnp.minimum(c, 2 * r - c).astype(F32)
        for h in range(A_HEADS):
            cols = slice(h * A_VDIM, (h + 1) * A_VDIM)
            s = lax.dot_general(qqs[h], kn_ref[:, cols], NT_DIMS, preferred_element_type=F32)
            update(h, s + slopes_ref[h] * own, vn_ref[:, cols])
            acc = acc_sc[h] * (1.0 / l_sc[h])
            o = acc[:tq] - lam * acc[tq:]
            ms = jnp.mean(o * o, axis=-1, keepdims=True)
            o_ref[:, cols] = (o * lax.rsqrt(ms + EPS) * g_ref[...]
                              * (1.0 - lam0)).astype(o_ref.dtype)


def _attn_sample(q, kb, vb, cache_k, cache_v, layer, lam_rows, subln_g, slopes, lam0, tq):
    nb, past = cache_k.shape[1], cache_k.shape[2] // A_HEADS
    tk = min(SAMPLE_KEY_TILE, past)
    assert past % tk == 0
    aw = A_HEADS * A_VDIM
    rows = pl.BlockSpec((tq, aw), lambda b, t: (b, 0))
    cache = pl.BlockSpec((None, None, tk * A_HEADS, A_VDIM), lambda b, t: (layer, b, t, 0))
    return pl.pallas_call(
        functools.partial(_attn_sample_kernel, past=past, lam0=lam0),
        grid=(nb, past // tk),
        in_specs=[pl.BlockSpec(memory_space=pltpu.SMEM),
                  pl.BlockSpec((8, LANES), lambda b, t: (0, 0)),
                  pl.BlockSpec((1, A_VDIM), lambda b, t: (0, 0)),
                  rows, rows, rows, cache, cache],
        out_specs=rows,
        out_shape=jax.ShapeDtypeStruct((nb * tq, aw), BF16),
        scratch_shapes=[pltpu.VMEM((A_HEADS, 2 * tq, 1), F32),
                        pltpu.VMEM((A_HEADS, 2 * tq, 1), F32),
                        pltpu.VMEM((A_HEADS, 2 * tq, A_VDIM), F32)],
        compiler_params=_cparams(2),
        name="attn_sample",
    )(slopes, lam_rows, subln_g, q, kb, vb, cache_k, cache_v)


_HI = lax.Precision.HIGHEST
GDN_ROWS_PER_STEP = 4 * CHUNK


def _dot(a, b):
    return jnp.dot(a, b, preferred_element_type=F32)


def _split(a):
    hi = a.astype(BF16)
    return hi, (a - hi.astype(F32)).astype(BF16)


def _block_diag2(a, left):
    zero = jnp.zeros_like(a)
    return jnp.concatenate([jnp.where(left, a, zero), jnp.where(left, zero, a)], axis=0)


def _pair_prod(a_parts, b_parts, left):
    ah, bh = a_parts[0], b_parts[0]
    bdh = _block_diag2(bh, left)
    if len(a_parts) == 1 and len(b_parts) == 1:
        return _dot(ah, bdh)
    if len(a_parts) == 1:
        return _dot(jnp.concatenate([ah, ah], axis=1),
                    jnp.concatenate([bdh, _block_diag2(b_parts[1], left)], axis=0))
    r = _dot(jnp.concatenate([ah, a_parts[1]], axis=1), jnp.concatenate([bdh, bdh], axis=0))
    if len(b_parts) == 1:
        return r
    return r + _dot(ah, _block_diag2(b_parts[1], left))


def _inverse_masks(ri, cj):
    n = ri.shape[0]
    masks = [(ri // 2) == (cj // 2)]
    b = 2
    while b < n:
        same_big = ((ri // (2 * b)) == (cj // (2 * b))).astype(jnp.int32)
        same_small = ((ri // b) == (cj // b)).astype(jnp.int32)
        masks.append((same_big - same_small) > 0)
        b *= 2
    return masks


def _unit_lower_inverse(lows, masks, eye, left):
    xs = [eye - jnp.where(masks[0], low, 0.0) for low in lows]
    for mask in masks[1:]:
        xparts = [_split(x) for x in xs]
        ys = [_pair_prod((jnp.where(mask, low, 0.0).astype(BF16),), xp, left)
              for low, xp in zip(lows, xparts)]
        xs = [x - _pair_prod(xp, _split(y), left) for x, xp, y in zip(xs, xparts, ys)]
    return xs


def _gdn_kernel(x_ref, buf_ref, ba_ref, z_ref, s0_ref, cw_ref, par_ref, ng_ref,
                o_ref, s_out_ref, buf_out_ref, xp_sc, s_sc):
    c_idx = pl.program_id(1)
    nc = pl.num_programs(1)
    R = x_ref.shape[0]
    C = CHUNK
    n_sub = R // C
    nq = B_HEADS * B_DK
    cat = jnp.concatenate

    @pl.when(c_idx == 0)
    def _():
        xp_sc[HIST - (GDN_CONV - 1):HIST, :] = buf_ref[...]
        s_sc[...] = s0_ref[...].astype(F32)

    xp_sc[HIST:HIST + R, :] = x_ref[...]
    xp = xp_sc[...]
    conv = xp[HIST:] * cw_ref[GDN_CONV - 1:GDN_CONV, :]
    for j in range(GDN_CONV - 1):
        shifted = pltpu.roll(xp, GDN_CONV - 1 - j, axis=0)
        conv = conv + shifted[HIST:] * cw_ref[j:j + 1, :]
    conv = conv * jax.nn.sigmoid(conv)
    tail = xp_sc[R:R + HIST, :]
    xp_sc[0:HIST, :] = tail
    buf_out_ref[...] = tail[HIST - (GDN_CONV - 1):HIST, :]

    ba = ba_ref[...]
    lane = lax.broadcasted_iota(jnp.int32, ba.shape, 1)
    beta_all = jax.nn.sigmoid(ba)
    g_all = jnp.where((lane >= B_HEADS) & (lane < 2 * B_HEADS),
                      -jnp.exp(par_ref[0:1, :]) * jax.nn.softplus(ba + par_ref[1:2, :]), 0.0)
    rr = lax.broadcasted_iota(jnp.int32, (R, R), 0)
    cc = lax.broadcasted_iota(jnp.int32, (R, R), 1)
    same_chunk = ((rr // C) - (cc // C)) == 0
    tri = jnp.where(same_chunk, jnp.where(rr >= cc, 1.0, 0.0), 0.0).astype(F32)
    gcum_all = jnp.dot(tri, g_all, precision=_HI, preferred_element_type=F32)

    ri = lax.broadcasted_iota(jnp.int32, (C, 2 * C), 0)
    cl = lax.broadcasted_iota(jnp.int32, (C, 2 * C), 1)
    left = cl < C
    cj = jnp.where(left, cl, cl - C)
    incl = ri >= cj
    strict = ri > cj
    eye = jnp.where(ri == cj, 1.0, 0.0).astype(F32)
    inv_masks = _inverse_masks(ri, cj)
    zk = jnp.zeros((C, B_DK), F32)
    z2 = jnp.zeros((C, B_DV + B_DK), F32)
    zv = jnp.zeros((C, B_DV), F32)

    subs = range(n_sub)
    pairs = range(B_HEADS // 2)
    q, k, v, kb, beta, gc, eg, g_last, gcum_t = [], [], [], [], [], [], [], [], []
    for c in subs:
        rows = slice(c * C, (c + 1) * C)
        gsub = gcum_all[rows]
        gcum_t.append(cat([gsub, gsub], axis=0).T)
        qc, kc, vc, kbc, bc, gcc, egc = [], [], [], [], [], [], []
        for h in range(B_HEADS):
            qh = conv[rows, h * B_DK:(h + 1) * B_DK]
            kh = conv[rows, nq + h * B_DK:nq + (h + 1) * B_DK]
            qc.append(qh * lax.rsqrt(jnp.sum(qh * qh, axis=-1, keepdims=True) + EPS)
                      * (B_DK ** -0.5))
            kc.append(kh * lax.rsqrt(jnp.sum(kh * kh, axis=-1, keepdims=True) + EPS))
            vc.append(conv[rows, 2 * nq + h * B_DV:2 * nq + (h + 1) * B_DV])
            bc.append(beta_all[rows, h:h + 1])
            gcc.append(gsub[:, B_HEADS + h:B_HEADS + h + 1])
            kbc.append(kc[-1] * bc[-1])
            egc.append(jnp.exp(gcc[-1]))
        q.append(qc)
        k.append(kc)
        v.append(vc)
        kb.append(kbc)
        beta.append(bc)
        gc.append(gcc)
        eg.append(egc)
        g_last.append([g[C - 1:C, :] for g in gcc])

    lows, attns = [], []
    for c in subs:
        for p in pairs:
            a, b = 2 * p, 2 * p + 1
            g_col = jnp.where(left, gc[c][a], gc[c][b])
            g_row = jnp.where(left[0:1], gcum_t[c][B_HEADS + a:B_HEADS + a + 1, :],
                              gcum_t[c][B_HEADS + b:B_HEADS + b + 1, :])
            decay = jnp.where(incl, jnp.exp(jnp.where(incl, g_col - g_row, 0.0)), 0.0)
            lhs = cat([cat([kb[c][a], kb[c][b]], axis=1), cat([q[c][a], q[c][b]], axis=1)],
                      axis=0)
            rhs = cat([cat([k[c][a], zk], axis=1), cat([zk, k[c][b]], axis=1)], axis=0)
            m1 = lax.dot_general(lhs.astype(BF16), rhs.astype(BF16), NT_DIMS,
                                 preferred_element_type=F32)
            lows.append(jnp.where(strict, m1[:C] * decay, 0.0).astype(BF16).astype(F32))
            attns.append(m1[C:] * decay)

    tinvs = _unit_lower_inverse(lows, inv_masks, eye, left)

    sols = []
    for c in subs:
        for p in pairs:
            a, b = 2 * p, 2 * p + 1
            rhs_a = cat([v[c][a] * beta[c][a], kb[c][a] * eg[c][a]], axis=1)
            rhs_b = cat([v[c][b] * beta[c][b], kb[c][b] * eg[c][b]], axis=1)
            rhs_bd = cat([cat([rhs_a, z2], axis=1), cat([z2, rhs_b], axis=1)], axis=0)
            sols.append(_dot(tinvs[c * len(pairs) + p].astype(BF16), rhs_bd.astype(BF16)))

    s_cur = [s_sc[h] for h in range(B_HEADS)]
    for c in subs:
        rows = slice(c * C, (c + 1) * C)
        v_new, q_s = [], []
        for h in range(B_HEADS):
            sol = sols[c * len(pairs) + h // 2]
            base = (h % 2) * (B_DV + B_DK)
            u = sol[:, base:base + B_DV]
            w = sol[:, base + B_DV:base + B_DV + B_DK]
            m2 = _dot(cat([w, q[c][h] * eg[c][h]], axis=0).astype(BF16), s_cur[h].astype(BF16))
            v_new.append(u - m2[:C])
            q_s.append(m2[C:])
        o_pairs, upds = [], []
        for p in pairs:
            a, b = 2 * p, 2 * p + 1
            vn_bd = cat([cat([v_new[a], zv], axis=1), cat([zv, v_new[b]], axis=1)],
                        axis=0).astype(BF16)
            o_pairs.append(_dot(attns[c * len(pairs) + p].astype(BF16), vn_bd))
            kd_t = cat([k[c][a] * jnp.exp(g_last[c][a] - gc[c][a]),
                        k[c][b] * jnp.exp(g_last[c][b] - gc[c][b])], axis=0).T
            upds.append(_dot(kd_t.astype(BF16), vn_bd))
        for h in range(B_HEADS):
            lo = (h % 2) * B_DV
            s_cur[h] = s_cur[h] * jnp.exp(g_last[c][h]) + upds[h // 2][:, lo:lo + B_DV]
            o = q_s[h] + o_pairs[h // 2][:, lo:lo + B_DV]
            zh = z_ref[rows, h * B_DV:(h + 1) * B_DV]
            on = o * lax.rsqrt(jnp.mean(o * o, axis=-1, keepdims=True) + EPS) * ng_ref[...]
            o_ref[rows, h * B_DV:(h + 1) * B_DV] = (
                on * (zh * jax.nn.sigmoid(zh))).astype(o_ref.dtype)
    for h in range(B_HEADS):
        s_sc[h] = s_cur[h]

    @pl.when(c_idx == nc - 1)
    def _():
        s_out_ref[...] = s_sc[...].astype(s_out_ref.dtype)


def _gdn(qkv, ba, z, s0, buf, conv_w, par, norm_g, n_seq, rows_per_seq, rows_per_step):
    assert rows_per_seq % rows_per_step == 0 and rows_per_step % CHUNK == 0
    nc = rows_per_seq // rows_per_step
    nqkv = qkv.shape[1]
    row = lambda b, c: (b * nc + c, 0)
    return pl.pallas_call(
        _gdn_kernel,
        grid=(n_seq, nc),
        in_specs=[pl.BlockSpec((rows_per_step, nqkv), row),
                  pl.BlockSpec((None, GDN_CONV - 1, nqkv), lambda b, c: (b, 0, 0)),
                  pl.BlockSpec((rows_per_step, LANES), row),
                  pl.BlockSpec((rows_per_step, B_HEADS * B_DV), row),
                  pl.BlockSpec((None, B_HEADS, B_DK, B_DV), lambda b, c: (b, 0, 0, 0)),
                  pl.BlockSpec((GDN_CONV, nqkv), lambda b, c: (0, 0)),
                  pl.BlockSpec((8, LANES), lambda b, c: (0, 0)),
                  pl.BlockSpec((1, B_DV), lambda b, c: (0, 0))],
        out_specs=[pl.BlockSpec((rows_per_step, B_HEADS * B_DV), row),
                   pl.BlockSpec((None, B_HEADS, B_DK, B_DV), lambda b, c: (b, 0, 0, 0)),
                   pl.BlockSpec((None, GDN_CONV - 1, nqkv), lambda b, c: (b, 0, 0))],
        out_shape=[jax.ShapeDtypeStruct((n_seq * rows_per_seq, B_HEADS * B_DV), BF16),
                   jax.ShapeDtypeStruct((n_seq, B_HEADS, B_DK, B_DV), F32),
                   jax.ShapeDtypeStruct((n_seq, GDN_CONV - 1, nqkv), F32)],
        scratch_shapes=[pltpu.VMEM((HIST + rows_per_step, nqkv), F32),
                        pltpu.VMEM((B_HEADS, B_DK, B_DV), F32)],
        compiler_params=_cparams(2),
        name="gdn",
    )(qkv, buf, ba, z, s0, conv_w, par, norm_g)


def _merge_kernel(x_ref, oa_ref, ob_ref, ga_ref, gb_ref, wa_ref, wb_ref, wo_ref, h_ref):
    a = jnp.dot(oa_ref[...], wa_ref[...], preferred_element_type=F32)
    b = jnp.dot(ob_ref[...], wb_ref[...], preferred_element_type=F32)
    merged = jax.nn.sigmoid(ga_ref[...]) * a + jax.nn.sigmoid(gb_ref[...]) * b
    h_ref[...] = x_ref[...] + jnp.dot(merged.astype(BF16), wo_ref[...],
                                      preferred_element_type=F32)


def _merge(x, oa, ob, ga, gb, wa, wb, wo, tm):
    rows, d = x.shape
    rowspec = pl.BlockSpec((tm, d), lambda i: (i, 0))
    wspec = pl.BlockSpec((d, d), lambda i: (0, 0))
    return pl.pallas_call(
        _merge_kernel,
        grid=(rows // tm,),
        in_specs=[rowspec] * 5 + [wspec] * 3,
        out_specs=rowspec,
        out_shape=jax.ShapeDtypeStruct((rows, d), F32),
        compiler_params=_cparams(1),
        name="merge",
    )(x, oa, ob, ga, gb, wa, wb, wo)


FFN_ROW_TILE = 512


def _ffn_kernel(h_ref, g_ref, buf_g_ref, buf_v_ref, wg_ref, wv_ref, cwg_ref, cwv_ref,
                cbg_ref, cbv_ref, wd_ref, fg_ref, y_ref, nbuf_g_ref, nbuf_v_ref,
                hn_sc, ug_sc, uv_sc, acc_sc, *, tiles_per_seq, final):
    i = pl.program_id(0)
    j = pl.program_id(1)
    nj = pl.num_programs(1)
    tm = h_ref.shape[0]
    first = (i % tiles_per_seq) == 0
    nh = FFN_CONV - 1

    @pl.when(j == 0)
    def _():
        x = h_ref[...]
        ms = jnp.mean(x * x, axis=-1, keepdims=True)
        hn_sc[...] = (x * lax.rsqrt(ms + EPS) * g_ref[...]).astype(BF16)
        acc_sc[...] = jnp.zeros(acc_sc.shape, F32)

    def conv_half(u_sc, w_ref, cw_ref, cb_ref, buf_ref, nbuf_ref):
        @pl.when(first)
        def _():
            u_sc[j, HIST - nh:HIST, :] = buf_ref[...]
        u_sc[j, HIST:HIST + tm, :] = jnp.dot(hn_sc[...], w_ref[...], preferred_element_type=F32)
        out = u_sc[j, HIST:HIST + tm, :] * cw_ref[nh:nh + 1, :] + cb_ref[...]
        for t in range(nh):
            off = HIST - nh + t
            out = out + u_sc[j, off:off + tm, :] * cw_ref[t:t + 1, :]
        tail = u_sc[j, tm:tm + HIST, :]
        u_sc[j, 0:HIST, :] = tail
        nbuf_ref[j] = tail[HIST - nh:HIST, :]
        return out

    gate = conv_half(ug_sc, wg_ref, cwg_ref, cbg_ref, buf_g_ref, nbuf_g_ref)
    val = conv_half(uv_sc, wv_ref, cwv_ref, cbv_ref, buf_v_ref, nbuf_v_ref)
    act = (gate * jax.nn.sigmoid(gate) * val).astype(BF16)
    acc_sc[...] += jnp.dot(act, wd_ref[...], preferred_element_type=F32)

    @pl.when(j == nj - 1)
    def _():
        y = h_ref[...] + acc_sc[...]
        if final:
            ms = jnp.mean(y * y, axis=-1, keepdims=True)
            y = y * lax.rsqrt(ms + EPS) * fg_ref[...]
        y_ref[...] = y


def _ffn(h, g, buf, w_up, conv_w, conv_b, w_down, final_g, n_seq, rows_per_seq, tm, cw, final):
    rows, d = h.shape
    d_ff = w_down.shape[0]
    assert d_ff % cw == 0 and rows_per_seq % tm == 0
    nj = d_ff // cw
    tps = rows_per_seq // tm
    nh = FFN_CONV - 1
    gate_col = lambda i, j: (0, j)
    val_col = lambda i, j: (0, nj + j)
    seq_gate = lambda i, j: (i // tps, 0, j)
    seq_val = lambda i, j: (i // tps, 0, nj + j)
    y, nbg, nbv = pl.pallas_call(
        functools.partial(_ffn_kernel, tiles_per_seq=tps, final=final),
        grid=(rows // tm, nj),
        in_specs=[pl.BlockSpec((tm, d), lambda i, j: (i, 0)),
                  pl.BlockSpec((1, d), lambda i, j: (0, 0)),
                  pl.BlockSpec((None, nh, cw), seq_gate),
                  pl.BlockSpec((None, nh, cw), seq_val),
                  pl.BlockSpec((d, cw), gate_col),
                  pl.BlockSpec((d, cw), val_col),
                  pl.BlockSpec((FFN_CONV, cw), gate_col),
                  pl.BlockSpec((FFN_CONV, cw), val_col),
                  pl.BlockSpec((1, cw), gate_col),
                  pl.BlockSpec((1, cw), val_col),
                  pl.BlockSpec((cw, d), lambda i, j: (j, 0)),
                  pl.BlockSpec((1, d), lambda i, j: (0, 0))],
        out_specs=[pl.BlockSpec((tm, d), lambda i, j: (i, 0)),
                   pl.BlockSpec((None, nj, nh, cw), lambda i, j: (i // tps, 0, 0, 0)),
                   pl.BlockSpec((None, nj, nh, cw), lambda i, j: (i // tps, 0, 0, 0))],
        out_shape=[jax.ShapeDtypeStruct((rows, d), F32),
                   jax.ShapeDtypeStruct((n_seq, nj, nh, cw), F32),
                   jax.ShapeDtypeStruct((n_seq, nj, nh, cw), F32)],
        scratch_shapes=[pltpu.VMEM((tm, d), BF16),
                        pltpu.VMEM((nj, HIST + tm, cw), F32),
                        pltpu.VMEM((nj, HIST + tm, cw), F32),
                        pltpu.VMEM((tm, d), F32)],
        compiler_params=_cparams(2),
        name="ffn",
    )(h, g, buf, buf, w_up, w_up, conv_w, conv_w, conv_b, conv_b, w_down, final_g)
    unchunk = lambda a: a.transpose(0, 2, 1, 3).reshape(n_seq, nh, d_ff)
    return y, jnp.concatenate([unchunk(nbg), unchunk(nbv)], axis=-1)


def _prep_layer_weights(l, P):
    w_in = P["w_in"][l]
    d = w_in.shape[0]
    aw = A_HEADS * A_VDIM
    nqkv = B_HEADS * (2 * B_DK + B_DV)
    c_qkvb = 3 * aw
    c_beta = c_qkvb + nqkv
    c_z = c_beta + 2 * B_HEADS
    bw = B_HEADS * B_DV
    w_q = w_in[:, :aw] * (A_HEAD_DIM ** -0.5 * LOG2E)
    w_a = jnp.concatenate([w_q, w_in[:, aw:3 * aw]], axis=1).astype(BF16)
    w_b = jnp.concatenate([w_in[:, c_qkvb:c_beta + 2 * B_HEADS],
                           jnp.zeros((d, LANES - 2 * B_HEADS), F32)], axis=1).astype(BF16)
    w_c = w_in[:, c_z:c_z + bw + 2 * d].astype(BF16)
    lam_rows = jnp.zeros((8, LANES), F32)
    for r, name in enumerate(("lambda_q1", "lambda_k1", "lambda_q2", "lambda_k2")):
        lam_rows = lam_rows.at[r, :A_HEAD_DIM].set(P[name][l])
    par = jnp.zeros((8, LANES), F32)
    par = par.at[0, B_HEADS:2 * B_HEADS].set(P["gdn_a_log"][l])
    par = par.at[1, B_HEADS:2 * B_HEADS].set(P["gdn_dt_bias"][l])
    return dict(
        w_a=w_a, w_b=w_b, w_c=w_c, lam_rows=lam_rows, par=par,
        norm_mix_g=P["norm_mix_g"][l][None, :],
        subln_g=P["subln_g"][l][None, :],
        gdn_conv_w=P["gdn_conv_w"][l],
        gdn_norm_g=P["gdn_norm_g"][l][None, :],
        w_proj_a=P["w_proj_a"][l].astype(BF16),
        w_proj_b=P["w_proj_b"][l].astype(BF16),
        w_out=P["w_out"][l].astype(BF16),
        norm_ffn_g=P["norm_ffn_g"][l][None, :],
        w_up=P["w_up"][l].astype(BF16),
        ffn_conv_w=P["ffn_conv_w"][l],
        ffn_conv_b=P["ffn_conv_b"][l][None, :],
        w_down=P["w_down"][l].astype(BF16),
        final_g=P["final_norm_g"][None, :],
    )


def _layer(x, l, depth, kv_prev, W, n_seq, rows_per_seq, cache_k, cache_v, s0, gbuf, fbuf,
           final, tm, tq, tk):
    rows, d = x.shape
    tm = min(tm, rows)
    aw = A_HEADS * A_VDIM
    nqkv = B_HEADS * (2 * B_DK + B_DV)
    lam0 = _lambda_init(l)
    slopes = (2.0 ** (-8.0 * jnp.arange(1, A_HEADS + 1, dtype=F32) / A_HEADS)) * LOG2E
    prompt = cache_k is None

    v_kinds = ("h", "t") if prompt else ("h", BF16)
    q, k, kb, v, vx = _norm_proj(
        x, W["norm_mix_g"], W["w_a"],
        ((0, aw, (BF16,)), (aw, aw, ("h", BF16)), (2 * aw, aw, v_kinds)), tm if not prompt else tk,
        layer=l, depth=depth, carried=kv_prev)
    qkvb, ba = _norm_proj(x, W["norm_mix_g"], W["w_b"],
                          ((0, nqkv, (F32,)), (nqkv, LANES, (F32,))), tm)
    z, ga, gb = _norm_proj(x, W["norm_mix_g"], W["w_c"],
                           ((0, aw, (F32,)), (aw, d, (F32,)), (aw + d, d, (F32,))), tm)

    if prompt:
        o_a = _attn_prompt(q, kb, vx, W["lam_rows"], W["subln_g"], slopes, lam0, tq, tk)
    else:
        o_a = _attn_sample(q, kb, vx, cache_k, cache_v, l, W["lam_rows"], W["subln_g"],
                           slopes, lam0, rows_per_seq)

    o_b, s_new, gbuf_new = _gdn(qkvb, ba, z, s0, gbuf, W["gdn_conv_w"], W["par"],
                                W["gdn_norm_g"], n_seq, rows_per_seq,
                                min(GDN_ROWS_PER_STEP, rows_per_seq))

    h = _merge(x, o_a, o_b, ga, gb, W["w_proj_a"], W["w_proj_b"], W["w_out"], tm)
    d_ff = W["w_down"].shape[0]
    y, fbuf_new = _ffn(h, W["norm_ffn_g"], fbuf, W["w_up"], W["ffn_conv_w"], W["ffn_conv_b"],
                       W["w_down"], W["final_g"], n_seq, rows_per_seq,
                       min(FFN_ROW_TILE, rows_per_seq), d_ff // 2, final)
    return y, k, v, s_new, gbuf_new, fbuf_new


def kernel(x_prompt, x_sample, cache_k, cache_v, state_gdn, state_gdn_conv, state_ffn_conv, norm_mix_g, w_in, lambda_q1, lambda_k1, lambda_q2, lambda_k2, subln_g, gdn_conv_w, gdn_a_log, gdn_dt_bias, gdn_norm_g, w_proj_a, w_proj_b, w_out, norm_ffn_g, w_up, ffn_conv_w, ffn_conv_b, w_down, final_norm_g):
    P = dict(norm_mix_g=norm_mix_g, w_in=w_in, lambda_q1=lambda_q1, lambda_k1=lambda_k1,
             lambda_q2=lambda_q2, lambda_k2=lambda_k2, subln_g=subln_g, gdn_conv_w=gdn_conv_w,
             gdn_a_log=gdn_a_log, gdn_dt_bias=gdn_dt_bias, gdn_norm_g=gdn_norm_g,
             w_proj_a=w_proj_a, w_proj_b=w_proj_b, w_out=w_out, norm_ffn_g=norm_ffn_g,
             w_up=w_up, ffn_conv_w=ffn_conv_w, ffn_conv_b=ffn_conv_b, w_down=w_down,
             final_norm_g=final_norm_g)
    depth = w_in.shape[0]
    bp, tp, d = x_prompt.shape
    bs, ts, _ = x_sample.shape
    past = cache_k.shape[2]
    assert ts == CHUNK and past % CHUNK == 0 and tp % CHUNK == 0
    nqkv = state_gdn_conv.shape[-1]
    d_ff2 = state_ffn_conv.shape[-1]
    weights = [_prep_layer_weights(l, P) for l in range(depth)]
    ck = cache_k.reshape(depth, bs, past * A_HEADS, A_VDIM)
    cv = cache_v.reshape(depth, bs, past * A_HEADS, A_VDIM)

    tk = min(ATTN_KEY_TILE, tp)
    tq = min(ATTN_QUERY_BLOCK, tp)
    tm = 256

    def run(x3, nseq, rps, sample):
        assert sample or nseq == 1
        x = x3.reshape(nseq * rps, d)
        kv = tuple(jnp.zeros((depth, nseq * rps, A_HEADS, A_VDIM), F32) for _ in range(2))
        ss, gcs, fcs = [], [], []
        for l in range(depth):
            if sample:
                s0, gbuf, fbuf = state_gdn[l], state_gdn_conv[l], state_ffn_conv[l]
                c_k, c_v = ck, cv
            else:
                s0 = jnp.zeros((nseq, B_HEADS, B_DK, B_DV), F32)
                gbuf = jnp.zeros((nseq, GDN_CONV - 1, nqkv), F32)
                fbuf = jnp.zeros((nseq, FFN_CONV - 1, d_ff2), F32)
                c_k = c_v = None
            x, k_all, v_all, s_new, g_new, f_new = _layer(
                x, l, depth, kv, weights[l], nseq, rps, c_k, c_v, s0, gbuf, fbuf,
                l == depth - 1, tm, tq, tk)
            kv = (k_all, v_all)
            ss.append(s_new)
            gcs.append(g_new)
            fcs.append(f_new)
        kv_shape = (depth, nseq, rps, A_HEADS, A_VDIM)
        return (x.reshape(nseq, rps, d), kv[0].reshape(kv_shape), kv[1].reshape(kv_shape),
                jnp.stack(ss), jnp.stack(gcs), jnp.stack(fcs))

    y_p, k_p, v_p, s_p, gc_p, fc_p = run(x_prompt, bp, tp, False)
    y_s, k_s, v_s, s_s, gc_s, fc_s = run(x_sample, bs, ts, True)
    return (y_p, y_s, k_p, v_p, s_p, gc_p, fc_p, k_s, v_s, s_s, gc_s, fc_s)
```

```python
import functools
import math

import jax
import jax.numpy as jnp
from jax import lax
from jax.experimental import pallas as pl
from jax.experimental.pallas import tpu as pltpu

F32 = jnp.float32
BF16 = jnp.bfloat16

EPS = 1e-6
NEG = -1e30
LOG2E = 1.4426950408889634

CHUNK = 64
A_HEADS = 8
A_HEAD_DIM = 64
A_VDIM = 128
B_HEADS = 8
B_DK = 128
B_DV = 128
GDN_CONV = 4
FFN_CONV = 3
LANES = 128
HIST = 8

VMEM_LIMIT = 56 * 1024 * 1024


def _cparams(n_axes):
    return pltpu.CompilerParams(
        dimension_semantics=("arbitrary",) * n_axes,
        vmem_limit_bytes=VMEM_LIMIT)


def _lambda_init(layer):
    return 0.8 - 0.6 * math.exp(-0.3 * layer)


def _norm_proj_kernel(x_ref, g_ref, w_ref, *refs, segs, n_carried):
    out_refs = refs[n_carried:]
    x = x_ref[...]
    ms = jnp.mean(x * x, axis=-1, keepdims=True)
    xn = (x * lax.rsqrt(ms + EPS) * g_ref[...]).astype(BF16)
    k = 0
    for start, width, kinds in segs:
        w = w_ref[:, start:start + width]
        r = None
        for kind in kinds:
            o_ref = out_refs[k]
            k += 1
            if r is None:
                r = jnp.dot(xn, w, preferred_element_type=F32)
            if kind == "t":
                o_ref[...] = r.T.astype(o_ref.dtype)
            elif kind == "h":
                for hd in range(width // LANES):
                    o_ref[:, hd, :] = r[:, hd * LANES:(hd + 1) * LANES]
            else:
                o_ref[...] = r.astype(o_ref.dtype)


def _norm_proj(x, g, w, segs, tm, layer=0, depth=1, carried=()):
    rows, d = x.shape
    assert rows % tm == 0
    out_shape, out_specs, ksegs, h_outs = [], [], [], []
    for start, width, kinds in segs:
        kk = []
        for kind in kinds:
            if kind == "t":
                out_shape.append(jax.ShapeDtypeStruct((rows // tm, width, tm), BF16))
                out_specs.append(pl.BlockSpec((None, width, tm), lambda i: (i, 0, 0)))
                kk.append("t")
            elif kind == "h":
                h_outs.append(len(out_shape))
                out_shape.append(jax.ShapeDtypeStruct(
                    (depth, rows, width // LANES, LANES), F32))
                out_specs.append(pl.BlockSpec((None, tm, width // LANES, LANES),
                                              lambda i: (layer, i, 0, 0)))
                kk.append("h")
            else:
                out_shape.append(jax.ShapeDtypeStruct((rows, width), kind))
                out_specs.append(pl.BlockSpec((tm, width), lambda i: (i, 0)))
                kk.append("n")
        ksegs.append((start, width, tuple(kk)))
    assert len(carried) in (0, len(h_outs))
    return pl.pallas_call(
        functools.partial(_norm_proj_kernel, segs=tuple(ksegs), n_carried=len(carried)),
        grid=(rows // tm,),
        in_specs=[pl.BlockSpec((tm, d), lambda i: (i, 0)),
                  pl.BlockSpec((1, d), lambda i: (0, 0)),
                  pl.BlockSpec(w.shape, lambda i: (0, 0))]
                 + [pl.BlockSpec(memory_space=pl.ANY)] * len(carried),
        out_specs=out_specs,
        out_shape=out_shape,
        input_output_aliases={3 + n: h_outs[n] for n in range(len(carried))},
        compiler_params=_cparams(1),
        name="norm_proj",
    )(x, g, w, *carried)


def _lambda_value(lam_ref, lam0):
    a = jnp.sum(lam_ref[0:1, :] * lam_ref[1:2, :], axis=-1, keepdims=True)
    b = jnp.sum(lam_ref[2:3, :] * lam_ref[3:4, :], axis=-1, keepdims=True)
    return jnp.exp(a) - jnp.exp(b) + lam0


def _stack_maps(q):
    lane = lax.broadcasted_iota(jnp.int32, q.shape, 1)
    zero = jnp.zeros_like(q)
    return jnp.concatenate([jnp.where(lane < A_HEAD_DIM, q, zero),
                            jnp.where(lane >= A_HEAD_DIM, q, zero)], axis=0)


NT_DIMS = (((1,), (1,)), ((), ()))
ATTN_KEY_TILE = 512
ATTN_QUERY_BLOCK = 1024
SCORE_PAD = LANES


def _attn_prompt_kernel(slopes_ref, lam_ref, g_ref, q_ref, k_ref, vt_ref, o_ref,
                        qq_sc, kaug_sc, dfix_sc, sa_sc, sb_sc, m_sc, l_sc, acc_sc,
                        *, tq, tk, lam0):
    h = pl.program_id(0)
    i = pl.program_id(1)
    slope = slopes_ref[h]
    q0 = i * tq
    nt = NT_DIMS
    w = tk
    nd = tq // tk
    n_off = i * nd
    n_strips = 2 * tq // w

    qq_sc[:, :A_VDIM] = _stack_maps(q_ref[...])
    lane_q = lax.broadcasted_iota(jnp.int32, (2 * tq, LANES), 1)
    qq_sc[:, A_VDIM:] = jnp.where(lane_q < 3, 1.0, 0.0).astype(BF16)

    @pl.when(i == 0)
    def _():
        b = slope * lax.broadcasted_iota(jnp.int32, (tk, LANES), 0).astype(F32)
        b1 = b.astype(BF16).astype(F32)
        b2 = (b - b1).astype(BF16).astype(F32)
        b3 = b - b1 - b2
        lane_k = lax.broadcasted_iota(jnp.int32, (tk, LANES), 1)
        kaug_sc[...] = jnp.where(lane_k == 0, b1, jnp.where(
            lane_k == 1, b2, jnp.where(lane_k == 2, b3, 0.0))).astype(BF16)
        c = lax.broadcasted_iota(jnp.int32, (tk, w), 0)
        r = lax.broadcasted_iota(jnp.int32, (tk, w), 1)
        fix = slope * jnp.minimum(2 * (r - c), 0).astype(F32)
        dfix_sc[...] = jnp.where((c // CHUNK) <= (r // CHUNK), fix, NEG)

    m_sc[...] = jnp.full(m_sc.shape, NEG, F32)
    l_sc[...] = jnp.zeros(l_sc.shape, F32)
    acc_sc[...] = jnp.zeros(acc_sc.shape, F32)

    groups = 8

    def key_reduce(op, x):
        part = op(x.reshape(groups, tk // groups, x.shape[-1]), axis=0)
        return op(part, axis=0, keepdims=True)

    def softmax_update(st, s_ref, kappa, vtt):
        sl = slice(st * w, (st + 1) * w)
        m_prev = m_sc[:, sl]
        m_new = jnp.maximum(m_prev, key_reduce(jnp.max, s_ref[:, sl]) + kappa)
        alpha = jnp.exp2(m_prev - m_new)
        p = jnp.exp2(s_ref[:, sl] - (m_new - kappa))
        pv = jnp.dot(jnp.concatenate([vtt, jnp.ones((16, tk), BF16)], axis=0), p.astype(BF16),
                     preferred_element_type=F32)
        l_sc[:, sl] = alpha * l_sc[:, sl] + pv[A_VDIM:A_VDIM + 1]
        acc_sc[:, sl] = alpha * acc_sc[:, sl] + pv[:A_VDIM]
        m_sc[:, sl] = m_new

    def produce(j, s_ref, strips=None):
        k0 = pl.multiple_of(j * tk, tk)
        lhs = jnp.concatenate([k_ref[pl.ds(k0, tk), :], kaug_sc[...]], axis=1)
        if strips is None:
            s_ref[:, :2 * tq] = lax.dot_general(lhs, qq_sc[...], nt,
                                                preferred_element_type=F32)
        else:
            for st in strips:
                sl = slice(st * w, (st + 1) * w)
                s_ref[:, sl] = lax.dot_general(lhs, qq_sc[sl, :], nt,
                                               preferred_element_type=F32)

    def visible_strips(d):
        return [st for st in range(n_strips) if (st * w) % tq >= d * tk]

    def consume(j, s_ref):
        kappa = slope * (j * tk - q0).astype(F32)
        vtj = vt_ref[j]
        for st in range(n_strips):
            softmax_update(st, s_ref, kappa, vtj)

    def consume_overlap(d, s_ref):
        kappa = slope * (d * tk)
        vtd = vt_ref[n_off + d]
        for st in visible_strips(d):
            if (st * w) % tq == d * tk:
                sl = slice(st * w, (st + 1) * w)
                s_ref[:, sl] = s_ref[:, sl] + dfix_sc[...]
            softmax_update(st, s_ref, kappa, vtd)

    def overlap_tiles(first_ref, second_ref):
        for d in range(nd):
            cur, nxt = (first_ref, second_ref) if d % 2 == 0 else (second_ref, first_ref)
            if d + 1 < nd:
                produce(n_off + d + 1, nxt, visible_strips(d + 1))
            consume_overlap(d, cur)

    produce(0, sa_sc)

    def two_tiles(j):
        produce(j + 1, sb_sc)
        consume(j, sa_sc)
        produce(j + 2, sa_sc)
        consume(j + 1, sb_sc)

    def four_tiles(t, carry):
        two_tiles(4 * t)
        two_tiles(4 * t + 2)
        return carry

    lax.fori_loop(0, n_off // 4, four_tiles, 0)

    @pl.when(n_off % 4 >= 2)
    def _():
        two_tiles((n_off // 4) * 4)

    if nd % 2 == 0:
        overlap_tiles(sa_sc, sb_sc)
    else:
        @pl.when(n_off % 2 == 0)
        def _():
            overlap_tiles(sa_sc, sb_sc)

        @pl.when(n_off % 2 == 1)
        def _():
            produce(n_off, sb_sc)
            consume(n_off - 1, sa_sc)
            overlap_tiles(sb_sc, sa_sc)

    lam = _lambda_value(lam_ref, lam0)
    inv_l = 1.0 / l_sc[...]
    acc = acc_sc[...] * inv_l
    o = acc[:, :tq] - lam * acc[:, tq:]
    ms = jnp.mean(o * o, axis=0, keepdims=True)
    o = (o * lax.rsqrt(ms + EPS)).T
    o_ref[...] = (o * g_ref[...] * (1.0 - lam0)).astype(o_ref.dtype)


def _attn_prompt(q, kb, vt, lam_rows, subln_g, slopes, lam0, tq, tk):
    t = q.shape[0]
    assert t % tq == 0 and tq % tk == 0
    return pl.pallas_call(
        functools.partial(_attn_prompt_kernel, tq=tq, tk=tk, lam0=lam0),
        grid=(A_HEADS, t // tq),
        in_specs=[pl.BlockSpec(memory_space=pltpu.SMEM),
                  pl.BlockSpec((8, LANES), lambda h, i: (0, 0)),
                  pl.BlockSpec((1, A_VDIM), lambda h, i: (0, 0)),
                  pl.BlockSpec((tq, A_VDIM), lambda h, i: (i, h)),
                  pl.BlockSpec((t, A_VDIM), lambda h, i: (0, h)),
                  pl.BlockSpec((t // tk, A_VDIM, tk), lambda h, i: (0, h, 0))],
        out_specs=pl.BlockSpec((tq, A_VDIM), lambda h, i: (i, h)),
        out_shape=jax.ShapeDtypeStruct((t, A_HEADS * A_VDIM), BF16),
        scratch_shapes=[pltpu.VMEM((2 * tq, 2 * LANES), BF16),
                        pltpu.VMEM((tk, LANES), BF16),
                        pltpu.VMEM((tk, tk), F32),
                        pltpu.VMEM((tk, 2 * tq + SCORE_PAD), F32),
                        pltpu.VMEM((tk, 2 * tq + SCORE_PAD), F32),
                        pltpu.VMEM((1, 2 * tq), F32),
                        pltpu.VMEM((1, 2 * tq), F32),
                        pltpu.VMEM((A_VDIM, 2 * tq), F32)],
        compiler_params=_cparams(2),
        name="attn_prompt",
    )(slopes, lam_rows, subln_g, q, kb, vt)


SAMPLE_KEY_TILE = 1024


def _attn_sample_kernel(slopes_ref, lam_ref, g_ref, q_ref, kn_ref, vn_ref, kc_ref, vc_ref,
                        o_ref, m_sc, l_sc, acc_sc, *, past, lam0):
    t = pl.program_id(1)
    nt = pl.num_programs(1)
    tq = q_ref.shape[0]
    tk = kc_ref.shape[0] // A_HEADS

    @pl.when(t == 0)
    def _():
        m_sc[...] = jnp.full(m_sc.shape, NEG, F32)
        l_sc[...] = jnp.zeros(l_sc.shape, F32)
        acc_sc[...] = jnp.zeros(acc_sc.shape, F32)

    def update(h, s, v):
        m_prev = m_sc[h]
        m_new = jnp.maximum(m_prev, jnp.max(s, axis=-1, keepdims=True))
        alpha = jnp.exp2(m_prev - m_new)
        p = jnp.exp2(s - m_new)
        l_sc[h] = alpha * l_sc[h] + jnp.sum(p, axis=-1, keepdims=True)
        acc_sc[h] = alpha * acc_sc[h] + jnp.dot(p.astype(BF16), v, preferred_element_type=F32)
        m_sc[h] = m_new

    kpos = t * tk + lax.broadcasted_iota(jnp.int32, (1, tk), 1)
    rel = (kpos - past).astype(F32)
    heads = range(A_HEADS)
    qqs = [_stack_maps(q_ref[:, h * A_VDIM:(h + 1) * A_VDIM]) for h in heads]
    scores = [lax.dot_general(qqs[h], kc_ref[pl.ds(h, tk, stride=A_HEADS), :].astype(BF16),
                              NT_DIMS, preferred_element_type=F32) for h in heads]
    m_prev = [m_sc[h] for h in heads]
    m_new, alpha, p = [], [], []
    for h in heads:
        s = scores[h] + slopes_ref[h] * rel
        m_new.append(jnp.maximum(m_prev[h], jnp.max(s, axis=-1, keepdims=True)))
        alpha.append(jnp.exp2(m_prev[h] - m_new[h]))
        p.append(jnp.exp2(s - m_new[h]))
    pv = [jnp.dot(p[h].astype(BF16), vc_ref[pl.ds(h, tk, stride=A_HEADS), :].astype(BF16),
                  preferred_element_type=F32) for h in heads]
    for h in heads:
        l_sc[h] = alpha[h] * l_sc[h] + jnp.sum(p[h], axis=-1, keepdims=True)
        acc_sc[h] = alpha[h] * acc_sc[h] + pv[h]
        m_sc[h] = m_new[h]

    @pl.when(t == nt - 1)
    def _():
        lam = _lambda_value(lam_ref, lam0)
        r = lax.broadcasted_iota(jnp.int32, (2 * tq, tq), 0)
        c = lax.broadcasted_iota(jnp.int32, (2 * tq, tq), 1)
        r = jnp.where(r >= tq, r - tq, r)
        own = jnp.minimum(c, 2 * r - c).astype(F32)
        for h in range(A_HEADS):
            cols = slice(h * A_VDIM, (h + 1) * A_VDIM)
            s = lax.dot_general(qqs[h], kn_ref[:, cols], NT_DIMS, preferred_element_type=F32)
            update(h, s + slopes_ref[h] * own, vn_ref[:, cols])
            acc = acc_sc[h] * (1.0 / l_sc[h])
            o = acc[:tq] - lam * acc[tq:]
            ms = jnp.mean(o * o, axis=-1, keepdims=True)
            o_ref[:, cols] = (o * lax.rsqrt(ms + EPS) * g_ref[...]
                              * (1.0 - lam0)).astype(o_ref.dtype)


def _attn_sample(q, kb, vb, cache_k, cache_v, layer, lam_rows, subln_g, slopes, lam0, tq):
    nb, past = cache_k.shape[1], cache_k.shape[2] // A_HEADS
    tk = min(SAMPLE_KEY_TILE, past)
    assert past % tk == 0
    aw = A_HEADS * A_VDIM
    rows = pl.BlockSpec((tq, aw), lambda b, t: (b, 0))
    cache = pl.BlockSpec((None, None, tk * A_HEADS, A_VDIM), lambda b, t: (layer, b, t, 0))
    return pl.pallas_call(
        functools.partial(_attn_sample_kernel, past=past, lam0=lam0),
        grid=(nb, past // tk),
        in_specs=[pl.BlockSpec(memory_space=pltpu.SMEM),
                  pl.BlockSpec((8, LANES), lambda b, t: (0, 0)),
                  pl.BlockSpec((1, A_VDIM), lambda b, t: (0, 0)),
                  rows, rows, rows, cache, cache],
        out_specs=rows,
        out_shape=jax.ShapeDtypeStruct((nb * tq, aw), BF16),
        scratch_shapes=[pltpu.VMEM((A_HEADS, 2 * tq, 1), F32),
                        pltpu.VMEM((A_HEADS, 2 * tq, 1), F32),
                        pltpu.VMEM((A_HEADS, 2 * tq, A_VDIM), F32)],
        compiler_params=_cparams(2),
        name="attn_sample",
    )(slopes, lam_rows, subln_g, q, kb, vb, cache_k, cache_v)


_HI = lax.Precision.HIGHEST
GDN_ROWS_PER_STEP = 4 * CHUNK


def _dot(a, b):
    return jnp.dot(a, b, preferred_element_type=F32)


def _split(a):
    hi = a.astype(BF16)
    return hi, (a - hi.astype(F32)).astype(BF16)


def _block_diag2(a, left):
    zero = jnp.zeros_like(a)
    return jnp.concatenate([jnp.where(left, a, zero), jnp.where(left, zero, a)], axis=0)


def _pair_prod(a_parts, b_parts, left):
    ah, bh = a_parts[0], b_parts[0]
    bdh = _block_diag2(bh, left)
    if len(a_parts) == 1 and len(b_parts) == 1:
        return _dot(ah, bdh)
    if len(a_parts) == 1:
        return _dot(jnp.concatenate([ah, ah], axis=1),
                    jnp.concatenate([bdh, _block_diag2(b_parts[1], left)], axis=0))
    r = _dot(jnp.concatenate([ah, a_parts[1]], axis=1), jnp.concatenate([bdh, bdh], axis=0))
    if len(b_parts) == 1:
        return r
    return r + _dot(ah, _block_diag2(b_parts[1], left))


def _inverse_masks(ri, cj):
    n = ri.shape[0]
    masks = [(ri // 2) == (cj // 2)]
    b = 2
    while b < n:
        same_big = ((ri // (2 * b)) == (cj // (2 * b))).astype(jnp.int32)
        same_small = ((ri // b) == (cj // b)).astype(jnp.int32)
        masks.append((same_big - same_small) > 0)
        b *= 2
    return masks


def _unit_lower_inverse(lows, masks, eye, left):
    xs = [eye - jnp.where(masks[0], low, 0.0) for low in lows]
    for mask in masks[1:]:
        xparts = [_split(x) for x in xs]
        ys = [_pair_prod((jnp.where(mask, low, 0.0).astype(BF16),), xp, left)
              for low, xp in zip(lows, xparts)]
        xs = [x - _pair_prod(xp, _split(y), left) for x, xp, y in zip(xs, xparts, ys)]
    return xs


def _gdn_kernel(x_ref, buf_ref, ba_ref, z_ref, s0_ref, cw_ref, par_ref, ng_ref,
                o_ref, s_out_ref, buf_out_ref, xp_sc, s_sc):
    c_idx = pl.program_id(1)
    nc = pl.num_programs(1)
    R = x_ref.shape[0]
    C = CHUNK
    n_sub = R // C
    nq = B_HEADS * B_DK
    cat = jnp.concatenate

    @pl.when(c_idx == 0)
    def _():
        xp_sc[HIST - (GDN_CONV - 1):HIST, :] = buf_ref[...]
        s_sc[...] = s0_ref[...].astype(F32)

    xp_sc[HIST:HIST + R, :] = x_ref[...]
    xp = xp_sc[...]
    conv = xp[HIST:] * cw_ref[GDN_CONV - 1:GDN_CONV, :]
    for j in range(GDN_CONV - 1):
        shifted = pltpu.roll(xp, GDN_CONV - 1 - j, axis=0)
        conv = conv + shifted[HIST:] * cw_ref[j:j + 1, :]
    conv = conv * jax.nn.sigmoid(conv)
    tail = xp_sc[R:R + HIST, :]
    xp_sc[0:HIST, :] = tail
    buf_out_ref[...] = tail[HIST - (GDN_CONV - 1):HIST, :]

    ba = ba_ref[...]
    lane = lax.broadcasted_iota(jnp.int32, ba.shape, 1)
    beta_all = jax.nn.sigmoid(ba)
    g_all = jnp.where((lane >= B_HEADS) & (lane < 2 * B_HEADS),
                      -jnp.exp(par_ref[0:1, :]) * jax.nn.softplus(ba + par_ref[1:2, :]), 0.0)
    rr = lax.broadcasted_iota(jnp.int32, (R, R), 0)
    cc = lax.broadcasted_iota(jnp.int32, (R, R), 1)
    same_chunk = ((rr // C) - (cc // C)) == 0
    tri = jnp.where(same_chunk, jnp.where(rr >= cc, 1.0, 0.0), 0.0).astype(F32)
    gcum_all = jnp.dot(tri, g_all, precision=_HI, preferred_element_type=F32)

    ri = lax.broadcasted_iota(jnp.int32, (C, 2 * C), 0)
    cl = lax.broadcasted_iota(jnp.int32, (C, 2 * C), 1)
    left = cl < C
    cj = jnp.where(left, cl, cl - C)
    incl = ri >= cj
    strict = ri > cj
    eye = jnp.where(ri == cj, 1.0, 0.0).astype(F32)
    inv_masks = _inverse_masks(ri, cj)
    zk = jnp.zeros((C, B_DK), F32)
    z2 = jnp.zeros((C, B_DV + B_DK), F32)
    zv = jnp.zeros((C, B_DV), F32)

    subs = range(n_sub)
    pairs = range(B_HEADS // 2)
    q, k, v, kb, beta, gc, eg, g_last, gcum_t = [], [], [], [], [], [], [], [], []
    for c in subs:
        rows = slice(c * C, (c + 1) * C)
        gsub = gcum_all[rows]
        gcum_t.append(cat([gsub, gsub], axis=0).T)
        qc, kc, vc, kbc, bc, gcc, egc = [], [], [], [], [], [], []
        for h in range(B_HEADS):
            qh = conv[rows, h * B_DK:(h + 1) * B_DK]
            kh = conv[rows, nq + h * B_DK:nq + (h + 1) * B_DK]
            qc.append(qh * lax.rsqrt(jnp.sum(qh * qh, axis=-1, keepdims=True) + EPS)
                      * (B_DK ** -0.5))
            kc.append(kh * lax.rsqrt(jnp.sum(kh * kh, axis=-1, keepdims=True) + EPS))
            vc.append(conv[rows, 2 * nq + h * B_DV:2 * nq + (h + 1) * B_DV])
            bc.append(beta_all[rows, h:h + 1])
            gcc.append(gsub[:, B_HEADS + h:B_HEADS + h + 1])
            kbc.append(kc[-1] * bc[-1])
            egc.append(jnp.exp(gcc[-1]))
        q.append(qc)
        k.append(kc)
        v.append(vc)
        kb.append(kbc)
        beta.append(bc)
        gc.append(gcc)
        eg.append(egc)
        g_last.append([g[C - 1:C, :] for g in gcc])

    lows, attns = [], []
    for c in subs:
        for p in pairs:
            a, b = 2 * p, 2 * p + 1
            g_col = jnp.where(left, gc[c][a], gc[c][b])
            g_row = jnp.where(left[0:1], gcum_t[c][B_HEADS + a:B_HEADS + a + 1, :],
                              gcum_t[c][B_HEADS + b:B_HEADS + b + 1, :])
            decay = jnp.where(incl, jnp.exp(jnp.where(incl, g_col - g_row, 0.0)), 0.0)
            lhs = cat([cat([kb[c][a], kb[c][b]], axis=1), cat([q[c][a], q[c][b]], axis=1)],
                      axis=0)
            rhs = cat([cat([k[c][a], zk], axis=1), cat([zk, k[c][b]], axis=1)], axis=0)
            m1 = lax.dot_general(lhs.astype(BF16), rhs.astype(BF16), NT_DIMS,
                                 preferred_element_type=F32)
            lows.append(jnp.where(strict, m1[:C] * decay, 0.0).astype(BF16).astype(F32))
            attns.append(m1[C:] * decay)

    tinvs = _unit_lower_inverse(lows, inv_masks, eye, left)

    sols = []
    for c in subs:
        for p in pairs:
            a, b = 2 * p, 2 * p + 1
            rhs_a = cat([v[c][a] * beta[c][a], kb[c][a] * eg[c][a]], axis=1)
            rhs_b = cat([v[c][b] * beta[c][b], kb[c][b] * eg[c][b]], axis=1)
            rhs_bd = cat([cat([rhs_a, z2], axis=1), cat([z2, rhs_b], axis=1)], axis=0)
            sols.append(_dot(tinvs[c * len(pairs) + p].astype(BF16), rhs_bd.astype(BF16)))

    s_cur = [s_sc[h] for h in range(B_HEADS)]
    for c in subs:
        rows = slice(c * C, (c + 1) * C)
        v_new, q_s = [], []
        for h in range(B_HEADS):
            sol = sols[c * len(pairs) + h // 2]
            base = (h % 2) * (B_DV + B_DK)
            u = sol[:, base:base + B_DV]
            w = sol[:, base + B_DV:base + B_DV + B_DK]
            m2 = _dot(cat([w, q[c][h] * eg[c][h]], axis=0).astype(BF16), s_cur[h].astype(BF16))
            v_new.append(u - m2[:C])
            q_s.append(m2[C:])
        o_pairs, upds = [], []
        for p in pairs:
            a, b = 2 * p, 2 * p + 1
            vn_bd = cat([cat([v_new[a], zv], axis=1), cat([zv, v_new[b]], axis=1)],
                        axis=0).astype(BF16)
            o_pairs.append(_dot(attns[c * len(pairs) + p].astype(BF16), vn_bd))
            kd_t = cat([k[c][a] * jnp.exp(g_last[c][a] - gc[c][a]),
                        k[c][b] * jnp.exp(g_last[c][b] - gc[c][b])], axis=0).T
            upds.append(_dot(kd_t.astype(BF16), vn_bd))
        for h in range(B_HEADS):
            lo = (h % 2) * B_DV
            s_cur[h] = s_cur[h] * jnp.exp(g_last[c][h]) + upds[h // 2][:, lo:lo + B_DV]
            o = q_s[h] + o_pairs[h // 2][:, lo:lo + B_DV]
            zh = z_ref[rows, h * B_DV:(h + 1) * B_DV]
            on = o * lax.rsqrt(jnp.mean(o * o, axis=-1, keepdims=True) + EPS) * ng_ref[...]
            o_ref[rows, h * B_DV:(h + 1) * B_DV] = (
                on * (zh * jax.nn.sigmoid(zh))).astype(o_ref.dtype)
    for h in range(B_HEADS):
        s_sc[h] = s_cur[h]

    @pl.when(c_idx == nc - 1)
    def _():
        s_out_ref[...] = s_sc[...].astype(s_out_ref.dtype)


def _gdn(qkv, ba, z, s0, buf, conv_w, par, norm_g, n_seq, rows_per_seq, rows_per_step):
    assert rows_per_seq % rows_per_step == 0 and rows_per_step % CHUNK == 0
    nc = rows_per_seq // rows_per_step
    nqkv = qkv.shape[1]
    row = lambda b, c: (b * nc + c, 0)
    return pl.pallas_call(
        _gdn_kernel,
        grid=(n_seq, nc),
        in_specs=[pl.BlockSpec((rows_per_step, nqkv), row),
                  pl.BlockSpec((None, GDN_CONV - 1, nqkv), lambda b, c: (b, 0, 0)),
                  pl.BlockSpec((rows_per_step, LANES), row),
                  pl.BlockSpec((rows_per_step, B_HEADS * B_DV), row),
                  pl.BlockSpec((None, B_HEADS, B_DK, B_DV), lambda b, c: (b, 0, 0, 0)),
                  pl.BlockSpec((GDN_CONV, nqkv), lambda b, c: (0, 0)),
                  pl.BlockSpec((8, LANES), lambda b, c: (0, 0)),
                  pl.BlockSpec((1, B_DV), lambda b, c: (0, 0))],
        out_specs=[pl.BlockSpec((rows_per_step, B_HEADS * B_DV), row),
                   pl.BlockSpec((None, B_HEADS, B_DK, B_DV), lambda b, c: (b, 0, 0, 0)),
                   pl.BlockSpec((None, GDN_CONV - 1, nqkv), lambda b, c: (b, 0, 0))],
        out_shape=[jax.ShapeDtypeStruct((n_seq * rows_per_seq, B_HEADS * B_DV), BF16),
                   jax.ShapeDtypeStruct((n_seq, B_HEADS, B_DK, B_DV), F32),
                   jax.ShapeDtypeStruct((n_seq, GDN_CONV - 1, nqkv), F32)],
        scratch_shapes=[pltpu.VMEM((HIST + rows_per_step, nqkv), F32),
                        pltpu.VMEM((B_HEADS, B_DK, B_DV), F32)],
        compiler_params=_cparams(2),
        name="gdn",
    )(qkv, buf, ba, z, s0, conv_w, par, norm_g)


def _merge_kernel(x_ref, oa_ref, ob_ref, ga_ref, gb_ref, wa_ref, wb_ref, wo_ref, h_ref):
    a = jnp.dot(oa_ref[...], wa_ref[...], preferred_element_type=F32)
    b = jnp.dot(ob_ref[...], wb_ref[...], preferred_element_type=F32)
    merged = jax.nn.sigmoid(ga_ref[...]) * a + jax.nn.sigmoid(gb_ref[...]) * b
    h_ref[...] = x_ref[...] + jnp.dot(merged.astype(BF16), wo_ref[...],
                                      preferred_element_type=F32)


def _merge(x, oa, ob, ga, gb, wa, wb, wo, tm):
    rows, d = x.shape
    rowspec = pl.BlockSpec((tm, d), lambda i: (i, 0))
    wspec = pl.BlockSpec((d, d), lambda i: (0, 0))
    return pl.pallas_call(
        _merge_kernel,
        grid=(rows // tm,),
        in_specs=[rowspec] * 5 + [wspec] * 3,
        out_specs=rowspec,
        out_shape=jax.ShapeDtypeStruct((rows, d), F32),
        compiler_params=_cparams(1),
        name="merge",
    )(x, oa, ob, ga, gb, wa, wb, wo)


FFN_ROW_TILE = 512


def _ffn_kernel(h_ref, g_ref, buf_g_ref, buf_v_ref, wg_ref, wv_ref, cwg_ref, cwv_ref,
                cbg_ref, cbv_ref, wd_ref, fg_ref, y_ref, nbuf_g_ref, nbuf_v_ref,
                hn_sc, ug_sc, uv_sc, acc_sc, *, tiles_per_seq, final):
    i = pl.program_id(0)
    j = pl.program_id(1)
    nj = pl.num_programs(1)
    tm = h_ref.shape[0]
    first = (i % tiles_per_seq) == 0
    nh = FFN_CONV - 1

    @pl.when(j == 0)
    def _():
        x = h_ref[...]
        ms = jnp.mean(x * x, axis=-1, keepdims=True)
        hn_sc[...] = (x * lax.rsqrt(ms + EPS) * g_ref[...]).astype(BF16)
        acc_sc[...] = jnp.zeros(acc_sc.shape, F32)

    def conv_half(u_sc, w_ref, cw_ref, cb_ref, buf_ref, nbuf_ref):
        @pl.when(first)
        def _():
            u_sc[j, HIST - nh:HIST, :] = buf_ref[...]
        u_sc[j, HIST:HIST + tm, :] = jnp.dot(hn_sc[...], w_ref[...], preferred_element_type=F32)
        out = u_sc[j, HIST:HIST + tm, :] * cw_ref[nh:nh + 1, :] + cb_ref[...]
        for t in range(nh):
            off = HIST - nh + t
            out = out + u_sc[j, off:off + tm, :] * cw_ref[t:t + 1, :]
        tail = u_sc[j, tm:tm + HIST, :]
        u_sc[j, 0:HIST, :] = tail
        nbuf_ref[j] = tail[HIST - nh:HIST, :]
        return out

    gate = conv_half(ug_sc, wg_ref, cwg_ref, cbg_ref, buf_g_ref, nbuf_g_ref)
    val = conv_half(uv_sc, wv_ref, cwv_ref, cbv_ref, buf_v_ref, nbuf_v_ref)
    act = (gate * jax.nn.sigmoid(gate) * val).astype(BF16)
    acc_sc[...] += jnp.dot(act, wd_ref[...], preferred_element_type=F32)

    @pl.when(j == nj - 1)
    def _():
        y = h_ref[...] + acc_sc[...]
        if final:
            ms = jnp.mean(y * y, axis=-1, keepdims=True)
            y = y * lax.rsqrt(ms + EPS) * fg_ref[...]
        y_ref[...] = y


def _ffn(h, g, buf, w_up, conv_w, conv_b, w_down, final_g, n_seq, rows_per_seq, tm, cw, final):
    rows, d = h.shape
    d_ff = w_down.shape[0]
    assert d_ff % cw == 0 and rows_per_seq % tm == 0
    nj = d_ff // cw
    tps = rows_per_seq // tm
    nh = FFN_CONV - 1
    gate_col = lambda i, j: (0, j)
    val_col = lambda i, j: (0, nj + j)
    seq_gate = lambda i, j: (i // tps, 0, j)
    seq_val = lambda i, j: (i // tps, 0, nj + j)
    y, nbg, nbv = pl.pallas_call(
        functools.partial(_ffn_kernel, tiles_per_seq=tps, final=final),
        grid=(rows // tm, nj),
        in_specs=[pl.BlockSpec((tm, d), lambda i, j: (i, 0)),
                  pl.BlockSpec((1, d), lambda i, j: (0, 0)),
                  pl.BlockSpec((None, nh, cw), seq_gate),
                  pl.BlockSpec((None, nh, cw), seq_val),
                  pl.BlockSpec((d, cw), gate_col),
                  pl.BlockSpec((d, cw), val_col),
                  pl.BlockSpec((FFN_CONV, cw), gate_col),
                  pl.BlockSpec((FFN_CONV, cw), val_col),
                  pl.BlockSpec((1, cw), gate_col),
                  pl.BlockSpec((1, cw), val_col),
                  pl.BlockSpec((cw, d), lambda i, j: (j, 0)),
                  pl.BlockSpec((1, d), lambda i, j: (0, 0))],
        out_specs=[pl.BlockSpec((tm, d), lambda i, j: (i, 0)),
                   pl.BlockSpec((None, nj, nh, cw), lambda i, j: (i // tps, 0, 0, 0)),
                   pl.BlockSpec((None, nj, nh, cw), lambda i, j: (i // tps, 0, 0, 0))],
        out_shape=[jax.ShapeDtypeStruct((rows, d), F32),
                   jax.ShapeDtypeStruct((n_seq, nj, nh, cw), F32),
                   jax.ShapeDtypeStruct((n_seq, nj, nh, cw), F32)],
        scratch_shapes=[pltpu.VMEM((tm, d), BF16),
                        pltpu.VMEM((nj, HIST + tm, cw), F32),
                        pltpu.VMEM((nj, HIST + tm, cw), F32),
                        pltpu.VMEM((tm, d), F32)],
        compiler_params=_cparams(2),
        name="ffn",
    )(h, g, buf, buf, w_up, w_up, conv_w, conv_w, conv_b, conv_b, w_down, final_g)
    unchunk = lambda a: a.transpose(0, 2, 1, 3).reshape(n_seq, nh, d_ff)
    return y, jnp.concatenate([unchunk(nbg), unchunk(nbv)], axis=-1)


def _prep_layer_weights(l, P):
    w_in = P["w_in"][l]
    d = w_in.shape[0]
    aw = A_HEADS * A_VDIM
    nqkv = B_HEADS * (2 * B_DK + B_DV)
    c_qkvb = 3 * aw
    c_beta = c_qkvb + nqkv
    c_z = c_beta + 2 * B_HEADS
    bw = B_HEADS * B_DV
    w_q = w_in[:, :aw] * (A_HEAD_DIM ** -0.5 * LOG2E)
    w_a = jnp.concatenate([w_q, w_in[:, aw:3 * aw]], axis=1).astype(BF16)
    w_b = jnp.concatenate([w_in[:, c_qkvb:c_beta + 2 * B_HEADS],
                           jnp.zeros((d, LANES - 2 * B_HEADS), F32)], axis=1).astype(BF16)
    w_c = w_in[:, c_z:c_z + bw + 2 * d].astype(BF16)
    lam_rows = jnp.zeros((8, LANES), F32)
    for r, name in enumerate(("lambda_q1", "lambda_k1", "lambda_q2", "lambda_k2")):
        lam_rows = lam_rows.at[r, :A_HEAD_DIM].set(P[name][l])
    par = jnp.zeros((8, LANES), F32)
    par = par.at[0, B_HEADS:2 * B_HEADS].set(P["gdn_a_log"][l])
    par = par.at[1, B_HEADS:2 * B_HEADS].set(P["gdn_dt_bias"][l])
    return dict(
        w_a=w_a, w_b=w_b, w_c=w_c, lam_rows=lam_rows, par=par,
        norm_mix_g=P["norm_mix_g"][l][None, :],
        subln_g=P["subln_g"][l][None, :],
        gdn_conv_w=P["gdn_conv_w"][l],
        gdn_norm_g=P["gdn_norm_g"][l][None, :],
        w_proj_a=P["w_proj_a"][l].astype(BF16),
        w_proj_b=P["w_proj_b"][l].astype(BF16),
        w_out=P["w_out"][l].astype(BF16),
        norm_ffn_g=P["norm_ffn_g"][l][None, :],
        w_up=P["w_up"][l].astype(BF16),
        ffn_conv_w=P["ffn_conv_w"][l],
        ffn_conv_b=P["ffn_conv_b"][l][None, :],
        w_down=P["w_down"][l].astype(BF16),
        final_g=P["final_norm_g"][None, :],
    )


def _layer(x, l, depth, kv_prev, W, n_seq, rows_per_seq, cache_k, cache_v, s0, gbuf, fbuf,
           final, tm, tq, tk):
    rows, d = x.shape
    tm = min(tm, rows)
    aw = A_HEADS * A_VDIM
    nqkv = B_HEADS * (2 * B_DK + B_DV)
    lam0 = _lambda_init(l)
    slopes = (2.0 ** (-8.0 * jnp.arange(1, A_HEADS + 1, dtype=F32) / A_HEADS)) * LOG2E
    prompt = cache_k is None

    v_kinds = ("h", "t") if prompt else ("h", BF16)
    q, k, kb, v, vx = _norm_proj(
        x, W["norm_mix_g"], W["w_a"],
        ((0, aw, (BF16,)), (aw, aw, ("h", BF16)), (2 * aw, aw, v_kinds)), tm if not prompt else tk,
        layer=l, depth=depth, carried=kv_prev)
    qkvb, ba = _norm_proj(x, W["norm_mix_g"], W["w_b"],
                          ((0, nqkv, (F32,)), (nqkv, LANES, (F32,))), tm)
    z, ga, gb = _norm_proj(x, W["norm_mix_g"], W["w_c"],
                           ((0, aw, (F32,)), (aw, d, (F32,)), (aw + d, d, (F32,))), tm)

    if prompt:
        o_a = _attn_prompt(q, kb, vx, W["lam_rows"], W["subln_g"], slopes, lam0, tq, tk)
    else:
        o_a = _attn_sample(q, kb, vx, cache_k, cache_v, l, W["lam_rows"], W["subln_g"],
                           slopes, lam0, rows_per_seq)

    o_b, s_new, gbuf_new = _gdn(qkvb, ba, z, s0, gbuf, W["gdn_conv_w"], W["par"],
                                W["gdn_norm_g"], n_seq, rows_per_seq,
                                min(GDN_ROWS_PER_STEP, rows_per_seq))

    h = _merge(x, o_a, o_b, ga, gb, W["w_proj_a"], W["w_proj_b"], W["w_out"], tm)
    d_ff = W["w_down"].shape[0]
    y, fbuf_new = _ffn(h, W["norm_ffn_g"], fbuf, W["w_up"], W["ffn_conv_w"], W["ffn_conv_b"],
                       W["w_down"], W["final_g"], n_seq, rows_per_seq,
                       min(FFN_ROW_TILE, rows_per_seq), d_ff // 2, final)
    return y, k, v, s_new, gbuf_new, fbuf_new


def kernel(x_prompt, x_sample, cache_k, cache_v, state_gdn, state_gdn_conv, state_ffn_conv, norm_mix_g, w_in, lambda_q1, lambda_k1, lambda_q2, lambda_k2, subln_g, gdn_conv_w, gdn_a_log, gdn_dt_bias, gdn_norm_g, w_proj_a, w_proj_b, w_out, norm_ffn_g, w_up, ffn_conv_w, ffn_conv_b, w_down, final_norm_g):
    P = dict(norm_mix_g=norm_mix_g, w_in=w_in, lambda_q1=lambda_q1, lambda_k1=lambda_k1,
             lambda_q2=lambda_q2, lambda_k2=lambda_k2, subln_g=subln_g, gdn_conv_w=gdn_conv_w,
             gdn_a_log=gdn_a_log, gdn_dt_bias=gdn_dt_bias, gdn_norm_g=gdn_norm_g,
             w_proj_a=w_proj_a, w_proj_b=w_proj_b, w_out=w_out, norm_ffn_g=norm_ffn_g,
             w_up=w_up, ffn_conv_w=ffn_conv_w, ffn_conv_b=ffn_conv_b, w_down=w_down,
             final_norm_g=final_norm_g)
    depth = w_in.shape[0]
    bp, tp, d = x_prompt.shape
    bs, ts, _ = x_sample.shape
    past = cache_k.shape[2]
    assert ts == CHUNK and past % CHUNK == 0 and tp % CHUNK == 0
    nqkv = state_gdn_conv.shape[-1]
    d_ff2 = state_ffn_conv.shape[-1]
    weights = [_prep_layer_weights(l, P) for l in range(depth)]
    ck = cache_k.reshape(depth, bs, past * A_HEADS, A_VDIM)
    cv = cache_v.reshape(depth, bs, past * A_HEADS, A_VDIM)

    tk = min(ATTN_KEY_TILE, tp)
    tq = min(ATTN_QUERY_BLOCK, tp)
    tm = 256

    def run(x3, nseq, rps, sample):
        assert sample or nseq == 1
        x = x3.reshape(nseq * rps, d)
        kv = tuple(jnp.zeros((depth, nseq * rps, A_HEADS, A_VDIM), F32) for _ in range(2))
        ss, gcs, fcs = [], [], []
        for l in range(depth):
            if sample:
                s0, gbuf, fbuf = state_gdn[l], state_gdn_conv[l], state_ffn_conv[l]
                c_k, c_v = ck, cv
            else:
                s0 = jnp.zeros((nseq, B_HEADS, B_DK, B_DV), F32)
                gbuf = jnp.zeros((nseq, GDN_CONV - 1, nqkv), F32)
                fbuf = jnp.zeros((nseq, FFN_CONV - 1, d_ff2), F32)
                c_k = c_v = None
            x, k_all, v_all, s_new, g_new, f_new = _layer(
                x, l, depth, kv, weights[l], nseq, rps, c_k, c_v, s0, gbuf, fbuf,
                l == depth - 1, tm, tq, tk)
            kv = (k_all, v_all)
            ss.append(s_new)
            gcs.append(g_new)
            fcs.append(f_new)
        kv_shape = (depth, nseq, rps, A_HEADS, A_VDIM)
        return (x.reshape(nseq, rps, d), kv[0].reshape(kv_shape), kv[1].reshape(kv_shape),
                jnp.stack(ss), jnp.stack(gcs), jnp.stack(fcs))

    y_p, k_p, v_p, s_p, gc_p, fc_p = run(x_prompt, bp, tp, False)
    y_s, k_s, v_s, s_s, gc_s, fc_s = run(x_sample, bs, ts, True)
    return (y_p, y_s, k_p, v_p, s_p, gc_p, fc_p, k_s, v_s, s_s, gc_s, fc_s)
```

```python
import functools
import math

import jax
import jax.numpy as jnp
from jax import lax
from jax.experimental import pallas as pl
from jax.experimental.pallas import tpu as pltpu

F32 = jnp.float32
BF16 = jnp.bfloat16

EPS = 1e-6
NEG = -1e30
LOG2E = 1.4426950408889634

CHUNK = 64
A_HEADS = 8
A_HEAD_DIM = 64
A_VDIM = 128
B_HEADS = 8
B_DK = 128
B_DV = 128
GDN_CONV = 4
FFN_CONV = 3
LANES = 128
HIST = 8

VMEM_LIMIT = 56 * 1024 * 1024


def _cparams(n_axes):
    return pltpu.CompilerParams(
        dimension_semantics=("arbitrary",) * n_axes,
        vmem_limit_bytes=VMEM_LIMIT)


def _lambda_init(layer):
    return 0.8 - 0.6 * math.exp(-0.3 * layer)


def _norm_proj_kernel(x_ref, g_ref, w_ref, *refs, segs, n_carried):
    out_refs = refs[n_carried:]
    x = x_ref[...]
    ms = jnp.mean(x * x, axis=-1, keepdims=True)
    xn = (x * lax.rsqrt(ms + EPS) * g_ref[...]).astype(BF16)
    k = 0
    for start, width, kinds in segs:
        w = w_ref[:, start:start + width]
        r = None
        for kind in kinds:
            o_ref = out_refs[k]
            k += 1
            if r is None:
                r = jnp.dot(xn, w, preferred_element_type=F32)
            if kind == "t":
                o_ref[...] = r.T.astype(o_ref.dtype)
            elif kind == "h":
                for hd in range(width // LANES):
                    o_ref[:, hd, :] = r[:, hd * LANES:(hd + 1) * LANES]
            else:
                o_ref[...] = r.astype(o_ref.dtype)


def _norm_proj(x, g, w, segs, tm, layer=0, depth=1, carried=()):
    rows, d = x.shape
    assert rows % tm == 0
    out_shape, out_specs, ksegs, h_outs = [], [], [], []
    for start, width, kinds in segs:
        kk = []
        for kind in kinds:
            if kind == "t":
                out_shape.append(jax.ShapeDtypeStruct((rows // tm, width, tm), BF16))
                out_specs.append(pl.BlockSpec((None, width, tm), lambda i: (i, 0, 0)))
                kk.append("t")
            elif kind == "h":
                h_outs.append(len(out_shape))
                out_shape.append(jax.ShapeDtypeStruct(
                    (depth, rows, width // LANES, LANES), F32))
                out_specs.append(pl.BlockSpec((None, tm, width // LANES, LANES),
                                              lambda i: (layer, i, 0, 0)))
                kk.append("h")
            else:
                out_shape.append(jax.ShapeDtypeStruct((rows, width), kind))
                out_specs.append(pl.BlockSpec((tm, width), lambda i: (i, 0)))
                kk.append("n")
        ksegs.append((start, width, tuple(kk)))
    assert len(carried) in (0, len(h_outs))
    return pl.pallas_call(
        functools.partial(_norm_proj_kernel, segs=tuple(ksegs), n_carried=len(carried)),
        grid=(rows // tm,),
        in_specs=[pl.BlockSpec((tm, d), lambda i: (i, 0)),
                  pl.BlockSpec((1, d), lambda i: (0, 0)),
                  pl.BlockSpec(w.shape, lambda i: (0, 0))]
                 + [pl.BlockSpec(memory_space=pl.ANY)] * len(carried),
        out_specs=out_specs,
        out_shape=out_shape,
        input_output_aliases={3 + n: h_outs[n] for n in range(len(carried))},
        compiler_params=_cparams(1),
        name="norm_proj",
    )(x, g, w, *carried)


def _lambda_value(lam_ref, lam0):
    a = jnp.sum(lam_ref[0:1, :] * lam_ref[1:2, :], axis=-1, keepdims=True)
    b = jnp.sum(lam_ref[2:3, :] * lam_ref[3:4, :], axis=-1, keepdims=True)
    return jnp.exp(a) - jnp.exp(b) + lam0


def _stack_maps(q):
    lane = lax.broadcasted_iota(jnp.int32, q.shape, 1)
    zero = jnp.zeros_like(q)
    return jnp.concatenate([jnp.where(lane < A_HEAD_DIM, q, zero),
                            jnp.where(lane >= A_HEAD_DIM, q, zero)], axis=0)


NT_DIMS = (((1,), (1,)), ((), ()))
ATTN_KEY_TILE = 512
ATTN_QUERY_BLOCK = 1024


def _attn_prompt_kernel(slopes_ref, lam_ref, g_ref, q_ref, k_ref, vt_ref, o_ref,
                        qq_sc, kaug_sc, dfix_sc, sa_sc, sb_sc, m_sc, l_sc, acc_sc,
                        *, tq, tk, lam0):
    h = pl.program_id(0)
    i = pl.program_id(1)
    slope = slopes_ref[h]
    q0 = i * tq
    nt = NT_DIMS
    w = tk
    nd = tq // tk
    n_off = i * nd
    n_strips = 2 * tq // w

    qq_sc[:, :A_VDIM] = _stack_maps(q_ref[...])
    lane_q = lax.broadcasted_iota(jnp.int32, (2 * tq, LANES), 1)
    qq_sc[:, A_VDIM:] = jnp.where(lane_q < 3, 1.0, 0.0).astype(BF16)

    @pl.when(i == 0)
    def _():
        b = slope * lax.broadcasted_iota(jnp.int32, (tk, LANES), 0).astype(F32)
        b1 = b.astype(BF16).astype(F32)
        b2 = (b - b1).astype(BF16).astype(F32)
        b3 = b - b1 - b2
        lane_k = lax.broadcasted_iota(jnp.int32, (tk, LANES), 1)
        kaug_sc[...] = jnp.where(lane_k == 0, b1, jnp.where(
            lane_k == 1, b2, jnp.where(lane_k == 2, b3, 0.0))).astype(BF16)
        c = lax.broadcasted_iota(jnp.int32, (tk, w), 0)
        r = lax.broadcasted_iota(jnp.int32, (tk, w), 1)
        fix = slope * jnp.minimum(2 * (r - c), 0).astype(F32)
        dfix_sc[...] = jnp.where((c // CHUNK) <= (r // CHUNK), fix, NEG)

    m_sc[...] = jnp.full(m_sc.shape, NEG, F32)
    l_sc[...] = jnp.zeros(l_sc.shape, F32)
    acc_sc[...] = jnp.zeros(acc_sc.shape, F32)

    groups = 8

    def key_reduce(op, x):
        part = op(x.reshape(groups, tk // groups, x.shape[-1]), axis=0)
        return op(part, axis=0, keepdims=True)

    def softmax_update(st, s_ref, kappa, vtt):
        sl = slice(st * w, (st + 1) * w)
        m_prev = m_sc[:, sl]
        m_new = jnp.maximum(m_prev, key_reduce(jnp.max, s_ref[:, sl]) + kappa)
        alpha = jnp.exp2(m_prev - m_new)
        p = jnp.exp2(s_ref[:, sl] - (m_new - kappa))
        pv = jnp.dot(jnp.concatenate([vtt, jnp.ones((16, tk), BF16)], axis=0), p.astype(BF16),
                     preferred_element_type=F32)
        l_sc[:, sl] = alpha * l_sc[:, sl] + pv[A_VDIM:A_VDIM + 1]
        acc_sc[:, sl] = alpha * acc_sc[:, sl] + pv[:A_VDIM]
        m_sc[:, sl] = m_new

    def produce(j, s_ref, strips=None):
        k0 = pl.multiple_of(j * tk, tk)
        lhs = jnp.concatenate([k_ref[pl.ds(k0, tk), :], kaug_sc[...]], axis=1)
        if strips is None:
            s_ref[...] = lax.dot_general(lhs, qq_sc[...], nt, preferred_element_type=F32)
        else:
            for st in strips:
                sl = slice(st * w, (st + 1) * w)
                s_ref[:, sl] = lax.dot_general(lhs, qq_sc[sl, :], nt,
                                               preferred_element_type=F32)

    def visible_strips(d):
        return [st for st in range(n_strips) if (st * w) % tq >= d * tk]

    def consume(j, s_ref):
        kappa = slope * (j * tk - q0).astype(F32)
        vtj = vt_ref[j]
        for st in range(n_strips):
            softmax_update(st, s_ref, kappa, vtj)

    def consume_overlap(d, s_ref):
        kappa = slope * (d * tk)
        vtd = vt_ref[n_off + d]
        for st in visible_strips(d):
            if (st * w) % tq == d * tk:
                sl = slice(st * w, (st + 1) * w)
                s_ref[:, sl] = s_ref[:, sl] + dfix_sc[...]
            softmax_update(st, s_ref, kappa, vtd)

    def overlap_tiles(first_ref, second_ref):
        for d in range(nd):
            cur, nxt = (first_ref, second_ref) if d % 2 == 0 else (second_ref, first_ref)
            if d + 1 < nd:
                produce(n_off + d + 1, nxt, visible_strips(d + 1))
            consume_overlap(d, cur)

    produce(0, sa_sc)

    def two_tiles(j):
        produce(j + 1, sb_sc)
        consume(j, sa_sc)
        produce(j + 2, sa_sc)
        consume(j + 1, sb_sc)

    def four_tiles(t, carry):
        two_tiles(4 * t)
        two_tiles(4 * t + 2)
        return carry

    lax.fori_loop(0, n_off // 4, four_tiles, 0)

    @pl.when(n_off % 4 >= 2)
    def _():
        two_tiles((n_off // 4) * 4)

    if nd % 2 == 0:
        overlap_tiles(sa_sc, sb_sc)
    else:
        @pl.when(n_off % 2 == 0)
        def _():
            overlap_tiles(sa_sc, sb_sc)

        @pl.when(n_off % 2 == 1)
        def _():
            produce(n_off, sb_sc)
            consume(n_off - 1, sa_sc)
            overlap_tiles(sb_sc, sa_sc)

    lam = _lambda_value(lam_ref, lam0)
    inv_l = 1.0 / l_sc[...]
    acc = acc_sc[...] * inv_l
    o = acc[:, :tq] - lam * acc[:, tq:]
    ms = jnp.mean(o * o, axis=0, keepdims=True)
    o = (o * lax.rsqrt(ms + EPS)).T
    o_ref[...] = (o * g_ref[...] * (1.0 - lam0)).astype(o_ref.dtype)


def _attn_prompt(q, kb, vt, lam_rows, subln_g, slopes, lam0, tq, tk):
    t = q.shape[0]
    assert t % tq == 0 and tq % tk == 0
    return pl.pallas_call(
        functools.partial(_attn_prompt_kernel, tq=tq, tk=tk, lam0=lam0),
        grid=(A_HEADS, t // tq),
        in_specs=[pl.BlockSpec(memory_space=pltpu.SMEM),
                  pl.BlockSpec((8, LANES), lambda h, i: (0, 0)),
                  pl.BlockSpec((1, A_VDIM), lambda h, i: (0, 0)),
                  pl.BlockSpec((tq, A_VDIM), lambda h, i: (i, h)),
                  pl.BlockSpec((t, A_VDIM), lambda h, i: (0, h)),
                  pl.BlockSpec((t // tk, A_VDIM, tk), lambda h, i: (0, h, 0))],
        out_specs=pl.BlockSpec((tq, A_VDIM), lambda h, i: (i, h)),
        out_shape=jax.ShapeDtypeStruct((t, A_HEADS * A_VDIM), BF16),
        scratch_shapes=[pltpu.VMEM((2 * tq, 2 * LANES), BF16),
                        pltpu.VMEM((tk, LANES), BF16),
                        pltpu.VMEM((tk, tk), F32),
                        pltpu.VMEM((tk, 2 * tq), F32),
                        pltpu.VMEM((tk, 2 * tq), F32),
                        pltpu.VMEM((1, 2 * tq), F32),
                        pltpu.VMEM((1, 2 * tq), F32),
                        pltpu.VMEM((A_VDIM, 2 * tq), F32)],
        compiler_params=_cparams(2),
        name="attn_prompt",
    )(slopes, lam_rows, subln_g, q, kb, vt)


SAMPLE_KEY_TILE = 1024


def _attn_sample_kernel(slopes_ref, lam_ref, g_ref, q_ref, kn_ref, vn_ref, kc_ref, vc_ref,
                        o_ref, m_sc, l_sc, acc_sc, *, past, lam0):
    t = pl.program_id(1)
    nt = pl.num_programs(1)
    tq = q_ref.shape[0]
    tk = kc_ref.shape[0] // A_HEADS

    @pl.when(t == 0)
    def _():
        m_sc[...] = jnp.full(m_sc.shape, NEG, F32)
        l_sc[...] = jnp.zeros(l_sc.shape, F32)
        acc_sc[...] = jnp.zeros(acc_sc.shape, F32)

    def update(h, s, v):
        m_prev = m_sc[h]
        m_new = jnp.maximum(m_prev, jnp.max(s, axis=-1, keepdims=True))
        alpha = jnp.exp2(m_prev - m_new)
        p = jnp.exp2(s - m_new)
        l_sc[h] = alpha * l_sc[h] + jnp.sum(p, axis=-1, keepdims=True)
        acc_sc[h] = alpha * acc_sc[h] + jnp.dot(p.astype(BF16), v, preferred_element_type=F32)
        m_sc[h] = m_new

    kpos = t * tk + lax.broadcasted_iota(jnp.int32, (1, tk), 1)
    rel = (kpos - past).astype(F32)
    heads = range(A_HEADS)
    qqs = [_stack_maps(q_ref[:, h * A_VDIM:(h + 1) * A_VDIM]) for h in heads]
    scores = [lax.dot_general(qqs[h], kc_ref[pl.ds(h, tk, stride=A_HEADS), :].astype(BF16),
                              NT_DIMS, preferred_element_type=F32) for h in heads]
    m_prev = [m_sc[h] for h in heads]
    m_new, alpha, p = [], [], []
    for h in heads:
        s = scores[h] + slopes_ref[h] * rel
        m_new.append(jnp.maximum(m_prev[h], jnp.max(s, axis=-1, keepdims=True)))
        alpha.append(jnp.exp2(m_prev[h] - m_new[h]))
        p.append(jnp.exp2(s - m_new[h]))
    pv = [jnp.dot(p[h].astype(BF16), vc_ref[pl.ds(h, tk, stride=A_HEADS), :].astype(BF16),
                  preferred_element_type=F32) for h in heads]
    for h in heads:
        l_sc[h] = alpha[h] * l_sc[h] + jnp.sum(p[h], axis=-1, keepdims=True)
        acc_sc[h] = alpha[h] * acc_sc[h] + pv[h]
        m_sc[h] = m_new[h]

    @pl.when(t == nt - 1)
    def _():
        lam = _lambda_value(lam_ref, lam0)
        r = lax.broadcasted_iota(jnp.int32, (2 * tq, tq), 0)
        c = lax.broadcasted_iota(jnp.int32, (2 * tq, tq), 1)
        r = jnp.where(r >= tq, r - tq, r)
        own = jnp.minimum(c, 2 * r - c).astype(F32)
        for h in range(A_HEADS):
            cols = slice(h * A_VDIM, (h + 1) * A_VDIM)
            s = lax.dot_general(qqs[h], kn_ref[:, cols], NT_DIMS, preferred_element_type=F32)
            update(h, s + slopes_ref[h] * own, vn_ref[:, cols])
            acc = acc_sc[h] * (1.0 / l_sc[h])
            o = acc[:tq] - lam * acc[tq:]
            ms = jnp.mean(o * o, axis=-1, keepdims=True)
            o_ref[:, cols] = (o * lax.rsqrt(ms + EPS) * g_ref[...]
                              * (1.0 - lam0)).astype(o_ref.dtype)


def _attn_sample(q, kb, vb, cache_k, cache_v, layer, lam_rows, subln_g, slopes, lam0, tq):
    nb, past = cache_k.shape[1], cache_k.shape[2] // A_HEADS
    tk = min(SAMPLE_KEY_TILE, past)
    assert past % tk == 0
    aw = A_HEADS * A_VDIM
    rows = pl.BlockSpec((tq, aw), lambda b, t: (b, 0))
    cache = pl.BlockSpec((None, None, tk * A_HEADS, A_VDIM), lambda b, t: (layer, b, t, 0))
    return pl.pallas_call(
        functools.partial(_attn_sample_kernel, past=past, lam0=lam0),
        grid=(nb, past // tk),
        in_specs=[pl.BlockSpec(memory_space=pltpu.SMEM),
                  pl.BlockSpec((8, LANES), lambda b, t: (0, 0)),
                  pl.BlockSpec((1, A_VDIM), lambda b, t: (0, 0)),
                  rows, rows, rows, cache, cache],
        out_specs=rows,
        out_shape=jax.ShapeDtypeStruct((nb * tq, aw), BF16),
        scratch_shapes=[pltpu.VMEM((A_HEADS, 2 * tq, 1), F32),
                        pltpu.VMEM((A_HEADS, 2 * tq, 1), F32),
                        pltpu.VMEM((A_HEADS, 2 * tq, A_VDIM), F32)],
        compiler_params=_cparams(2),
        name="attn_sample",
    )(slopes, lam_rows, subln_g, q, kb, vb, cache_k, cache_v)


_HI = lax.Precision.HIGHEST
GDN_ROWS_PER_STEP = 4 * CHUNK


def _dot(a, b):
    return jnp.dot(a, b, preferred_element_type=F32)


def _split(a):
    hi = a.astype(BF16)
    return hi, (a - hi.astype(F32)).astype(BF16)


def _block_diag2(a, left):
    zero = jnp.zeros_like(a)
    return jnp.concatenate([jnp.where(left, a, zero), jnp.where(left, zero, a)], axis=0)


def _pair_prod(a_parts, b_parts, left):
    ah, bh = a_parts[0], b_parts[0]
    bdh = _block_diag2(bh, left)
    if len(a_parts) == 1 and len(b_parts) == 1:
        return _dot(ah, bdh)
    if len(a_parts) == 1:
        return _dot(jnp.concatenate([ah, ah], axis=1),
                    jnp.concatenate([bdh, _block_diag2(b_parts[1], left)], axis=0))
    r = _dot(jnp.concatenate([ah, a_parts[1]], axis=1), jnp.concatenate([bdh, bdh], axis=0))
    if len(b_parts) == 1:
        return r
    return r + _dot(ah, _block_diag2(b_parts[1], left))


def _inverse_masks(ri, cj):
    n = ri.shape[0]
    masks = [(ri // 2) == (cj // 2)]
    b = 2
    while b < n:
        same_big = ((ri // (2 * b)) == (cj // (2 * b))).astype(jnp.int32)
        same_small = ((ri // b) == (cj // b)).astype(jnp.int32)
        masks.append((same_big - same_small) > 0)
        b *= 2
    return masks


def _unit_lower_inverse(lows, masks, eye, left):
    xs = [eye - jnp.where(masks[0], low, 0.0) for low in lows]
    for mask in masks[1:]:
        xparts = [_split(x) for x in xs]
        ys = [_pair_prod((jnp.where(mask, low, 0.0).astype(BF16),), xp, left)
              for low, xp in zip(lows, xparts)]
        xs = [x - _pair_prod(xp, _split(y), left) for x, xp, y in zip(xs, xparts, ys)]
    return xs


def _gdn_kernel(x_ref, buf_ref, ba_ref, z_ref, s0_ref, cw_ref, par_ref, ng_ref,
                o_ref, s_out_ref, buf_out_ref, xp_sc, s_sc):
    c_idx = pl.program_id(1)
    nc = pl.num_programs(1)
    R = x_ref.shape[0]
    C = CHUNK
    n_sub = R // C
    nq = B_HEADS * B_DK
    cat = jnp.concatenate

    @pl.when(c_idx == 0)
    def _():
        xp_sc[HIST - (GDN_CONV - 1):HIST, :] = buf_ref[...]
        s_sc[...] = s0_ref[...].astype(F32)

    xp_sc[HIST:HIST + R, :] = x_ref[...]
    xp = xp_sc[...]
    conv = xp[HIST:] * cw_ref[GDN_CONV - 1:GDN_CONV, :]
    for j in range(GDN_CONV - 1):
        shifted = pltpu.roll(xp, GDN_CONV - 1 - j, axis=0)
        conv = conv + shifted[HIST:] * cw_ref[j:j + 1, :]
    conv = conv * jax.nn.sigmoid(conv)
    tail = xp_sc[R:R + HIST, :]
    xp_sc[0:HIST, :] = tail
    buf_out_ref[...] = tail[HIST - (GDN_CONV - 1):HIST, :]

    ba = ba_ref[...]
    lane = lax.broadcasted_iota(jnp.int32, ba.shape, 1)
    beta_all = jax.nn.sigmoid(ba)
    g_all = jnp.where((lane >= B_HEADS) & (lane < 2 * B_HEADS),
                      -jnp.exp(par_ref[0:1, :]) * jax.nn.softplus(ba + par_ref[1:2, :]), 0.0)
    rr = lax.broadcasted_iota(jnp.int32, (R, R), 0)
    cc = lax.broadcasted_iota(jnp.int32, (R, R), 1)
    same_chunk = ((rr // C) - (cc // C)) == 0
    tri = jnp.where(same_chunk, jnp.where(rr >= cc, 1.0, 0.0), 0.0).astype(F32)
    gcum_all = jnp.dot(tri, g_all, precision=_HI, preferred_element_type=F32)

    ri = lax.broadcasted_iota(jnp.int32, (C, 2 * C), 0)
    cl = lax.broadcasted_iota(jnp.int32, (C, 2 * C), 1)
    left = cl < C
    cj = jnp.where(left, cl, cl - C)
    incl = ri >= cj
    strict = ri > cj
    eye = jnp.where(ri == cj, 1.0, 0.0).astype(F32)
    inv_masks = _inverse_masks(ri, cj)
    zk = jnp.zeros((C, B_DK), F32)
    z2 = jnp.zeros((C, B_DV + B_DK), F32)
    zv = jnp.zeros((C, B_DV), F32)

    subs = range(n_sub)
    pairs = range(B_HEADS // 2)
    q, k, v, kb, beta, gc, eg, g_last, gcum_t = [], [], [], [], [], [], [], [], []
    for c in subs:
        rows = slice(c * C, (c + 1) * C)
        gsub = gcum_all[rows]
        gcum_t.append(cat([gsub, gsub], axis=0).T)
        qc, kc, vc, kbc, bc, gcc, egc = [], [], [], [], [], [], []
        for h in range(B_HEADS):
            qh = conv[rows, h * B_DK:(h + 1) * B_DK]
            kh = conv[rows, nq + h * B_DK:nq + (h + 1) * B_DK]
            qc.append(qh * lax.rsqrt(jnp.sum(qh * qh, axis=-1, keepdims=True) + EPS)
                      * (B_DK ** -0.5))
            kc.append(kh * lax.rsqrt(jnp.sum(kh * kh, axis=-1, keepdims=True) + EPS))
            vc.append(conv[rows, 2 * nq + h * B_DV:2 * nq + (h + 1) * B_DV])
            bc.append(beta_all[rows, h:h + 1])
            gcc.append(gsub[:, B_HEADS + h:B_HEADS + h + 1])
            kbc.append(kc[-1] * bc[-1])
            egc.append(jnp.exp(gcc[-1]))
        q.append(qc)
        k.append(kc)
        v.append(vc)
        kb.append(kbc)
        beta.append(bc)
        gc.append(gcc)
        eg.append(egc)
        g_last.append([g[C - 1:C, :] for g in gcc])

    lows, attns = [], []
    for c in subs:
        for p in pairs:
            a, b = 2 * p, 2 * p + 1
            g_col = jnp.where(left, gc[c][a], gc[c][b])
            g_row = jnp.where(left[0:1], gcum_t[c][B_HEADS + a:B_HEADS + a + 1, :],
                              gcum_t[c][B_HEADS + b:B_HEADS + b + 1, :])
            decay = jnp.where(incl, jnp.exp(jnp.where(incl, g_col - g_row, 0.0)), 0.0)
            lhs = cat([cat([kb[c][a], kb[c][b]], axis=1), cat([q[c][a], q[c][b]], axis=1)],
                      axis=0)
            rhs = cat([cat([k[c][a], zk], axis=1), cat([zk, k[c][b]], axis=1)], axis=0)
            m1 = lax.dot_general(lhs.astype(BF16), rhs.astype(BF16), NT_DIMS,
                                 preferred_element_type=F32)
            lows.append(jnp.where(strict, m1[:C] * decay, 0.0).astype(BF16).astype(F32))
            attns.append(m1[C:] * decay)

    tinvs = _unit_lower_inverse(lows, inv_masks, eye, left)

    sols = []
    for c in subs:
        for p in pairs:
            a, b = 2 * p, 2 * p + 1
            rhs_a = cat([v[c][a] * beta[c][a], kb[c][a] * eg[c][a]], axis=1)
            rhs_b = cat([v[c][b] * beta[c][b], kb[c][b] * eg[c][b]], axis=1)
            rhs_bd = cat([cat([rhs_a, z2], axis=1), cat([z2, rhs_b], axis=1)], axis=0)
            sols.append(_dot(tinvs[c * len(pairs) + p].astype(BF16), rhs_bd.astype(BF16)))

    s_cur = [s_sc[h] for h in range(B_HEADS)]
    for c in subs:
        rows = slice(c * C, (c + 1) * C)
        v_new, q_s = [], []
        for h in range(B_HEADS):
            sol = sols[c * len(pairs) + h // 2]
            base = (h % 2) * (B_DV + B_DK)
            u = sol[:, base:base + B_DV]
            w = sol[:, base + B_DV:base + B_DV + B_DK]
            m2 = _dot(cat([w, q[c][h] * eg[c][h]], axis=0).astype(BF16), s_cur[h].astype(BF16))
            v_new.append(u - m2[:C])
            q_s.append(m2[C:])
        o_pairs, upds = [], []
        for p in pairs:
            a, b = 2 * p, 2 * p + 1
            vn_bd = cat([cat([v_new[a], zv], axis=1), cat([zv, v_new[b]], axis=1)],
                        axis=0).astype(BF16)
            o_pairs.append(_dot(attns[c * len(pairs) + p].astype(BF16), vn_bd))
            kd_t = cat([k[c][a] * jnp.exp(g_last[c][a] - gc[c][a]),
                        k[c][b] * jnp.exp(g_last[c][b] - gc[c][b])], axis=0).T
            upds.append(_dot(kd_t.astype(BF16), vn_bd))
        for h in range(B_HEADS):
            lo = (h % 2) * B_DV
            s_cur[h] = s_cur[h] * jnp.exp(g_last[c][h]) + upds[h // 2][:, lo:lo + B_DV]
            o = q_s[h] + o_pairs[h // 2][:, lo:lo + B_DV]
            zh = z_ref[rows, h * B_DV:(h + 1) * B_DV]
            on = o * lax.rsqrt(jnp.mean(o * o, axis=-1, keepdims=True) + EPS) * ng_ref[...]
            o_ref[rows, h * B_DV:(h + 1) * B_DV] = (
                on * (zh * jax.nn.sigmoid(zh))).astype(o_ref.dtype)
    for h in range(B_HEADS):
        s_sc[h] = s_cur[h]

    @pl.when(c_idx == nc - 1)
    def _():
        s_out_ref[...] = s_sc[...].astype(s_out_ref.dtype)


def _gdn(qkv, ba, z, s0, buf, conv_w, par, norm_g, n_seq, rows_per_seq, rows_per_step):
    assert rows_per_seq % rows_per_step == 0 and rows_per_step % CHUNK == 0
    nc = rows_per_seq // rows_per_step
    nqkv = qkv.shape[1]
    row = lambda b, c: (b * nc + c, 0)
    return pl.pallas_call(
        _gdn_kernel,
        grid=(n_seq, nc),
        in_specs=[pl.BlockSpec((rows_per_step, nqkv), row),
                  pl.BlockSpec((None, GDN_CONV - 1, nqkv), lambda b, c: (b, 0, 0)),
                  pl.BlockSpec((rows_per_step, LANES), row),
                  pl.BlockSpec((rows_per_step, B_HEADS * B_DV), row),
                  pl.BlockSpec((None, B_HEADS, B_DK, B_DV), lambda b, c: (b, 0, 0, 0)),
                  pl.BlockSpec((GDN_CONV, nqkv), lambda b, c: (0, 0)),
                  pl.BlockSpec((8, LANES), lambda b, c: (0, 0)),
                  pl.BlockSpec((1, B_DV), lambda b, c: (0, 0))],
        out_specs=[pl.BlockSpec((rows_per_step, B_HEADS * B_DV), row),
                   pl.BlockSpec((None, B_HEADS, B_DK, B_DV), lambda b, c: (b, 0, 0, 0)),
                   pl.BlockSpec((None, GDN_CONV - 1, nqkv), lambda b, c: (b, 0, 0))],
        out_shape=[jax.ShapeDtypeStruct((n_seq * rows_per_seq, B_HEADS * B_DV), BF16),
                   jax.ShapeDtypeStruct((n_seq, B_HEADS, B_DK, B_DV), F32),
                   jax.ShapeDtypeStruct((n_seq, GDN_CONV - 1, nqkv), F32)],
        scratch_shapes=[pltpu.VMEM((HIST + rows_per_step, nqkv), F32),
                        pltpu.VMEM((B_HEADS, B_DK, B_DV), F32)],
        compiler_params=_cparams(2),
        name="gdn",
    )(qkv, buf, ba, z, s0, conv_w, par, norm_g)


def _merge_kernel(x_ref, oa_ref, ob_ref, ga_ref, gb_ref, wa_ref, wb_ref, wo_ref, h_ref):
    a = jnp.dot(oa_ref[...], wa_ref[...], preferred_element_type=F32)
    b = jnp.dot(ob_ref[...], wb_ref[...], preferred_element_type=F32)
    merged = jax.nn.sigmoid(ga_ref[...]) * a + jax.nn.sigmoid(gb_ref[...]) * b
    h_ref[...] = x_ref[...] + jnp.dot(merged.astype(BF16), wo_ref[...],
                                      preferred_element_type=F32)


def _merge(x, oa, ob, ga, gb, wa, wb, wo, tm):
    rows, d = x.shape
    rowspec = pl.BlockSpec((tm, d), lambda i: (i, 0))
    wspec = pl.BlockSpec((d, d), lambda i: (0, 0))
    return pl.pallas_call(
        _merge_kernel,
        grid=(rows // tm,),
        in_specs=[rowspec] * 5 + [wspec] * 3,
        out_specs=rowspec,
        out_shape=jax.ShapeDtypeStruct((rows, d), F32),
        compiler_params=_cparams(1),
        name="merge",
    )(x, oa, ob, ga, gb, wa, wb, wo)


FFN_ROW_TILE = 512


def _ffn_kernel(h_ref, g_ref, buf_g_ref, buf_v_ref, wg_ref, wv_ref, cwg_ref, cwv_ref,
                cbg_ref, cbv_ref, wd_ref, fg_ref, y_ref, nbuf_g_ref, nbuf_v_ref,
                hn_sc, ug_sc, uv_sc, acc_sc, *, tiles_per_seq, final):
    i = pl.program_id(0)
    j = pl.program_id(1)
    nj = pl.num_programs(1)
    tm = h_ref.shape[0]
    first = (i % tiles_per_seq) == 0
    nh = FFN_CONV - 1

    @pl.when(j == 0)
    def _():
        x = h_ref[...]
        ms = jnp.mean(x * x, axis=-1, keepdims=True)
        hn_sc[...] = (x * lax.rsqrt(ms + EPS) * g_ref[...]).astype(BF16)
        acc_sc[...] = jnp.zeros(acc_sc.shape, F32)

    def conv_half(u_sc, w_ref, cw_ref, cb_ref, buf_ref, nbuf_ref):
        @pl.when(first)
        def _():
            u_sc[j, HIST - nh:HIST, :] = buf_ref[...]
        u_sc[j, HIST:HIST + tm, :] = jnp.dot(hn_sc[...], w_ref[...], preferred_element_type=F32)
        out = u_sc[j, HIST:HIST + tm, :] * cw_ref[nh:nh + 1, :] + cb_ref[...]
        for t in range(nh):
            off = HIST - nh + t
            out = out + u_sc[j, off:off + tm, :] * cw_ref[t:t + 1, :]
        tail = u_sc[j, tm:tm + HIST, :]
        u_sc[j, 0:HIST, :] = tail
        nbuf_ref[j] = tail[HIST - nh:HIST, :]
        return out

    gate = conv_half(ug_sc, wg_ref, cwg_ref, cbg_ref, buf_g_ref, nbuf_g_ref)
    val = conv_half(uv_sc, wv_ref, cwv_ref, cbv_ref, buf_v_ref, nbuf_v_ref)
    act = (gate * jax.nn.sigmoid(gate) * val).astype(BF16)
    acc_sc[...] += jnp.dot(act, wd_ref[...], preferred_element_type=F32)

    @pl.when(j == nj - 1)
    def _():
        y = h_ref[...] + acc_sc[...]
        if final:
            ms = jnp.mean(y * y, axis=-1, keepdims=True)
            y = y * lax.rsqrt(ms + EPS) * fg_ref[...]
        y_ref[...] = y


def _ffn(h, g, buf, w_up, conv_w, conv_b, w_down, final_g, n_seq, rows_per_seq, tm, cw, final):
    rows, d = h.shape
    d_ff = w_down.shape[0]
    assert d_ff % cw == 0 and rows_per_seq % tm == 0
    nj = d_ff // cw
    tps = rows_per_seq // tm
    nh = FFN_CONV - 1
    gate_col = lambda i, j: (0, j)
    val_col = lambda i, j: (0, nj + j)
    seq_gate = lambda i, j: (i // tps, 0, j)
    seq_val = lambda i, j: (i // tps, 0, nj + j)
    y, nbg, nbv = pl.pallas_call(
        functools.partial(_ffn_kernel, tiles_per_seq=tps, final=final),
        grid=(rows // tm, nj),
        in_specs=[pl.BlockSpec((tm, d), lambda i, j: (i, 0)),
                  pl.BlockSpec((1, d), lambda i, j: (0, 0)),
                  pl.BlockSpec((None, nh, cw), seq_gate),
                  pl.BlockSpec((None, nh, cw), seq_val),
                  pl.BlockSpec((d, cw), gate_col),
                  pl.BlockSpec((d, cw), val_col),
                  pl.BlockSpec((FFN_CONV, cw), gate_col),
                  pl.BlockSpec((FFN_CONV, cw), val_col),
                  pl.BlockSpec((1, cw), gate_col),
                  pl.BlockSpec((1, cw), val_col),
                  pl.BlockSpec((cw, d), lambda i, j: (j, 0)),
                  pl.BlockSpec((1, d), lambda i, j: (0, 0))],
        out_specs=[pl.BlockSpec((tm, d), lambda i, j: (i, 0)),
                   pl.BlockSpec((None, nj, nh, cw), lambda i, j: (i // tps, 0, 0, 0)),
                   pl.BlockSpec((None, nj, nh, cw), lambda i, j: (i // tps, 0, 0, 0))],
        out_shape=[jax.ShapeDtypeStruct((rows, d), F32),
                   jax.ShapeDtypeStruct((n_seq, nj, nh, cw), F32),
                   jax.ShapeDtypeStruct((n_seq, nj, nh, cw), F32)],
        scratch_shapes=[pltpu.VMEM((tm, d), BF16),
                        pltpu.VMEM((nj, HIST + tm, cw), F32),
                        pltpu.VMEM((nj, HIST + tm, cw), F32),
                        pltpu.VMEM((tm, d), F32)],
        compiler_params=_cparams(2),
        name="ffn",
    )(h, g, buf, buf, w_up, w_up, conv_w, conv_w, conv_b, conv_b, w_down, final_g)
    unchunk = lambda a: a.transpose(0, 2, 1, 3).reshape(n_seq, nh, d_ff)
    return y, jnp.concatenate([unchunk(nbg), unchunk(nbv)], axis=-1)


def _prep_layer_weights(l, P):
    w_in = P["w_in"][l]
    d = w_in.shape[0]
    aw = A_HEADS * A_VDIM
    nqkv = B_HEADS * (2 * B_DK + B_DV)
    c_qkvb = 3 * aw
    c_beta = c_qkvb + nqkv
    c_z = c_beta + 2 * B_HEADS
    bw = B_HEADS * B_DV
    w_q = w_in[:, :aw] * (A_HEAD_DIM ** -0.5 * LOG2E)
    w_a = jnp.concatenate([w_q, w_in[:, aw:3 * aw]], axis=1).astype(BF16)
    w_b = jnp.concatenate([w_in[:, c_qkvb:c_beta + 2 * B_HEADS],
                           jnp.zeros((d, LANES - 2 * B_HEADS), F32)], axis=1).astype(BF16)
    w_c = w_in[:, c_z:c_z + bw + 2 * d].astype(BF16)
    lam_rows = jnp.zeros((8, LANES), F32)
    for r, name in enumerate(("lambda_q1", "lambda_k1", "lambda_q2", "lambda_k2")):
        lam_rows = lam_rows.at[r, :A_HEAD_DIM].set(P[name][l])
    par = jnp.zeros((8, LANES), F32)
    par = par.at[0, B_HEADS:2 * B_HEADS].set(P["gdn_a_log"][l])
    par = par.at[1, B_HEADS:2 * B_HEADS].set(P["gdn_dt_bias"][l])
    return dict(
        w_a=w_a, w_b=w_b, w_c=w_c, lam_rows=lam_rows, par=par,
        norm_mix_g=P["norm_mix_g"][l][None, :],
        subln_g=P["subln_g"][l][None, :],
        gdn_conv_w=P["gdn_conv_w"][l],
        gdn_norm_g=P["gdn_norm_g"][l][None, :],
        w_proj_a=P["w_proj_a"][l].astype(BF16),
        w_proj_b=P["w_proj_b"][l].astype(BF16),
        w_out=P["w_out"][l].astype(BF16),
        norm_ffn_g=P["norm_ffn_g"][l][None, :],
        w_up=P["w_up"][l].astype(BF16),
        ffn_conv_w=P["ffn_conv_w"][l],
        ffn_conv_b=P["ffn_conv_b"][l][None, :],
        w_down=P["w_down"][l].astype(BF16),
        final_g=P["final_norm_g"][None, :],
    )


def _layer(x, l, depth, kv_prev, W, n_seq, rows_per_seq, cache_k, cache_v, s0, gbuf, fbuf,
           final, tm, tq, tk):
    rows, d = x.shape
    tm = min(tm, rows)
    aw = A_HEADS * A_VDIM
    nqkv = B_HEADS * (2 * B_DK + B_DV)
    lam0 = _lambda_init(l)
    slopes = (2.0 ** (-8.0 * jnp.arange(1, A_HEADS + 1, dtype=F32) / A_HEADS)) * LOG2E
    prompt = cache_k is None

    v_kinds = ("h", "t") if prompt else ("h", BF16)
    q, k, kb, v, vx = _norm_proj(
        x, W["norm_mix_g"], W["w_a"],
        ((0, aw, (BF16,)), (aw, aw, ("h", BF16)), (2 * aw, aw, v_kinds)), tm if not prompt else tk,
        layer=l, depth=depth, carried=kv_prev)
    qkvb, ba = _norm_proj(x, W["norm_mix_g"], W["w_b"],
                          ((0, nqkv, (F32,)), (nqkv, LANES, (F32,))), tm)
    z, ga, gb = _norm_proj(x, W["norm_mix_g"], W["w_c"],
                           ((0, aw, (F32,)), (aw, d, (F32,)), (aw + d, d, (F32,))), tm)

    if prompt:
        o_a = _attn_prompt(q, kb, vx, W["lam_rows"], W["subln_g"], slopes, lam0, tq, tk)
    else:
        o_a = _attn_sample(q, kb, vx, cache_k, cache_v, l, W["lam_rows"], W["subln_g"],
                           slopes, lam0, rows_per_seq)

    o_b, s_new, gbuf_new = _gdn(qkvb, ba, z, s0, gbuf, W["gdn_conv_w"], W["par"],
                                W["gdn_norm_g"], n_seq, rows_per_seq,
                                min(GDN_ROWS_PER_STEP, rows_per_seq))

    h = _merge(x, o_a, o_b, ga, gb, W["w_proj_a"], W["w_proj_b"], W["w_out"], tm)
    d_ff = W["w_down"].shape[0]
    y, fbuf_new = _ffn(h, W["norm_ffn_g"], fbuf, W["w_up"], W["ffn_conv_w"], W["ffn_conv_b"],
                       W["w_down"], W["final_g"], n_seq, rows_per_seq,
                       min(FFN_ROW_TILE, rows_per_seq), d_ff // 2, final)
    return y, k, v, s_new, gbuf_new, fbuf_new


def kernel(x_prompt, x_sample, cache_k, cache_v, state_gdn, state_gdn_conv, state_ffn_conv, norm_mix_g, w_in, lambda_q1, lambda_k1, lambda_q2, lambda_k2, subln_g, gdn_conv_w, gdn_a_log, gdn_dt_bias, gdn_norm_g, w_proj_a, w_proj_b, w_out, norm_ffn_g, w_up, ffn_conv_w, ffn_conv_b, w_down, final_norm_g):
    P = dict(norm_mix_g=norm_mix_g, w_in=w_in, lambda_q1=lambda_q1, lambda_k1=lambda_k1,
             lambda_q2=lambda_q2, lambda_k2=lambda_k2, subln_g=subln_g, gdn_conv_w=gdn_conv_w,
             gdn_a_log=gdn_a_log, gdn_dt_bias=gdn_dt_bias, gdn_norm_g=gdn_norm_g,
             w_proj_a=w_proj_a, w_proj_b=w_proj_b, w_out=w_out, norm_ffn_g=norm_ffn_g,
             w_up=w_up, ffn_conv_w=ffn_conv_w, ffn_conv_b=ffn_conv_b, w_down=w_down,
             final_norm_g=final_norm_g)
    depth = w_in.shape[0]
    bp, tp, d = x_prompt.shape
    bs, ts, _ = x_sample.shape
    past = cache_k.shape[2]
    assert ts == CHUNK and past % CHUNK == 0 and tp % CHUNK == 0
    nqkv = state_gdn_conv.shape[-1]
    d_ff2 = state_ffn_conv.shape[-1]
    weights = [_prep_layer_weights(l, P) for l in range(depth)]
    ck = cache_k.reshape(depth, bs, past * A_HEADS, A_VDIM)
    cv = cache_v.reshape(depth, bs, past * A_HEADS, A_VDIM)

    tk = min(ATTN_KEY_TILE, tp)
    tq = min(ATTN_QUERY_BLOCK, tp)
    tm = 512

    def run(x3, nseq, rps, sample):
        assert sample or nseq == 1
        x = x3.reshape(nseq * rps, d)
        kv = tuple(jnp.zeros((depth, nseq * rps, A_HEADS, A_VDIM), F32) for _ in range(2))
        ss, gcs, fcs = [], [], []
        for l in range(depth):
            if sample:
                s0, gbuf, fbuf = state_gdn[l], state_gdn_conv[l], state_ffn_conv[l]
                c_k, c_v = ck, cv
            else:
                s0 = jnp.zeros((nseq, B_HEADS, B_DK, B_DV), F32)
                gbuf = jnp.zeros((nseq, GDN_CONV - 1, nqkv), F32)
                fbuf = jnp.zeros((nseq, FFN_CONV - 1, d_ff2), F32)
                c_k = c_v = None
            x, k_all, v_all, s_new, g_new, f_new = _layer(
                x, l, depth, kv, weights[l], nseq, rps, c_k, c_v, s0, gbuf, fbuf,
                l == depth - 1, tm, tq, tk)
            kv = (k_all, v_all)
            ss.append(s_new)
            gcs.append(g_new)
            fcs.append(f_new)
        kv_shape = (depth, nseq, rps, A_HEADS, A_VDIM)
        return (x.reshape(nseq, rps, d), kv[0].reshape(kv_shape), kv[1].reshape(kv_shape),
                jnp.stack(ss), jnp.stack(gcs), jnp.stack(fcs))

    y_p, k_p, v_p, s_p, gc_p, fc_p = run(x_prompt, bp, tp, False)
    y_s, k_s, v_s, s_s, gc_s, fc_s = run(x_sample, bs, ts, True)
    return (y_p, y_s, k_p, v_p, s_p, gc_p, fc_p, k_s, v_s, s_s, gc_s, fc_s)
```

```python
import functools
import math

import jax
import jax.numpy as jnp
from jax import lax
from jax.experimental import pallas as pl
from jax.experimental.pallas import tpu as pltpu

F32 = jnp.float32
BF16 = jnp.bfloat16

EPS = 1e-6
NEG = -1e30
LOG2E = 1.4426950408889634

CHUNK = 64
A_HEADS = 8
A_HEAD_DIM = 64
A_VDIM = 128
B_HEADS = 8
B_DK = 128
B_DV = 128
GDN_CONV = 4
FFN_CONV = 3
LANES = 128
HIST = 8

VMEM_LIMIT = 56 * 1024 * 1024


def _cparams(n_axes):
    return pltpu.CompilerParams(
        dimension_semantics=("arbitrary",) * n_axes,
        vmem_limit_bytes=VMEM_LIMIT)


def _lambda_init(layer):
    return 0.8 - 0.6 * math.exp(-0.3 * layer)


def _norm_proj_kernel(x_ref, g_ref, w_ref, *refs, segs, n_carried):
    out_refs = refs[n_carried:]
    x = x_ref[...]
    ms = jnp.mean(x * x, axis=-1, keepdims=True)
    xn = (x * lax.rsqrt(ms + EPS) * g_ref[...]).astype(BF16)
    k = 0
    for start, width, kinds in segs:
        w = w_ref[:, start:start + width]
        r = None
        for kind in kinds:
            o_ref = out_refs[k]
            k += 1
            if r is None:
                r = jnp.dot(xn, w, preferred_element_type=F32)
            if kind == "t":
                o_ref[...] = r.T.astype(o_ref.dtype)
            elif kind == "h":
                for hd in range(width // LANES):
                    o_ref[:, hd, :] = r[:, hd * LANES:(hd + 1) * LANES]
            else:
                o_ref[...] = r.astype(o_ref.dtype)


def _norm_proj(x, g, w, segs, tm, layer=0, depth=1, carried=()):
    rows, d = x.shape
    assert rows % tm == 0
    out_shape, out_specs, ksegs, h_outs = [], [], [], []
    for start, width, kinds in segs:
        kk = []
        for kind in kinds:
            if kind == "t":
                out_shape.append(jax.ShapeDtypeStruct((rows // tm, width, tm), BF16))
                out_specs.append(pl.BlockSpec((None, width, tm), lambda i: (i, 0, 0)))
                kk.append("t")
            elif kind == "h":
                h_outs.append(len(out_shape))
                out_shape.append(jax.ShapeDtypeStruct(
                    (depth, rows, width // LANES, LANES), F32))
                out_specs.append(pl.BlockSpec((None, tm, width // LANES, LANES),
                                              lambda i: (layer, i, 0, 0)))
                kk.append("h")
            else:
                out_shape.append(jax.ShapeDtypeStruct((rows, width), kind))
                out_specs.append(pl.BlockSpec((tm, width), lambda i: (i, 0)))
                kk.append("n")
        ksegs.append((start, width, tuple(kk)))
    assert len(carried) in (0, len(h_outs))
    return pl.pallas_call(
        functools.partial(_norm_proj_kernel, segs=tuple(ksegs), n_carried=len(carried)),
        grid=(rows // tm,),
        in_specs=[pl.BlockSpec((tm, d), lambda i: (i, 0)),
                  pl.BlockSpec((1, d), lambda i: (0, 0)),
                  pl.BlockSpec(w.shape, lambda i: (0, 0))]
                 + [pl.BlockSpec(memory_space=pl.ANY)] * len(carried),
        out_specs=out_specs,
        out_shape=out_shape,
        input_output_aliases={3 + n: h_outs[n] for n in range(len(carried))},
        compiler_params=_cparams(1),
        name="norm_proj",
    )(x, g, w, *carried)


def _lambda_value(lam_ref, lam0):
    a = jnp.sum(lam_ref[0:1, :] * lam_ref[1:2, :], axis=-1, keepdims=True)
    b = jnp.sum(lam_ref[2:3, :] * lam_ref[3:4, :], axis=-1, keepdims=True)
    return jnp.exp(a) - jnp.exp(b) + lam0


def _stack_maps(q):
    lane = lax.broadcasted_iota(jnp.int32, q.shape, 1)
    zero = jnp.zeros_like(q)
    return jnp.concatenate([jnp.where(lane < A_HEAD_DIM, q, zero),
                            jnp.where(lane >= A_HEAD_DIM, q, zero)], axis=0)


NT_DIMS = (((1,), (1,)), ((), ()))
ATTN_KEY_TILE = 512
ATTN_QUERY_BLOCK = 1024


def _attn_prompt_kernel(slopes_ref, lam_ref, g_ref, q_ref, k_ref, vt_ref, o_ref,
                        qq_sc, kaug_sc, dfix_sc, sa_sc, sb_sc, m_sc, l_sc, acc_sc,
                        *, tq, tk, lam0):
    h = pl.program_id(0)
    i = pl.program_id(1)
    slope = slopes_ref[h]
    q0 = i * tq
    nt = NT_DIMS
    w = tk
    nd = tq // tk
    n_off = i * nd
    n_strips = 2 * tq // w

    q_t = q_ref[...].astype(F32).T
    feat = lax.broadcasted_iota(jnp.int32, q_t.shape, 0)
    qq_sc[:A_VDIM, :] = jnp.concatenate(
        [jnp.where(feat < A_HEAD_DIM, q_t, 0.0), jnp.where(feat >= A_HEAD_DIM, q_t, 0.0)],
        axis=1).astype(BF16)
    row_q = lax.broadcasted_iota(jnp.int32, (LANES, 2 * tq), 0)
    qq_sc[A_VDIM:, :] = jnp.where(row_q < 3, 1.0, 0.0).astype(BF16)

    @pl.when(i == 0)
    def _():
        b = slope * lax.broadcasted_iota(jnp.int32, (tk, LANES), 0).astype(F32)
        b1 = b.astype(BF16).astype(F32)
        b2 = (b - b1).astype(BF16).astype(F32)
        b3 = b - b1 - b2
        lane_k = lax.broadcasted_iota(jnp.int32, (tk, LANES), 1)
        kaug_sc[...] = jnp.where(lane_k == 0, b1, jnp.where(
            lane_k == 1, b2, jnp.where(lane_k == 2, b3, 0.0))).astype(BF16)
        c = lax.broadcasted_iota(jnp.int32, (tk, w), 0)
        r = lax.broadcasted_iota(jnp.int32, (tk, w), 1)
        fix = slope * jnp.minimum(2 * (r - c), 0).astype(F32)
        dfix_sc[...] = jnp.where((c // CHUNK) <= (r // CHUNK), fix, NEG)

    m_sc[...] = jnp.full(m_sc.shape, NEG, F32)
    l_sc[...] = jnp.zeros(l_sc.shape, F32)
    acc_sc[...] = jnp.zeros(acc_sc.shape, F32)

    groups = 8

    def key_reduce(op, x):
        part = op(x.reshape(groups, tk // groups, x.shape[-1]), axis=0)
        return op(part, axis=0, keepdims=True)

    def softmax_update(st, s_ref, kappa, vtt):
        sl = slice(st * w, (st + 1) * w)
        m_prev = m_sc[:, sl]
        m_new = jnp.maximum(m_prev, key_reduce(jnp.max, s_ref[:, sl]) + kappa)
        alpha = jnp.exp2(m_prev - m_new)
        p = jnp.exp2(s_ref[:, sl] - (m_new - kappa))
        pv = jnp.dot(jnp.concatenate([vtt, jnp.ones((16, tk), BF16)], axis=0), p.astype(BF16),
                     preferred_element_type=F32)
        l_sc[:, sl] = alpha * l_sc[:, sl] + pv[A_VDIM:A_VDIM + 1]
        acc_sc[:, sl] = alpha * acc_sc[:, sl] + pv[:A_VDIM]
        m_sc[:, sl] = m_new

    def produce(j, s_ref, strips=None):
        k0 = pl.multiple_of(j * tk, tk)
        lhs = jnp.concatenate([k_ref[pl.ds(k0, tk), :], kaug_sc[...]], axis=1)
        if strips is None:
            s_ref[...] = jnp.dot(lhs, qq_sc[...], preferred_element_type=F32)
        else:
            for st in strips:
                sl = slice(st * w, (st + 1) * w)
                s_ref[:, sl] = jnp.dot(lhs, qq_sc[:, sl], preferred_element_type=F32)

    def visible_strips(d):
        return [st for st in range(n_strips) if (st * w) % tq >= d * tk]

    def consume(j, s_ref):
        kappa = slope * (j * tk - q0).astype(F32)
        vtj = vt_ref[j]
        for st in range(n_strips):
            softmax_update(st, s_ref, kappa, vtj)

    def consume_overlap(d, s_ref):
        kappa = slope * (d * tk)
        vtd = vt_ref[n_off + d]
        for st in visible_strips(d):
            if (st * w) % tq == d * tk:
                sl = slice(st * w, (st + 1) * w)
                s_ref[:, sl] = s_ref[:, sl] + dfix_sc[...]
            softmax_update(st, s_ref, kappa, vtd)

    def overlap_tiles(first_ref, second_ref):
        for d in range(nd):
            cur, nxt = (first_ref, second_ref) if d % 2 == 0 else (second_ref, first_ref)
            if d + 1 < nd:
                produce(n_off + d + 1, nxt, visible_strips(d + 1))
            consume_overlap(d, cur)

    produce(0, sa_sc)

    def two_tiles(j):
        produce(j + 1, sb_sc)
        consume(j, sa_sc)
        produce(j + 2, sa_sc)
        consume(j + 1, sb_sc)

    def four_tiles(t, carry):
        two_tiles(4 * t)
        two_tiles(4 * t + 2)
        return carry

    lax.fori_loop(0, n_off // 4, four_tiles, 0)

    @pl.when(n_off % 4 >= 2)
    def _():
        two_tiles((n_off // 4) * 4)

    if nd % 2 == 0:
        overlap_tiles(sa_sc, sb_sc)
    else:
        @pl.when(n_off % 2 == 0)
        def _():
            overlap_tiles(sa_sc, sb_sc)

        @pl.when(n_off % 2 == 1)
        def _():
            produce(n_off, sb_sc)
            consume(n_off - 1, sa_sc)
            overlap_tiles(sb_sc, sa_sc)

    lam = _lambda_value(lam_ref, lam0)
    inv_l = 1.0 / l_sc[...]
    acc = acc_sc[...] * inv_l
    o = acc[:, :tq] - lam * acc[:, tq:]
    ms = jnp.mean(o * o, axis=0, keepdims=True)
    o = (o * lax.rsqrt(ms + EPS)).T
    o_ref[...] = (o * g_ref[...] * (1.0 - lam0)).astype(o_ref.dtype)


def _attn_prompt(q, kb, vt, lam_rows, subln_g, slopes, lam0, tq, tk):
    t = q.shape[0]
    assert t % tq == 0 and tq % tk == 0
    return pl.pallas_call(
        functools.partial(_attn_prompt_kernel, tq=tq, tk=tk, lam0=lam0),
        grid=(A_HEADS, t // tq),
        in_specs=[pl.BlockSpec(memory_space=pltpu.SMEM),
                  pl.BlockSpec((8, LANES), lambda h, i: (0, 0)),
                  pl.BlockSpec((1, A_VDIM), lambda h, i: (0, 0)),
                  pl.BlockSpec((tq, A_VDIM), lambda h, i: (i, h)),
                  pl.BlockSpec((t, A_VDIM), lambda h, i: (0, h)),
                  pl.BlockSpec((t // tk, A_VDIM, tk), lambda h, i: (0, h, 0))],
        out_specs=pl.BlockSpec((tq, A_VDIM), lambda h, i: (i, h)),
        out_shape=jax.ShapeDtypeStruct((t, A_HEADS * A_VDIM), BF16),
        scratch_shapes=[pltpu.VMEM((2 * LANES, 2 * tq), BF16),
                        pltpu.VMEM((tk, LANES), BF16),
                        pltpu.VMEM((tk, tk), F32),
                        pltpu.VMEM((tk, 2 * tq), F32),
                        pltpu.VMEM((tk, 2 * tq), F32),
                        pltpu.VMEM((1, 2 * tq), F32),
                        pltpu.VMEM((1, 2 * tq), F32),
                        pltpu.VMEM((A_VDIM, 2 * tq), F32)],
        compiler_params=_cparams(2),
        name="attn_prompt",
    )(slopes, lam_rows, subln_g, q, kb, vt)


SAMPLE_KEY_TILE = 1024


def _attn_sample_kernel(slopes_ref, lam_ref, g_ref, q_ref, kn_ref, vn_ref, kc_ref, vc_ref,
                        o_ref, m_sc, l_sc, acc_sc, *, past, lam0):
    t = pl.program_id(1)
    nt = pl.num_programs(1)
    tq = q_ref.shape[0]
    tk = kc_ref.shape[0] // A_HEADS

    @pl.when(t == 0)
    def _():
        m_sc[...] = jnp.full(m_sc.shape, NEG, F32)
        l_sc[...] = jnp.zeros(l_sc.shape, F32)
        acc_sc[...] = jnp.zeros(acc_sc.shape, F32)

    def update(h, s, v):
        m_prev = m_sc[h]
        m_new = jnp.maximum(m_prev, jnp.max(s, axis=-1, keepdims=True))
        alpha = jnp.exp2(m_prev - m_new)
        p = jnp.exp2(s - m_new)
        l_sc[h] = alpha * l_sc[h] + jnp.sum(p, axis=-1, keepdims=True)
        acc_sc[h] = alpha * acc_sc[h] + jnp.dot(p.astype(BF16), v, preferred_element_type=F32)
        m_sc[h] = m_new

    kpos = t * tk + lax.broadcasted_iota(jnp.int32, (1, tk), 1)
    rel = (kpos - past).astype(F32)
    heads = range(A_HEADS)
    qqs = [_stack_maps(q_ref[:, h * A_VDIM:(h + 1) * A_VDIM]) for h in heads]
    scores = [lax.dot_general(qqs[h], kc_ref[pl.ds(h, tk, stride=A_HEADS), :].astype(BF16),
                              NT_DIMS, preferred_element_type=F32) for h in heads]
    m_prev = [m_sc[h] for h in heads]
    m_new, alpha, p = [], [], []
    for h in heads:
        s = scores[h] + slopes_ref[h] * rel
        m_new.append(jnp.maximum(m_prev[h], jnp.max(s, axis=-1, keepdims=True)))
        alpha.append(jnp.exp2(m_prev[h] - m_new[h]))
        p.append(jnp.exp2(s - m_new[h]))
    pv = [jnp.dot(p[h].astype(BF16), vc_ref[pl.ds(h, tk, stride=A_HEADS), :].astype(BF16),
                  preferred_element_type=F32) for h in heads]
    for h in heads:
        l_sc[h] = alpha[h] * l_sc[h] + jnp.sum(p[h], axis=-1, keepdims=True)
        acc_sc[h] = alpha[h] * acc_sc[h] + pv[h]
        m_sc[h] = m_new[h]

    @pl.when(t == nt - 1)
    def _():
        lam = _lambda_value(lam_ref, lam0)
        r = lax.broadcasted_iota(jnp.int32, (2 * tq, tq), 0)
        c = lax.broadcasted_iota(jnp.int32, (2 * tq, tq), 1)
        r = jnp.where(r >= tq, r - tq, r)
        own = jnp.minimum(c, 2 * r - c).astype(F32)
        for h in range(A_HEADS):
            cols = slice(h * A_VDIM, (h + 1) * A_VDIM)
            s = lax.dot_general(qqs[h], kn_ref[:, cols], NT_DIMS, preferred_element_type=F32)
            update(h, s + slopes_ref[h] * own, vn_ref[:, cols])
            acc = acc_sc[h] * (1.0 / l_sc[h])
            o = acc[:tq] - lam * acc[tq:]
            ms = jnp.mean(o * o, axis=-1, keepdims=True)
            o_ref[:, cols] = (o * lax.rsqrt(ms + EPS) * g_ref[...]
                              * (1.0 - lam0)).astype(o_ref.dtype)


def _attn_sample(q, kb, vb, cache_k, cache_v, layer, lam_rows, subln_g, slopes, lam0, tq):
    nb, past = cache_k.shape[1], cache_k.shape[2] // A_HEADS
    tk = min(SAMPLE_KEY_TILE, past)
    assert past % tk == 0
    aw = A_HEADS * A_VDIM
    rows = pl.BlockSpec((tq, aw), lambda b, t: (b, 0))
    cache = pl.BlockSpec((None, None, tk * A_HEADS, A_VDIM), lambda b, t: (layer, b, t, 0))
    return pl.pallas_call(
        functools.partial(_attn_sample_kernel, past=past, lam0=lam0),
        grid=(nb, past // tk),
        in_specs=[pl.BlockSpec(memory_space=pltpu.SMEM),
                  pl.BlockSpec((8, LANES), lambda b, t: (0, 0)),
                  pl.BlockSpec((1, A_VDIM), lambda b, t: (0, 0)),
                  rows, rows, rows, cache, cache],
        out_specs=rows,
        out_shape=jax.ShapeDtypeStruct((nb * tq, aw), BF16),
        scratch_shapes=[pltpu.VMEM((A_HEADS, 2 * tq, 1), F32),
                        pltpu.VMEM((A_HEADS, 2 * tq, 1), F32),
                        pltpu.VMEM((A_HEADS, 2 * tq, A_VDIM), F32)],
        compiler_params=_cparams(2),
        name="attn_sample",
    )(slopes, lam_rows, subln_g, q, kb, vb, cache_k, cache_v)


_HI = lax.Precision.HIGHEST
GDN_ROWS_PER_STEP = 8 * CHUNK


def _dot(a, b):
    return jnp.dot(a, b, preferred_element_type=F32)


def _split(a):
    hi = a.astype(BF16)
    return hi, (a - hi.astype(F32)).astype(BF16)


def _block_diag2(a, left):
    zero = jnp.zeros_like(a)
    return jnp.concatenate([jnp.where(left, a, zero), jnp.where(left, zero, a)], axis=0)


def _pair_prod(a_parts, b_parts, left):
    ah, bh = a_parts[0], b_parts[0]
    bdh = _block_diag2(bh, left)
    if len(a_parts) == 1 and len(b_parts) == 1:
        return _dot(ah, bdh)
    if len(a_parts) == 1:
        return _dot(jnp.concatenate([ah, ah], axis=1),
                    jnp.concatenate([bdh, _block_diag2(b_parts[1], left)], axis=0))
    r = _dot(jnp.concatenate([ah, a_parts[1]], axis=1), jnp.concatenate([bdh, bdh], axis=0))
    if len(b_parts) == 1:
        return r
    return r + _dot(ah, _block_diag2(b_parts[1], left))


def _inverse_masks(ri, cj):
    n = ri.shape[0]
    masks = [(ri // 2) == (cj // 2)]
    b = 2
    while b < n:
        same_big = ((ri // (2 * b)) == (cj // (2 * b))).astype(jnp.int32)
        same_small = ((ri // b) == (cj // b)).astype(jnp.int32)
        masks.append((same_big - same_small) > 0)
        b *= 2
    return masks


def _unit_lower_inverse(lows, masks, eye, left):
    xs = [eye - jnp.where(masks[0], low, 0.0) for low in lows]
    for mask in masks[1:]:
        xparts = [_split(x) for x in xs]
        ys = [_pair_prod((jnp.where(mask, low, 0.0).astype(BF16),), xp, left)
              for low, xp in zip(lows, xparts)]
        xs = [x - _pair_prod(xp, _split(y), left) for x, xp, y in zip(xs, xparts, ys)]
    return xs


def _gdn_kernel(x_ref, buf_ref, ba_ref, z_ref, s0_ref, cw_ref, par_ref, ng_ref,
                o_ref, s_out_ref, buf_out_ref, xp_sc, s_sc):
    c_idx = pl.program_id(1)
    nc = pl.num_programs(1)
    R = x_ref.shape[0]
    C = CHUNK
    n_sub = R // C
    nq = B_HEADS * B_DK
    cat = jnp.concatenate

    @pl.when(c_idx == 0)
    def _():
        xp_sc[HIST - (GDN_CONV - 1):HIST, :] = buf_ref[...]
        s_sc[...] = s0_ref[...].astype(F32)

    xp_sc[HIST:HIST + R, :] = x_ref[...]
    xp = xp_sc[...]
    conv = xp[HIST:] * cw_ref[GDN_CONV - 1:GDN_CONV, :]
    for j in range(GDN_CONV - 1):
        shifted = pltpu.roll(xp, GDN_CONV - 1 - j, axis=0)
        conv = conv + shifted[HIST:] * cw_ref[j:j + 1, :]
    conv = conv * jax.nn.sigmoid(conv)
    tail = xp_sc[R:R + HIST, :]
    xp_sc[0:HIST, :] = tail
    buf_out_ref[...] = tail[HIST - (GDN_CONV - 1):HIST, :]

    ba = ba_ref[...]
    lane = lax.broadcasted_iota(jnp.int32, ba.shape, 1)
    beta_all = jax.nn.sigmoid(ba)
    g_all = jnp.where((lane >= B_HEADS) & (lane < 2 * B_HEADS),
                      -jnp.exp(par_ref[0:1, :]) * jax.nn.softplus(ba + par_ref[1:2, :]), 0.0)
    rr = lax.broadcasted_iota(jnp.int32, (R, R), 0)
    cc = lax.broadcasted_iota(jnp.int32, (R, R), 1)
    same_chunk = ((rr // C) - (cc // C)) == 0
    tri = jnp.where(same_chunk, jnp.where(rr >= cc, 1.0, 0.0), 0.0).astype(F32)
    gcum_all = jnp.dot(tri, g_all, precision=_HI, preferred_element_type=F32)

    ri = lax.broadcasted_iota(jnp.int32, (C, 2 * C), 0)
    cl = lax.broadcasted_iota(jnp.int32, (C, 2 * C), 1)
    left = cl < C
    cj = jnp.where(left, cl, cl - C)
    incl = ri >= cj
    strict = ri > cj
    eye = jnp.where(ri == cj, 1.0, 0.0).astype(F32)
    inv_masks = _inverse_masks(ri, cj)
    zk = jnp.zeros((C, B_DK), F32)
    z2 = jnp.zeros((C, B_DV + B_DK), F32)
    zv = jnp.zeros((C, B_DV), F32)

    subs = range(n_sub)
    pairs = range(B_HEADS // 2)
    q, k, v, kb, beta, gc, eg, g_last, gcum_t = [], [], [], [], [], [], [], [], []
    for c in subs:
        rows = slice(c * C, (c + 1) * C)
        gsub = gcum_all[rows]
        gcum_t.append(cat([gsub, gsub], axis=0).T)
        qc, kc, vc, kbc, bc, gcc, egc = [], [], [], [], [], [], []
        for h in range(B_HEADS):
            qh = conv[rows, h * B_DK:(h + 1) * B_DK]
            kh = conv[rows, nq + h * B_DK:nq + (h + 1) * B_DK]
            qc.append(qh * lax.rsqrt(jnp.sum(qh * qh, axis=-1, keepdims=True) + EPS)
                      * (B_DK ** -0.5))
            kc.append(kh * lax.rsqrt(jnp.sum(kh * kh, axis=-1, keepdims=True) + EPS))
            vc.append(conv[rows, 2 * nq + h * B_DV:2 * nq + (h + 1) * B_DV])
            bc.append(beta_all[rows, h:h + 1])
            gcc.append(gsub[:, B_HEADS + h:B_HEADS + h + 1])
            kbc.append(kc[-1] * bc[-1])
            egc.append(jnp.exp(gcc[-1]))
        q.append(qc)
        k.append(kc)
        v.append(vc)
        kb.append(kbc)
        beta.append(bc)
        gc.append(gcc)
        eg.append(egc)
        g_last.append([g[C - 1:C, :] for g in gcc])

    lows, attns = [], []
    for c in subs:
        for p in pairs:
            a, b = 2 * p, 2 * p + 1
            g_col = jnp.where(left, gc[c][a], gc[c][b])
            g_row = jnp.where(left[0:1], gcum_t[c][B_HEADS + a:B_HEADS + a + 1, :],
                              gcum_t[c][B_HEADS + b:B_HEADS + b + 1, :])
            decay = jnp.where(incl, jnp.exp(jnp.where(incl, g_col - g_row, 0.0)), 0.0)
            lhs = cat([cat([kb[c][a], kb[c][b]], axis=1), cat([q[c][a], q[c][b]], axis=1)],
                      axis=0)
            rhs = cat([cat([k[c][a], zk], axis=1), cat([zk, k[c][b]], axis=1)], axis=0)
            m1 = lax.dot_general(lhs.astype(BF16), rhs.astype(BF16), NT_DIMS,
                                 preferred_element_type=F32)
            lows.append(jnp.where(strict, m1[:C] * decay, 0.0).astype(BF16).astype(F32))
            attns.append(m1[C:] * decay)

    tinvs = _unit_lower_inverse(lows, inv_masks, eye, left)

    sols = []
    for c in subs:
        for p in pairs:
            a, b = 2 * p, 2 * p + 1
            rhs_a = cat([v[c][a] * beta[c][a], kb[c][a] * eg[c][a]], axis=1)
            rhs_b = cat([v[c][b] * beta[c][b], kb[c][b] * eg[c][b]], axis=1)
            rhs_bd = cat([cat([rhs_a, z2], axis=1), cat([z2, rhs_b], axis=1)], axis=0)
            sols.append(_dot(tinvs[c * len(pairs) + p].astype(BF16), rhs_bd.astype(BF16)))

    s_cur = [s_sc[h] for h in range(B_HEADS)]
    for c in subs:
        rows = slice(c * C, (c + 1) * C)
        v_new, q_s = [], []
        for h in range(B_HEADS):
            sol = sols[c * len(pairs) + h // 2]
            base = (h % 2) * (B_DV + B_DK)
            u = sol[:, base:base + B_DV]
            w = sol[:, base + B_DV:base + B_DV + B_DK]
            m2 = _dot(cat([w, q[c][h] * eg[c][h]], axis=0).astype(BF16), s_cur[h].astype(BF16))
            v_new.append(u - m2[:C])
            q_s.append(m2[C:])
        o_pairs, upds = [], []
        for p in pairs:
            a, b = 2 * p, 2 * p + 1
            vn_bd = cat([cat([v_new[a], zv], axis=1), cat([zv, v_new[b]], axis=1)],
                        axis=0).astype(BF16)
            o_pairs.append(_dot(attns[c * len(pairs) + p].astype(BF16), vn_bd))
            kd_t = cat([k[c][a] * jnp.exp(g_last[c][a] - gc[c][a]),
                        k[c][b] * jnp.exp(g_last[c][b] - gc[c][b])], axis=0).T
            upds.append(_dot(kd_t.astype(BF16), vn_bd))
        for h in range(B_HEADS):
            lo = (h % 2) * B_DV
            s_cur[h] = s_cur[h] * jnp.exp(g_last[c][h]) + upds[h // 2][:, lo:lo + B_DV]
            o = q_s[h] + o_pairs[h // 2][:, lo:lo + B_DV]
            zh = z_ref[rows, h * B_DV:(h + 1) * B_DV]
            on = o * lax.rsqrt(jnp.mean(o * o, axis=-1, keepdims=True) + EPS) * ng_ref[...]
            o_ref[rows, h * B_DV:(h + 1) * B_DV] = (
                on * (zh * jax.nn.sigmoid(zh))).astype(o_ref.dtype)
    for h in range(B_HEADS):
        s_sc[h] = s_cur[h]

    @pl.when(c_idx == nc - 1)
    def _():
        s_out_ref[...] = s_sc[...].astype(s_out_ref.dtype)


def _gdn(qkv, ba, z, s0, buf, conv_w, par, norm_g, n_seq, rows_per_seq, rows_per_step):
    assert rows_per_seq % rows_per_step == 0 and rows_per_step % CHUNK == 0
    nc = rows_per_seq // rows_per_step
    nqkv = qkv.shape[1]
    row = lambda b, c: (b * nc + c, 0)
    return pl.pallas_call(
        _gdn_kernel,
        grid=(n_seq, nc),
        in_specs=[pl.BlockSpec((rows_per_step, nqkv), row),
                  pl.BlockSpec((None, GDN_CONV - 1, nqkv), lambda b, c: (b, 0, 0)),
                  pl.BlockSpec((rows_per_step, LANES), row),
                  pl.BlockSpec((rows_per_step, B_HEADS * B_DV), row),
                  pl.BlockSpec((None, B_HEADS, B_DK, B_DV), lambda b, c: (b, 0, 0, 0)),
                  pl.BlockSpec((GDN_CONV, nqkv), lambda b, c: (0, 0)),
                  pl.BlockSpec((8, LANES), lambda b, c: (0, 0)),
                  pl.BlockSpec((1, B_DV), lambda b, c: (0, 0))],
        out_specs=[pl.BlockSpec((rows_per_step, B_HEADS * B_DV), row),
                   pl.BlockSpec((None, B_HEADS, B_DK, B_DV), lambda b, c: (b, 0, 0, 0)),
                   pl.BlockSpec((None, GDN_CONV - 1, nqkv), lambda b, c: (b, 0, 0))],
        out_shape=[jax.ShapeDtypeStruct((n_seq * rows_per_seq, B_HEADS * B_DV), BF16),
                   jax.ShapeDtypeStruct((n_seq, B_HEADS, B_DK, B_DV), F32),
                   jax.ShapeDtypeStruct((n_seq, GDN_CONV - 1, nqkv), F32)],
        scratch_shapes=[pltpu.VMEM((HIST + rows_per_step, nqkv), F32),
                        pltpu.VMEM((B_HEADS, B_DK, B_DV), F32)],
        compiler_params=_cparams(2),
        name="gdn",
    )(qkv, buf, ba, z, s0, conv_w, par, norm_g)


def _merge_kernel(x_ref, oa_ref, ob_ref, ga_ref, gb_ref, wa_ref, wb_ref, wo_ref, h_ref):
    a = jnp.dot(oa_ref[...], wa_ref[...], preferred_element_type=F32)
    b = jnp.dot(ob_ref[...], wb_ref[...], preferred_element_type=F32)
    merged = jax.nn.sigmoid(ga_ref[...]) * a + jax.nn.sigmoid(gb_ref[...]) * b
    h_ref[...] = x_ref[...] + jnp.dot(merged.astype(BF16), wo_ref[...],
                                      preferred_element_type=F32)


def _merge(x, oa, ob, ga, gb, wa, wb, wo, tm):
    rows, d = x.shape
    rowspec = pl.BlockSpec((tm, d), lambda i: (i, 0))
    wspec = pl.BlockSpec((d, d), lambda i: (0, 0))
    return pl.pallas_call(
        _merge_kernel,
        grid=(rows // tm,),
        in_specs=[rowspec] * 5 + [wspec] * 3,
        out_specs=rowspec,
        out_shape=jax.ShapeDtypeStruct((rows, d), F32),
        compiler_params=_cparams(1),
        name="merge",
    )(x, oa, ob, ga, gb, wa, wb, wo)


FFN_ROW_TILE = 512


def _ffn_kernel(h_ref, g_ref, buf_g_ref, buf_v_ref, wg_ref, wv_ref, cwg_ref, cwv_ref,
                cbg_ref, cbv_ref, wd_ref, fg_ref, y_ref, nbuf_g_ref, nbuf_v_ref,
                hn_sc, ug_sc, uv_sc, acc_sc, *, tiles_per_seq, final):
    i = pl.program_id(0)
    j = pl.program_id(1)
    nj = pl.num_programs(1)
    tm = h_ref.shape[0]
    first = (i % tiles_per_seq) == 0
    nh = FFN_CONV - 1

    @pl.when(j == 0)
    def _():
        x = h_ref[...]
        ms = jnp.mean(x * x, axis=-1, keepdims=True)
        hn_sc[...] = (x * lax.rsqrt(ms + EPS) * g_ref[...]).astype(BF16)
        acc_sc[...] = jnp.zeros(acc_sc.shape, F32)

    def conv_half(u_sc, w_ref, cw_ref, cb_ref, buf_ref, nbuf_ref):
        @pl.when(first)
        def _():
            u_sc[j, HIST - nh:HIST, :] = buf_ref[...]
        u_sc[j, HIST:HIST + tm, :] = jnp.dot(hn_sc[...], w_ref[...], preferred_element_type=F32)
        out = u_sc[j, HIST:HIST + tm, :] * cw_ref[nh:nh + 1, :] + cb_ref[...]
        for t in range(nh):
            off = HIST - nh + t
            out = out + u_sc[j, off:off + tm, :] * cw_ref[t:t + 1, :]
        tail = u_sc[j, tm:tm + HIST, :]
        u_sc[j, 0:HIST, :] = tail
        nbuf_ref[j] = tail[HIST - nh:HIST, :]
        return out

    gate = conv_half(ug_sc, wg_ref, cwg_ref, cbg_ref, buf_g_ref, nbuf_g_ref)
    val = conv_half(uv_sc, wv_ref, cwv_ref, cbv_ref, buf_v_ref, nbuf_v_ref)
    act = (gate * jax.nn.sigmoid(gate) * val).astype(BF16)
    acc_sc[...] += jnp.dot(act, wd_ref[...], preferred_element_type=F32)

    @pl.when(j == nj - 1)
    def _():
        y = h_ref[...] + acc_sc[...]
        if final:
            ms = jnp.mean(y * y, axis=-1, keepdims=True)
            y = y * lax.rsqrt(ms + EPS) * fg_ref[...]
        y_ref[...] = y


def _ffn(h, g, buf, w_up, conv_w, conv_b, w_down, final_g, n_seq, rows_per_seq, tm, cw, final):
    rows, d = h.shape
    d_ff = w_down.shape[0]
    assert d_ff % cw == 0 and rows_per_seq % tm == 0
    nj = d_ff // cw
    tps = rows_per_seq // tm
    nh = FFN_CONV - 1
    gate_col = lambda i, j: (0, j)
    val_col = lambda i, j: (0, nj + j)
    seq_gate = lambda i, j: (i // tps, 0, j)
    seq_val = lambda i, j: (i // tps, 0, nj + j)
    y, nbg, nbv = pl.pallas_call(
        functools.partial(_ffn_kernel, tiles_per_seq=tps, final=final),
        grid=(rows // tm, nj),
        in_specs=[pl.BlockSpec((tm, d), lambda i, j: (i, 0)),
                  pl.BlockSpec((1, d), lambda i, j: (0, 0)),
                  pl.BlockSpec((None, nh, cw), seq_gate),
                  pl.BlockSpec((None, nh, cw), seq_val),
                  pl.BlockSpec((d, cw), gate_col),
                  pl.BlockSpec((d, cw), val_col),
                  pl.BlockSpec((FFN_CONV, cw), gate_col),
                  pl.BlockSpec((FFN_CONV, cw), val_col),
                  pl.BlockSpec((1, cw), gate_col),
                  pl.BlockSpec((1, cw), val_col),
                  pl.BlockSpec((cw, d), lambda i, j: (j, 0)),
                  pl.BlockSpec((1, d), lambda i, j: (0, 0))],
        out_specs=[pl.BlockSpec((tm, d), lambda i, j: (i, 0)),
                   pl.BlockSpec((None, nj, nh, cw), lambda i, j: (i // tps, 0, 0, 0)),
                   pl.BlockSpec((None, nj, nh, cw), lambda i, j: (i // tps, 0, 0, 0))],
        out_shape=[jax.ShapeDtypeStruct((rows, d), F32),
                   jax.ShapeDtypeStruct((n_seq, nj, nh, cw), F32),
                   jax.ShapeDtypeStruct((n_seq, nj, nh, cw), F32)],
        scratch_shapes=[pltpu.VMEM((tm, d), BF16),
                        pltpu.VMEM((nj, HIST + tm, cw), F32),
                        pltpu.VMEM((nj, HIST + tm, cw), F32),
                        pltpu.VMEM((tm, d), F32)],
        compiler_params=_cparams(2),
        name="ffn",
    )(h, g, buf, buf, w_up, w_up, conv_w, conv_w, conv_b, conv_b, w_down, final_g)
    unchunk = lambda a: a.transpose(0, 2, 1, 3).reshape(n_seq, nh, d_ff)
    return y, jnp.concatenate([unchunk(nbg), unchunk(nbv)], axis=-1)


def _prep_layer_weights(l, P):
    w_in = P["w_in"][l]
    d = w_in.shape[0]
    aw = A_HEADS * A_VDIM
    nqkv = B_HEADS * (2 * B_DK + B_DV)
    c_qkvb = 3 * aw
    c_beta = c_qkvb + nqkv
    c_z = c_beta + 2 * B_HEADS
    bw = B_HEADS * B_DV
    w_q = w_in[:, :aw] * (A_HEAD_DIM ** -0.5 * LOG2E)
    w_a = jnp.concatenate([w_q, w_in[:, aw:3 * aw]], axis=1).astype(BF16)
    w_b = jnp.concatenate([w_in[:, c_qkvb:c_beta + 2 * B_HEADS],
                           jnp.zeros((d, LANES - 2 * B_HEADS), F32)], axis=1).astype(BF16)
    w_c = w_in[:, c_z:c_z + bw + 2 * d].astype(BF16)
    lam_rows = jnp.zeros((8, LANES), F32)
    for r, name in enumerate(("lambda_q1", "lambda_k1", "lambda_q2", "lambda_k2")):
        lam_rows = lam_rows.at[r, :A_HEAD_DIM].set(P[name][l])
    par = jnp.zeros((8, LANES), F32)
    par = par.at[0, B_HEADS:2 * B_HEADS].set(P["gdn_a_log"][l])
    par = par.at[1, B_HEADS:2 * B_HEADS].set(P["gdn_dt_bias"][l])
    return dict(
        w_a=w_a, w_b=w_b, w_c=w_c, lam_rows=lam_rows, par=par,
        norm_mix_g=P["norm_mix_g"][l][None, :],
        subln_g=P["subln_g"][l][None, :],
        gdn_conv_w=P["gdn_conv_w"][l],
        gdn_norm_g=P["gdn_norm_g"][l][None, :],
        w_proj_a=P["w_proj_a"][l].astype(BF16),
        w_proj_b=P["w_proj_b"][l].astype(BF16),
        w_out=P["w_out"][l].astype(BF16),
        norm_ffn_g=P["norm_ffn_g"][l][None, :],
        w_up=P["w_up"][l].astype(BF16),
        ffn_conv_w=P["ffn_conv_w"][l],
        ffn_conv_b=P["ffn_conv_b"][l][None, :],
        w_down=P["w_down"][l].astype(BF16),
        final_g=P["final_norm_g"][None, :],
    )


def _layer(x, l, depth, kv_prev, W, n_seq, rows_per_seq, cache_k, cache_v, s0, gbuf, fbuf,
           final, tm, tq, tk):
    rows, d = x.shape
    tm = min(tm, rows)
    aw = A_HEADS * A_VDIM
    nqkv = B_HEADS * (2 * B_DK + B_DV)
    lam0 = _lambda_init(l)
    slopes = (2.0 ** (-8.0 * jnp.arange(1, A_HEADS + 1, dtype=F32) / A_HEADS)) * LOG2E
    prompt = cache_k is None

    v_kinds = ("h", "t") if prompt else ("h", BF16)
    q, k, kb, v, vx = _norm_proj(
        x, W["norm_mix_g"], W["w_a"],
        ((0, aw, (BF16,)), (aw, aw, ("h", BF16)), (2 * aw, aw, v_kinds)), tm if not prompt else tk,
        layer=l, depth=depth, carried=kv_prev)
    qkvb, ba = _norm_proj(x, W["norm_mix_g"], W["w_b"],
                          ((0, nqkv, (F32,)), (nqkv, LANES, (F32,))), tm)
    z, ga, gb = _norm_proj(x, W["norm_mix_g"], W["w_c"],
                           ((0, aw, (F32,)), (aw, d, (F32,)), (aw + d, d, (F32,))), tm)

    if prompt:
        o_a = _attn_prompt(q, kb, vx, W["lam_rows"], W["subln_g"], slopes, lam0, tq, tk)
    else:
        o_a = _attn_sample(q, kb, vx, cache_k, cache_v, l, W["lam_rows"], W["subln_g"],
                           slopes, lam0, rows_per_seq)

    o_b, s_new, gbuf_new = _gdn(qkvb, ba, z, s0, gbuf, W["gdn_conv_w"], W["par"],
                                W["gdn_norm_g"], n_seq, rows_per_seq,
                                min(GDN_ROWS_PER_STEP, rows_per_seq))

    h = _merge(x, o_a, o_b, ga, gb, W["w_proj_a"], W["w_proj_b"], W["w_out"], tm)
    d_ff = W["w_down"].shape[0]
    y, fbuf_new = _ffn(h, W["norm_ffn_g"], fbuf, W["w_up"], W["ffn_conv_w"], W["ffn_conv_b"],
                       W["w_down"], W["final_g"], n_seq, rows_per_seq,
                       min(FFN_ROW_TILE, rows_per_seq), d_ff // 2, final)
    return y, k, v, s_new, gbuf_new, fbuf_new


def kernel(x_prompt, x_sample, cache_k, cache_v, state_gdn, state_gdn_conv, state_ffn_conv, norm_mix_g, w_in, lambda_q1, lambda_k1, lambda_q2, lambda_k2, subln_g, gdn_conv_w, gdn_a_log, gdn_dt_bias, gdn_norm_g, w_proj_a, w_proj_b, w_out, norm_ffn_g, w_up, ffn_conv_w, ffn_conv_b, w_down, final_norm_g):
    P = dict(norm_mix_g=norm_mix_g, w_in=w_in, lambda_q1=lambda_q1, lambda_k1=lambda_k1,
             lambda_q2=lambda_q2, lambda_k2=lambda_k2, subln_g=subln_g, gdn_conv_w=gdn_conv_w,
             gdn_a_log=gdn_a_log, gdn_dt_bias=gdn_dt_bias, gdn_norm_g=gdn_norm_g,
             w_proj_a=w_proj_a, w_proj_b=w_proj_b, w_out=w_out, norm_ffn_g=norm_ffn_g,
             w_up=w_up, ffn_conv_w=ffn_conv_w, ffn_conv_b=ffn_conv_b, w_down=w_down,
             final_norm_g=final_norm_g)
    depth = w_in.shape[0]
    bp, tp, d = x_prompt.shape
    bs, ts, _ = x_sample.shape
    past = cache_k.shape[2]
    assert ts == CHUNK and past % CHUNK == 0 and tp % CHUNK == 0
    nqkv = state_gdn_conv.shape[-1]
    d_ff2 = state_ffn_conv.shape[-1]
    weights = [_prep_layer_weights(l, P) for l in range(depth)]
    ck = cache_k.reshape(depth, bs, past * A_HEADS, A_VDIM)
    cv = cache_v.reshape(depth, bs, past * A_HEADS, A_VDIM)

    tk = min(ATTN_KEY_TILE, tp)
    tq = min(ATTN_QUERY_BLOCK, tp)
    tm = 512

    def run(x3, nseq, rps, sample):
        assert sample or nseq == 1
        x = x3.reshape(nseq * rps, d)
        kv = tuple(jnp.zeros((depth, nseq * rps, A_HEADS, A_VDIM), F32) for _ in range(2))
        ss, gcs, fcs = [], [], []
        for l in range(depth):
            if sample:
                s0, gbuf, fbuf = state_gdn[l], state_gdn_conv[l], state_ffn_conv[l]
                c_k, c_v = ck, cv
            else:
                s0 = jnp.zeros((nseq, B_HEADS, B_DK, B_DV), F32)
                gbuf = jnp.zeros((nseq, GDN_CONV - 1, nqkv), F32)
                fbuf = jnp.zeros((nseq, FFN_CONV - 1, d_ff2), F32)
                c_k = c_v = None
            x, k_all, v_all, s_new, g_new, f_new = _layer(
                x, l, depth, kv, weights[l], nseq, rps, c_k, c_v, s0, gbuf, fbuf,
                l == depth - 1, tm, tq, tk)
            kv = (k_all, v_all)
            ss.append(s_new)
            gcs.append(g_new)
            fcs.append(f_new)
        kv_shape = (depth, nseq, rps, A_HEADS, A_VDIM)
        return (x.reshape(nseq, rps, d), kv[0].reshape(kv_shape), kv[1].reshape(kv_shape),
                jnp.stack(ss), jnp.stack(gcs), jnp.stack(fcs))

    y_p, k_p, v_p, s_p, gc_p, fc_p = run(x_prompt, bp, tp, False)
    y_s, k_s, v_s, s_s, gc_s, fc_s = run(x_sample, bs, ts, True)
    return (y_p, y_s, k_p, v_p, s_p, gc_p, fc_p, k_s, v_s, s_s, gc_s, fc_s)
```

```python
import functools
import math

import jax
import jax.numpy as jnp
from jax import lax
from jax.experimental import pallas as pl
from jax.experimental.pallas import tpu as pltpu

F32 = jnp.float32
BF16 = jnp.bfloat16

EPS = 1e-6
NEG = -1e30
LOG2E = 1.4426950408889634

CHUNK = 64
A_HEADS = 8
A_HEAD_DIM = 64
A_VDIM = 128
B_HEADS = 8
B_DK = 128
B_DV = 128
GDN_CONV = 4
FFN_CONV = 3
LANES = 128
HIST = 8

VMEM_LIMIT = 56 * 1024 * 1024


def _cparams(n_axes):
    return pltpu.CompilerParams(
        dimension_semantics=("arbitrary",) * n_axes,
        vmem_limit_bytes=VMEM_LIMIT)


def _lambda_init(layer):
    return 0.8 - 0.6 * math.exp(-0.3 * layer)


def _norm_proj_kernel(x_ref, g_ref, w_ref, *refs, segs, n_carried):
    out_refs = refs[n_carried:]
    x = x_ref[...]
    ms = jnp.mean(x * x, axis=-1, keepdims=True)
    xn = (x * lax.rsqrt(ms + EPS) * g_ref[...]).astype(BF16)
    k = 0
    for start, width, kinds in segs:
        w = w_ref[:, start:start + width]
        r = None
        for kind in kinds:
            o_ref = out_refs[k]
            k += 1
            if r is None:
                r = jnp.dot(xn, w, preferred_element_type=F32)
            if kind == "t":
                o_ref[...] = r.T.astype(o_ref.dtype)
            elif kind == "h":
                for hd in range(width // LANES):
                    o_ref[:, hd, :] = r[:, hd * LANES:(hd + 1) * LANES]
            else:
                o_ref[...] = r.astype(o_ref.dtype)


def _norm_proj(x, g, w, segs, tm, layer=0, depth=1, carried=()):
    rows, d = x.shape
    assert rows % tm == 0
    out_shape, out_specs, ksegs, h_outs = [], [], [], []
    for start, width, kinds in segs:
        kk = []
        for kind in kinds:
            if kind == "t":
                out_shape.append(jax.ShapeDtypeStruct((rows // tm, width, tm), BF16))
                out_specs.append(pl.BlockSpec((None, width, tm), lambda i: (i, 0, 0)))
                kk.append("t")
            elif kind == "h":
                h_outs.append(len(out_shape))
                out_shape.append(jax.ShapeDtypeStruct(
                    (depth, rows, width // LANES, LANES), F32))
                out_specs.append(pl.BlockSpec((None, tm, width // LANES, LANES),
                                              lambda i: (layer, i, 0, 0)))
                kk.append("h")
            else:
                out_shape.append(jax.ShapeDtypeStruct((rows, width), kind))
                out_specs.append(pl.BlockSpec((tm, width), lambda i: (i, 0)))
                kk.append("n")
        ksegs.append((start, width, tuple(kk)))
    assert len(carried) in (0, len(h_outs))
    return pl.pallas_call(
        functools.partial(_norm_proj_kernel, segs=tuple(ksegs), n_carried=len(carried)),
        grid=(rows // tm,),
        in_specs=[pl.BlockSpec((tm, d), lambda i: (i, 0)),
                  pl.BlockSpec((1, d), lambda i: (0, 0)),
                  pl.BlockSpec(w.shape, lambda i: (0, 0))]
                 + [pl.BlockSpec(memory_space=pl.ANY)] * len(carried),
        out_specs=out_specs,
        out_shape=out_shape,
        input_output_aliases={3 + n: h_outs[n] for n in range(len(carried))},
        compiler_params=_cparams(1),
        name="norm_proj",
    )(x, g, w, *carried)


def _lambda_value(lam_ref, lam0):
    a = jnp.sum(lam_ref[0:1, :] * lam_ref[1:2, :], axis=-1, keepdims=True)
    b = jnp.sum(lam_ref[2:3, :] * lam_ref[3:4, :], axis=-1, keepdims=True)
    return jnp.exp(a) - jnp.exp(b) + lam0


def _stack_maps(q):
    lane = lax.broadcasted_iota(jnp.int32, q.shape, 1)
    zero = jnp.zeros_like(q)
    return jnp.concatenate([jnp.where(lane < A_HEAD_DIM, q, zero),
                            jnp.where(lane >= A_HEAD_DIM, q, zero)], axis=0)


NT_DIMS = (((1,), (1,)), ((), ()))
ATTN_KEY_TILE = 512
ATTN_QUERY_BLOCK = 1024


def _attn_prompt_kernel(slopes_ref, lam_ref, g_ref, q_ref, k_ref, vt_ref, o_ref,
                        qq_sc, kaug_sc, dfix_sc, sa_sc, sb_sc, m_sc, l_sc, acc_sc,
                        *, tq, tk, lam0):
    h = pl.program_id(0)
    i = pl.program_id(1)
    slope = slopes_ref[h]
    q0 = i * tq
    nt = NT_DIMS
    w = tk
    nd = tq // tk
    n_off = i * nd
    n_strips = 2 * tq // w

    q_t = q_ref[...].astype(F32).T
    feat = lax.broadcasted_iota(jnp.int32, q_t.shape, 0)
    qq_sc[:A_VDIM, :] = jnp.concatenate(
        [jnp.where(feat < A_HEAD_DIM, q_t, 0.0), jnp.where(feat >= A_HEAD_DIM, q_t, 0.0)],
        axis=1).astype(BF16)
    row_q = lax.broadcasted_iota(jnp.int32, (LANES, 2 * tq), 0)
    qq_sc[A_VDIM:, :] = jnp.where(row_q < 3, 1.0, 0.0).astype(BF16)

    @pl.when(i == 0)
    def _():
        b = slope * lax.broadcasted_iota(jnp.int32, (tk, LANES), 0).astype(F32)
        b1 = b.astype(BF16).astype(F32)
        b2 = (b - b1).astype(BF16).astype(F32)
        b3 = b - b1 - b2
        lane_k = lax.broadcasted_iota(jnp.int32, (tk, LANES), 1)
        kaug_sc[...] = jnp.where(lane_k == 0, b1, jnp.where(
            lane_k == 1, b2, jnp.where(lane_k == 2, b3, 0.0))).astype(BF16)
        c = lax.broadcasted_iota(jnp.int32, (tk, w), 0)
        r = lax.broadcasted_iota(jnp.int32, (tk, w), 1)
        fix = slope * jnp.minimum(2 * (r - c), 0).astype(F32)
        dfix_sc[...] = jnp.where((c // CHUNK) <= (r // CHUNK), fix, NEG)

    m_sc[...] = jnp.full(m_sc.shape, NEG, F32)
    l_sc[...] = jnp.zeros(l_sc.shape, F32)
    acc_sc[...] = jnp.zeros(acc_sc.shape, F32)

    groups = 8

    def key_reduce(op, x):
        part = op(x.reshape(groups, tk // groups, x.shape[-1]), axis=0)
        return op(part, axis=0, keepdims=True)

    def softmax_update(st, s_ref, kappa, vtt):
        sl = slice(st * w, (st + 1) * w)
        m_prev = m_sc[:, sl]
        m_new = jnp.maximum(m_prev, key_reduce(jnp.max, s_ref[:, sl]) + kappa)
        alpha = jnp.exp2(m_prev - m_new)
        p = jnp.exp2(s_ref[:, sl] - (m_new - kappa))
        pv = jnp.dot(jnp.concatenate([vtt, jnp.ones((16, tk), BF16)], axis=0), p.astype(BF16),
                     preferred_element_type=F32)
        l_sc[:, sl] = alpha * l_sc[:, sl] + pv[A_VDIM:A_VDIM + 1]
        acc_sc[:, sl] = alpha * acc_sc[:, sl] + pv[:A_VDIM]
        m_sc[:, sl] = m_new

    def produce(j, s_ref, strips=None):
        k0 = pl.multiple_of(j * tk, tk)
        lhs = jnp.concatenate([k_ref[pl.ds(k0, tk), :], kaug_sc[...]], axis=1)
        if strips is None:
            s_ref[...] = jnp.dot(lhs, qq_sc[...], preferred_element_type=F32)
        else:
            for st in strips:
                sl = slice(st * w, (st + 1) * w)
                s_ref[:, sl] = jnp.dot(lhs, qq_sc[:, sl], preferred_element_type=F32)

    def visible_strips(d):
        return [st for st in range(n_strips) if (st * w) % tq >= d * tk]

    def consume(j, s_ref):
        kappa = slope * (j * tk - q0).astype(F32)
        vtj = vt_ref[j]
        for st in range(n_strips):
            softmax_update(st, s_ref, kappa, vtj)

    def consume_overlap(d, s_ref):
        kappa = slope * (d * tk)
        vtd = vt_ref[n_off + d]
        for st in visible_strips(d):
            if (st * w) % tq == d * tk:
                sl = slice(st * w, (st + 1) * w)
                s_ref[:, sl] = s_ref[:, sl] + dfix_sc[...]
            softmax_update(st, s_ref, kappa, vtd)

    def overlap_tiles(first_ref, second_ref):
        for d in range(nd):
            cur, nxt = (first_ref, second_ref) if d % 2 == 0 else (second_ref, first_ref)
            if d + 1 < nd:
                produce(n_off + d + 1, nxt, visible_strips(d + 1))
            consume_overlap(d, cur)

    produce(0, sa_sc)

    def two_tiles(j):
        produce(j + 1, sb_sc)
        consume(j, sa_sc)
        produce(j + 2, sa_sc)
        consume(j + 1, sb_sc)

    def four_tiles(t, carry):
        two_tiles(4 * t)
        two_tiles(4 * t + 2)
        return carry

    lax.fori_loop(0, n_off // 4, four_tiles, 0)

    @pl.when(n_off % 4 >= 2)
    def _():
        two_tiles((n_off // 4) * 4)

    if nd % 2 == 0:
        overlap_tiles(sa_sc, sb_sc)
    else:
        @pl.when(n_off % 2 == 0)
        def _():
            overlap_tiles(sa_sc, sb_sc)

        @pl.when(n_off % 2 == 1)
        def _():
            produce(n_off, sb_sc)
            consume(n_off - 1, sa_sc)
            overlap_tiles(sb_sc, sa_sc)

    lam = _lambda_value(lam_ref, lam0)
    inv_l = 1.0 / l_sc[...]
    acc = acc_sc[...] * inv_l
    o = acc[:, :tq] - lam * acc[:, tq:]
    ms = jnp.mean(o * o, axis=0, keepdims=True)
    o = (o * lax.rsqrt(ms + EPS)).T
    o_ref[...] = (o * g_ref[...] * (1.0 - lam0)).astype(o_ref.dtype)


def _attn_prompt(q, kb, vt, lam_rows, subln_g, slopes, lam0, tq, tk):
    t = q.shape[0]
    assert t % tq == 0 and tq % tk == 0
    return pl.pallas_call(
        functools.partial(_attn_prompt_kernel, tq=tq, tk=tk, lam0=lam0),
        grid=(A_HEADS, t // tq),
        in_specs=[pl.BlockSpec(memory_space=pltpu.SMEM),
                  pl.BlockSpec((8, LANES), lambda h, i: (0, 0)),
                  pl.BlockSpec((1, A_VDIM), lambda h, i: (0, 0)),
                  pl.BlockSpec((tq, A_VDIM), lambda h, i: (i, h)),
                  pl.BlockSpec((t, A_VDIM), lambda h, i: (0, h)),
                  pl.BlockSpec((t // tk, A_VDIM, tk), lambda h, i: (0, h, 0))],
        out_specs=pl.BlockSpec((tq, A_VDIM), lambda h, i: (i, h)),
        out_shape=jax.ShapeDtypeStruct((t, A_HEADS * A_VDIM), BF16),
        scratch_shapes=[pltpu.VMEM((2 * LANES, 2 * tq), BF16),
                        pltpu.VMEM((tk, LANES), BF16),
                        pltpu.VMEM((tk, tk), F32),
                        pltpu.VMEM((tk, 2 * tq), F32),
                        pltpu.VMEM((tk, 2 * tq), F32),
                        pltpu.VMEM((1, 2 * tq), F32),
                        pltpu.VMEM((1, 2 * tq), F32),
                        pltpu.VMEM((A_VDIM, 2 * tq), F32)],
        compiler_params=_cparams(2),
        name="attn_prompt",
    )(slopes, lam_rows, subln_g, q, kb, vt)


SAMPLE_KEY_TILE = 1024


def _attn_sample_kernel(slopes_ref, lam_ref, g_ref, q_ref, kn_ref, vn_ref, kc_ref, vc_ref,
                        o_ref, m_sc, l_sc, acc_sc, *, past, lam0):
    t = pl.program_id(1)
    nt = pl.num_programs(1)
    tq = q_ref.shape[0]
    tk = kc_ref.shape[0] // A_HEADS

    @pl.when(t == 0)
    def _():
        m_sc[...] = jnp.full(m_sc.shape, NEG, F32)
        l_sc[...] = jnp.zeros(l_sc.shape, F32)
        acc_sc[...] = jnp.zeros(acc_sc.shape, F32)

    def update(h, s, v):
        m_prev = m_sc[h]
        m_new = jnp.maximum(m_prev, jnp.max(s, axis=-1, keepdims=True))
        alpha = jnp.exp2(m_prev - m_new)
        p = jnp.exp2(s - m_new)
        l_sc[h] = alpha * l_sc[h] + jnp.sum(p, axis=-1, keepdims=True)
        acc_sc[h] = alpha * acc_sc[h] + jnp.dot(p.astype(BF16), v, preferred_element_type=F32)
        m_sc[h] = m_new

    kpos = t * tk + lax.broadcasted_iota(jnp.int32, (1, tk), 1)
    rel = (kpos - past).astype(F32)
    heads = range(A_HEADS)
    qqs = [_stack_maps(q_ref[:, h * A_VDIM:(h + 1) * A_VDIM]) for h in heads]
    scores = [lax.dot_general(qqs[h], kc_ref[pl.ds(h, tk, stride=A_HEADS), :].astype(BF16),
                              NT_DIMS, preferred_element_type=F32) for h in heads]
    m_prev = [m_sc[h] for h in heads]
    m_new, alpha, p = [], [], []
    for h in heads:
        s = scores[h] + slopes_ref[h] * rel
        m_new.append(jnp.maximum(m_prev[h], jnp.max(s, axis=-1, keepdims=True)))
        alpha.append(jnp.exp2(m_prev[h] - m_new[h]))
        p.append(jnp.exp2(s - m_new[h]))
    pv = [jnp.dot(p[h].astype(BF16), vc_ref[pl.ds(h, tk, stride=A_HEADS), :].astype(BF16),
                  preferred_element_type=F32) for h in heads]
    for h in heads:
        l_sc[h] = alpha[h] * l_sc[h] + jnp.sum(p[h], axis=-1, keepdims=True)
        acc_sc[h] = alpha[h] * acc_sc[h] + pv[h]
        m_sc[h] = m_new[h]

    @pl.when(t == nt - 1)
    def _():
        lam = _lambda_value(lam_ref, lam0)
        r = lax.broadcasted_iota(jnp.int32, (2 * tq, tq), 0)
        c = lax.broadcasted_iota(jnp.int32, (2 * tq, tq), 1)
        r = jnp.where(r >= tq, r - tq, r)
        own = jnp.minimum(c, 2 * r - c).astype(F32)
        for h in range(A_HEADS):
            cols = slice(h * A_VDIM, (h + 1) * A_VDIM)
            s = lax.dot_general(qqs[h], kn_ref[:, cols], NT_DIMS, preferred_element_type=F32)
            update(h, s + slopes_ref[h] * own, vn_ref[:, cols])
            acc = acc_sc[h] * (1.0 / l_sc[h])
            o = acc[:tq] - lam * acc[tq:]
            ms = jnp.mean(o * o, axis=-1, keepdims=True)
            o_ref[:, cols] = (o * lax.rsqrt(ms + EPS) * g_ref[...]
                              * (1.0 - lam0)).astype(o_ref.dtype)


def _attn_sample(q, kb, vb, cache_k, cache_v, layer, lam_rows, subln_g, slopes, lam0, tq):
    nb, past = cache_k.shape[1], cache_k.shape[2] // A_HEADS
    tk = min(SAMPLE_KEY_TILE, past)
    assert past % tk == 0
    aw = A_HEADS * A_VDIM
    rows = pl.BlockSpec((tq, aw), lambda b, t: (b, 0))
    cache = pl.BlockSpec((None, None, tk * A_HEADS, A_VDIM), lambda b, t: (layer, b, t, 0))
    return pl.pallas_call(
        functools.partial(_attn_sample_kernel, past=past, lam0=lam0),
        grid=(nb, past // tk),
        in_specs=[pl.BlockSpec(memory_space=pltpu.SMEM),
                  pl.BlockSpec((8, LANES), lambda b, t: (0, 0)),
                  pl.BlockSpec((1, A_VDIM), lambda b, t: (0, 0)),
                  rows, rows, rows, cache, cache],
        out_specs=rows,
        out_shape=jax.ShapeDtypeStruct((nb * tq, aw), BF16),
        scratch_shapes=[pltpu.VMEM((A_HEADS, 2 * tq, 1), F32),
                        pltpu.VMEM((A_HEADS, 2 * tq, 1), F32),
                        pltpu.VMEM((A_HEADS, 2 * tq, A_VDIM), F32)],
        compiler_params=_cparams(2),
        name="attn_sample",
    )(slopes, lam_rows, subln_g, q, kb, vb, cache_k, cache_v)


_HI = lax.Precision.HIGHEST
GDN_ROWS_PER_STEP = 8 * CHUNK


def _dot(a, b):
    return jnp.dot(a, b, preferred_element_type=F32)


def _split(a):
    hi = a.astype(BF16)
    return hi, (a - hi.astype(F32)).astype(BF16)


def _block_diag2(a, left):
    zero = jnp.zeros_like(a)
    return jnp.concatenate([jnp.where(left, a, zero), jnp.where(left, zero, a)], axis=0)


def _pair_prod(a_parts, b_parts, left):
    ah, bh = a_parts[0], b_parts[0]
    bdh = _block_diag2(bh, left)
    if len(a_parts) == 1 and len(b_parts) == 1:
        return _dot(ah, bdh)
    if len(a_parts) == 1:
        return _dot(jnp.concatenate([ah, ah], axis=1),
                    jnp.concatenate([bdh, _block_diag2(b_parts[1], left)], axis=0))
    r = _dot(jnp.concatenate([ah, a_parts[1]], axis=1), jnp.concatenate([bdh, bdh], axis=0))
    if len(b_parts) == 1:
        return r
    return r + _dot(ah, _block_diag2(b_parts[1], left))


def _inverse_masks(ri, cj):
    n = ri.shape[0]
    masks = [(ri // 2) == (cj // 2)]
    b = 2
    while b < n:
        same_big = ((ri // (2 * b)) == (cj // (2 * b))).astype(jnp.int32)
        same_small = ((ri // b) == (cj // b)).astype(jnp.int32)
        masks.append((same_big - same_small) > 0)
        b *= 2
    return masks


def _unit_lower_inverse(lows, masks, eye, left):
    xs = [eye - jnp.where(masks[0], low, 0.0) for low in lows]
    for mask in masks[1:]:
        xparts = [_split(x) for x in xs]
        ys = [_pair_prod((jnp.where(mask, low, 0.0).astype(BF16),), xp, left)
              for low, xp in zip(lows, xparts)]
        xs = [x - _pair_prod(xp, _split(y), left) for x, xp, y in zip(xs, xparts, ys)]
    return xs


def _gdn_kernel(x_ref, buf_ref, ba_ref, z_ref, s0_ref, cw_ref, par_ref, ng_ref,
                o_ref, s_out_ref, buf_out_ref, xp_sc, s_sc):
    c_idx = pl.program_id(1)
    nc = pl.num_programs(1)
    R = x_ref.shape[0]
    C = CHUNK
    n_sub = R // C
    nq = B_HEADS * B_DK
    cat = jnp.concatenate

    @pl.when(c_idx == 0)
    def _():
        xp_sc[HIST - (GDN_CONV - 1):HIST, :] = buf_ref[...]
        s_sc[...] = s0_ref[...].astype(F32)

    xp_sc[HIST:HIST + R, :] = x_ref[...]
    xp = xp_sc[...]
    conv = xp[HIST:] * cw_ref[GDN_CONV - 1:GDN_CONV, :]
    for j in range(GDN_CONV - 1):
        shifted = pltpu.roll(xp, GDN_CONV - 1 - j, axis=0)
        conv = conv + shifted[HIST:] * cw_ref[j:j + 1, :]
    conv = conv * jax.nn.sigmoid(conv)
    tail = xp_sc[R:R + HIST, :]
    xp_sc[0:HIST, :] = tail
    buf_out_ref[...] = tail[HIST - (GDN_CONV - 1):HIST, :]

    ba = ba_ref[...]
    lane = lax.broadcasted_iota(jnp.int32, ba.shape, 1)
    beta_all = jax.nn.sigmoid(ba)
    g_all = jnp.where((lane >= B_HEADS) & (lane < 2 * B_HEADS),
                      -jnp.exp(par_ref[0:1, :]) * jax.nn.softplus(ba + par_ref[1:2, :]), 0.0)
    rr = lax.broadcasted_iota(jnp.int32, (R, R), 0)
    cc = lax.broadcasted_iota(jnp.int32, (R, R), 1)
    same_chunk = ((rr // C) - (cc // C)) == 0
    tri = jnp.where(same_chunk, jnp.where(rr >= cc, 1.0, 0.0), 0.0).astype(F32)
    gcum_all = jnp.dot(tri, g_all, precision=_HI, preferred_element_type=F32)

    ri = lax.broadcasted_iota(jnp.int32, (C, 2 * C), 0)
    cl = lax.broadcasted_iota(jnp.int32, (C, 2 * C), 1)
    left = cl < C
    cj = jnp.where(left, cl, cl - C)
    incl = ri >= cj
    strict = ri > cj
    eye = jnp.where(ri == cj, 1.0, 0.0).astype(F32)
    inv_masks = _inverse_masks(ri, cj)
    zk = jnp.zeros((C, B_DK), F32)
    z2 = jnp.zeros((C, B_DV + B_DK), F32)
    zv = jnp.zeros((C, B_DV), F32)

    subs = range(n_sub)
    pairs = range(B_HEADS // 2)
    q, k, v, kb, beta, gc, eg, g_last, gcum_t = [], [], [], [], [], [], [], [], []
    for c in subs:
        rows = slice(c * C, (c + 1) * C)
        gsub = gcum_all[rows]
        gcum_t.append(cat([gsub, gsub], axis=0).T)
        qc, kc, vc, kbc, bc, gcc, egc = [], [], [], [], [], [], []
        for h in range(B_HEADS):
            qh = conv[rows, h * B_DK:(h + 1) * B_DK]
            kh = conv[rows, nq + h * B_DK:nq + (h + 1) * B_DK]
            qc.append(qh * lax.rsqrt(jnp.sum(qh * qh, axis=-1, keepdims=True) + EPS)
                      * (B_DK ** -0.5))
            kc.append(kh * lax.rsqrt(jnp.sum(kh * kh, axis=-1, keepdims=True) + EPS))
            vc.append(conv[rows, 2 * nq + h * B_DV:2 * nq + (h + 1) * B_DV])
            bc.append(beta_all[rows, h:h + 1])
            gcc.append(gsub[:, B_HEADS + h:B_HEADS + h + 1])
            kbc.append(kc[-1] * bc[-1])
            egc.append(jnp.exp(gcc[-1]))
        q.append(qc)
        k.append(kc)
        v.append(vc)
        kb.append(kbc)
        beta.append(bc)
        gc.append(gcc)
        eg.append(egc)
        g_last.append([g[C - 1:C, :] for g in gcc])

    lows, attns = [], []
    for c in subs:
        for p in pairs:
            a, b = 2 * p, 2 * p + 1
            g_col = jnp.where(left, gc[c][a], gc[c][b])
            g_row = jnp.where(left[0:1], gcum_t[c][B_HEADS + a:B_HEADS + a + 1, :],
                              gcum_t[c][B_HEADS + b:B_HEADS + b + 1, :])
            decay = jnp.where(incl, jnp.exp(jnp.where(incl, g_col - g_row, 0.0)), 0.0)
            lhs = cat([cat([kb[c][a], kb[c][b]], axis=1), cat([q[c][a], q[c][b]], axis=1)],
                      axis=0)
            rhs = cat([cat([k[c][a], zk], axis=1), cat([zk, k[c][b]], axis=1)], axis=0)
            m1 = lax.dot_general(lhs.astype(BF16), rhs.astype(BF16), NT_DIMS,
                                 preferred_element_type=F32)
            lows.append(jnp.where(strict, m1[:C] * decay, 0.0).astype(BF16).astype(F32))
            attns.append(m1[C:] * decay)

    tinvs = _unit_lower_inverse(lows, inv_masks, eye, left)

    sols = []
    for c in subs:
        for p in pairs:
            a, b = 2 * p, 2 * p + 1
            rhs_a = cat([v[c][a] * beta[c][a], kb[c][a] * eg[c][a]], axis=1)
            rhs_b = cat([v[c][b] * beta[c][b], kb[c][b] * eg[c][b]], axis=1)
            rhs_bd = cat([cat([rhs_a, z2], axis=1), cat([z2, rhs_b], axis=1)], axis=0)
            sols.append(_dot(tinvs[c * len(pairs) + p].astype(BF16), rhs_bd.astype(BF16)))

    s_cur = [s_sc[h] for h in range(B_HEADS)]
    for c in subs:
        rows = slice(c * C, (c + 1) * C)
        v_new, q_s = [], []
        for h in range(B_HEADS):
            sol = sols[c * len(pairs) + h // 2]
            base = (h % 2) * (B_DV + B_DK)
            u = sol[:, base:base + B_DV]
            w = sol[:, base + B_DV:base + B_DV + B_DK]
            m2 = _dot(cat([w, q[c][h] * eg[c][h]], axis=0).astype(BF16), s_cur[h].astype(BF16))
            v_new.append(u - m2[:C])
            q_s.append(m2[C:])
        o_pairs, upds = [], []
        for p in pairs:
            a, b = 2 * p, 2 * p + 1
            vn_bd = cat([cat([v_new[a], zv], axis=1), cat([zv, v_new[b]], axis=1)],
                        axis=0).astype(BF16)
            o_pairs.append(_dot(attns[c * len(pairs) + p].astype(BF16), vn_bd))
            kd_t = cat([k[c][a] * jnp.exp(g_last[c][a] - gc[c][a]),
                        k[c][b] * jnp.exp(g_last[c][b] - gc[c][b])], axis=0).T
            upds.append(_dot(kd_t.astype(BF16), vn_bd))
        for h in range(B_HEADS):
            lo = (h % 2) * B_DV
            s_cur[h] = s_cur[h] * jnp.exp(g_last[c][h]) + upds[h // 2][:, lo:lo + B_DV]
            o = q_s[h] + o_pairs[h // 2][:, lo:lo + B_DV]
            zh = z_ref[rows, h * B_DV:(h + 1) * B_DV]
            on = o * lax.rsqrt(jnp.mean(o * o, axis=-1, keepdims=True) + EPS) * ng_ref[...]
            o_ref[rows, h * B_DV:(h + 1) * B_DV] = (
                on * (zh * jax.nn.sigmoid(zh))).astype(o_ref.dtype)
    for h in range(B_HEADS):
        s_sc[h] = s_cur[h]

    @pl.when(c_idx == nc - 1)
    def _():
        s_out_ref[...] = s_sc[...].astype(s_out_ref.dtype)


def _gdn(qkv, ba, z, s0, buf, conv_w, par, norm_g, n_seq, rows_per_seq, rows_per_step):
    assert rows_per_seq % rows_per_step == 0 and rows_per_step % CHUNK == 0
    nc = rows_per_seq // rows_per_step
    nqkv = qkv.shape[1]
    row = lambda b, c: (b * nc + c, 0)
    return pl.pallas_call(
        _gdn_kernel,
        grid=(n_seq, nc),
        in_specs=[pl.BlockSpec((rows_per_step, nqkv), row),
                  pl.BlockSpec((None, GDN_CONV - 1, nqkv), lambda b, c: (b, 0, 0)),
                  pl.BlockSpec((rows_per_step, LANES), row),
                  pl.BlockSpec((rows_per_step, B_HEADS * B_DV), row),
                  pl.BlockSpec((None, B_HEADS, B_DK, B_DV), lambda b, c: (b, 0, 0, 0)),
                  pl.BlockSpec((GDN_CONV, nqkv), lambda b, c: (0, 0)),
                  pl.BlockSpec((8, LANES), lambda b, c: (0, 0)),
                  pl.BlockSpec((1, B_DV), lambda b, c: (0, 0))],
        out_specs=[pl.BlockSpec((rows_per_step, B_HEADS * B_DV), row),
                   pl.BlockSpec((None, B_HEADS, B_DK, B_DV), lambda b, c: (b, 0, 0, 0)),
                   pl.BlockSpec((None, GDN_CONV - 1, nqkv), lambda b, c: (b, 0, 0))],
        out_shape=[jax.ShapeDtypeStruct((n_seq * rows_per_seq, B_HEADS * B_DV), BF16),
                   jax.ShapeDtypeStruct((n_seq, B_HEADS, B_DK, B_DV), F32),
                   jax.ShapeDtypeStruct((n_seq, GDN_CONV - 1, nqkv), F32)],
        scratch_shapes=[pltpu.VMEM((HIST + rows_per_step, nqkv), F32),
                        pltpu.VMEM((B_HEADS, B_DK, B_DV), F32)],
        compiler_params=_cparams(2),
        name="gdn",
    )(qkv, buf, ba, z, s0, conv_w, par, norm_g)


def _merge_kernel(x_ref, oa_ref, ob_ref, ga_ref, gb_ref, wa_ref, wb_ref, wo_ref, h_ref):
    a = jnp.dot(oa_ref[...], wa_ref[...], preferred_element_type=F32)
    b = jnp.dot(ob_ref[...], wb_ref[...], preferred_element_type=F32)
    merged = jax.nn.sigmoid(ga_ref[...]) * a + jax.nn.sigmoid(gb_ref[...]) * b
    h_ref[...] = x_ref[...] + jnp.dot(merged.astype(BF16), wo_ref[...],
                                      preferred_element_type=F32)


def _merge(x, oa, ob, ga, gb, wa, wb, wo, tm):
    rows, d = x.shape
    rowspec = pl.BlockSpec((tm, d), lambda i: (i, 0))
    wspec = pl.BlockSpec((d, d), lambda i: (0, 0))
    return pl.pallas_call(
        _merge_kernel,
        grid=(rows // tm,),
        in_specs=[rowspec] * 5 + [wspec] * 3,
        out_specs=rowspec,
        out_shape=jax.ShapeDtypeStruct((rows, d), F32),
        compiler_params=_cparams(1),
        name="merge",
    )(x, oa, ob, ga, gb, wa, wb, wo)


FFN_ROW_TILE = 512
FFN_COLUMN_CHUNKS = 1


def _ffn_kernel(h_ref, g_ref, buf_g_ref, buf_v_ref, wg_ref, wv_ref, cwg_ref, cwv_ref,
                cbg_ref, cbv_ref, wd_ref, fg_ref, y_ref, nbuf_g_ref, nbuf_v_ref,
                hn_sc, ug_sc, uv_sc, acc_sc, *, tiles_per_seq, final):
    i = pl.program_id(0)
    j = pl.program_id(1)
    nj = pl.num_programs(1)
    tm = h_ref.shape[0]
    first = (i % tiles_per_seq) == 0
    nh = FFN_CONV - 1

    @pl.when(j == 0)
    def _():
        x = h_ref[...]
        ms = jnp.mean(x * x, axis=-1, keepdims=True)
        hn_sc[...] = (x * lax.rsqrt(ms + EPS) * g_ref[...]).astype(BF16)
        acc_sc[...] = jnp.zeros(acc_sc.shape, F32)

    def conv_half(u_sc, w_ref, cw_ref, cb_ref, buf_ref, nbuf_ref):
        @pl.when(first)
        def _():
            u_sc[j, HIST - nh:HIST, :] = buf_ref[...]
        u_sc[j, HIST:HIST + tm, :] = jnp.dot(hn_sc[...], w_ref[...], preferred_element_type=F32)
        out = u_sc[j, HIST:HIST + tm, :] * cw_ref[nh:nh + 1, :] + cb_ref[...]
        for t in range(nh):
            off = HIST - nh + t
            out = out + u_sc[j, off:off + tm, :] * cw_ref[t:t + 1, :]
        tail = u_sc[j, tm:tm + HIST, :]
        u_sc[j, 0:HIST, :] = tail
        nbuf_ref[j] = tail[HIST - nh:HIST, :]
        return out

    gate = conv_half(ug_sc, wg_ref, cwg_ref, cbg_ref, buf_g_ref, nbuf_g_ref)
    val = conv_half(uv_sc, wv_ref, cwv_ref, cbv_ref, buf_v_ref, nbuf_v_ref)
    act = (gate * jax.nn.sigmoid(gate) * val).astype(BF16)
    acc_sc[...] += jnp.dot(act, wd_ref[...], preferred_element_type=F32)

    @pl.when(j == nj - 1)
    def _():
        y = h_ref[...] + acc_sc[...]
        if final:
            ms = jnp.mean(y * y, axis=-1, keepdims=True)
            y = y * lax.rsqrt(ms + EPS) * fg_ref[...]
        y_ref[...] = y


def _ffn(h, g, buf, w_up, conv_w, conv_b, w_down, final_g, n_seq, rows_per_seq, tm, cw, final):
    rows, d = h.shape
    d_ff = w_down.shape[0]
    assert d_ff % cw == 0 and rows_per_seq % tm == 0
    nj = d_ff // cw
    tps = rows_per_seq // tm
    nh = FFN_CONV - 1
    gate_col = lambda i, j: (0, j)
    val_col = lambda i, j: (0, nj + j)
    seq_gate = lambda i, j: (i // tps, 0, j)
    seq_val = lambda i, j: (i // tps, 0, nj + j)
    y, nbg, nbv = pl.pallas_call(
        functools.partial(_ffn_kernel, tiles_per_seq=tps, final=final),
        grid=(rows // tm, nj),
        in_specs=[pl.BlockSpec((tm, d), lambda i, j: (i, 0)),
                  pl.BlockSpec((1, d), lambda i, j: (0, 0)),
                  pl.BlockSpec((None, nh, cw), seq_gate),
                  pl.BlockSpec((None, nh, cw), seq_val),
                  pl.BlockSpec((d, cw), gate_col),
                  pl.BlockSpec((d, cw), val_col),
                  pl.BlockSpec((FFN_CONV, cw), gate_col),
                  pl.BlockSpec((FFN_CONV, cw), val_col),
                  pl.BlockSpec((1, cw), gate_col),
                  pl.BlockSpec((1, cw), val_col),
                  pl.BlockSpec((cw, d), lambda i, j: (j, 0)),
                  pl.BlockSpec((1, d), lambda i, j: (0, 0))],
        out_specs=[pl.BlockSpec((tm, d), lambda i, j: (i, 0)),
                   pl.BlockSpec((None, nj, nh, cw), lambda i, j: (i // tps, 0, 0, 0)),
                   pl.BlockSpec((None, nj, nh, cw), lambda i, j: (i // tps, 0, 0, 0))],
        out_shape=[jax.ShapeDtypeStruct((rows, d), F32),
                   jax.ShapeDtypeStruct((n_seq, nj, nh, cw), F32),
                   jax.ShapeDtypeStruct((n_seq, nj, nh, cw), F32)],
        scratch_shapes=[pltpu.VMEM((tm, d), BF16),
                        pltpu.VMEM((nj, HIST + tm, cw), F32),
                        pltpu.VMEM((nj, HIST + tm, cw), F32),
                        pltpu.VMEM((tm, d), F32)],
        compiler_params=_cparams(2),
        name="ffn",
    )(h, g, buf, buf, w_up, w_up, conv_w, conv_w, conv_b, conv_b, w_down, final_g)
    unchunk = lambda a: a.transpose(0, 2, 1, 3).reshape(n_seq, nh, d_ff)
    return y, jnp.concatenate([unchunk(nbg), unchunk(nbv)], axis=-1)


def _prep_layer_weights(l, P):
    w_in = P["w_in"][l]
    d = w_in.shape[0]
    aw = A_HEADS * A_VDIM
    nqkv = B_HEADS * (2 * B_DK + B_DV)
    c_qkvb = 3 * aw
    c_beta = c_qkvb + nqkv
    c_z = c_beta + 2 * B_HEADS
    bw = B_HEADS * B_DV
    w_q = w_in[:, :aw] * (A_HEAD_DIM ** -0.5 * LOG2E)
    w_a = jnp.concatenate([w_q, w_in[:, aw:3 * aw]], axis=1).astype(BF16)
    w_b = jnp.concatenate([w_in[:, c_qkvb:c_beta + 2 * B_HEADS],
                           jnp.zeros((d, LANES - 2 * B_HEADS), F32)], axis=1).astype(BF16)
    w_c = w_in[:, c_z:c_z + bw + 2 * d].astype(BF16)
    lam_rows = jnp.zeros((8, LANES), F32)
    for r, name in enumerate(("lambda_q1", "lambda_k1", "lambda_q2", "lambda_k2")):
        lam_rows = lam_rows.at[r, :A_HEAD_DIM].set(P[name][l])
    par = jnp.zeros((8, LANES), F32)
    par = par.at[0, B_HEADS:2 * B_HEADS].set(P["gdn_a_log"][l])
    par = par.at[1, B_HEADS:2 * B_HEADS].set(P["gdn_dt_bias"][l])
    return dict(
        w_a=w_a, w_b=w_b, w_c=w_c, lam_rows=lam_rows, par=par,
        norm_mix_g=P["norm_mix_g"][l][None, :],
        subln_g=P["subln_g"][l][None, :],
        gdn_conv_w=P["gdn_conv_w"][l],
        gdn_norm_g=P["gdn_norm_g"][l][None, :],
        w_proj_a=P["w_proj_a"][l].astype(BF16),
        w_proj_b=P["w_proj_b"][l].astype(BF16),
        w_out=P["w_out"][l].astype(BF16),
        norm_ffn_g=P["norm_ffn_g"][l][None, :],
        w_up=P["w_up"][l].astype(BF16),
        ffn_conv_w=P["ffn_conv_w"][l],
        ffn_conv_b=P["ffn_conv_b"][l][None, :],
        w_down=P["w_down"][l].astype(BF16),
        final_g=P["final_norm_g"][None, :],
    )


def _layer(x, l, depth, kv_prev, W, n_seq, rows_per_seq, cache_k, cache_v, s0, gbuf, fbuf,
           final, tm, tq, tk):
    rows, d = x.shape
    tm = min(tm, rows)
    aw = A_HEADS * A_VDIM
    nqkv = B_HEADS * (2 * B_DK + B_DV)
    lam0 = _lambda_init(l)
    slopes = (2.0 ** (-8.0 * jnp.arange(1, A_HEADS + 1, dtype=F32) / A_HEADS)) * LOG2E
    prompt = cache_k is None

    v_kinds = ("h", "t") if prompt else ("h", BF16)
    q, k, kb, v, vx = _norm_proj(
        x, W["norm_mix_g"], W["w_a"],
        ((0, aw, (BF16,)), (aw, aw, ("h", BF16)), (2 * aw, aw, v_kinds)), tm if not prompt else tk,
        layer=l, depth=depth, carried=kv_prev)
    qkvb, ba = _norm_proj(x, W["norm_mix_g"], W["w_b"],
                          ((0, nqkv, (F32,)), (nqkv, LANES, (F32,))), tm)
    z, ga, gb = _norm_proj(x, W["norm_mix_g"], W["w_c"],
                           ((0, aw, (F32,)), (aw, d, (F32,)), (aw + d, d, (F32,))), tm)

    if prompt:
        o_a = _attn_prompt(q, kb, vx, W["lam_rows"], W["subln_g"], slopes, lam0, tq, tk)
    else:
        o_a = _attn_sample(q, kb, vx, cache_k, cache_v, l, W["lam_rows"], W["subln_g"],
                           slopes, lam0, rows_per_seq)

    o_b, s_new, gbuf_new = _gdn(qkvb, ba, z, s0, gbuf, W["gdn_conv_w"], W["par"],
                                W["gdn_norm_g"], n_seq, rows_per_seq,
                                min(GDN_ROWS_PER_STEP, rows_per_seq))

    h = _merge(x, o_a, o_b, ga, gb, W["w_proj_a"], W["w_proj_b"], W["w_out"], tm)
    d_ff = W["w_down"].shape[0]
    y, fbuf_new = _ffn(h, W["norm_ffn_g"], fbuf, W["w_up"], W["ffn_conv_w"], W["ffn_conv_b"],
                       W["w_down"], W["final_g"], n_seq, rows_per_seq,
                       min(FFN_ROW_TILE, rows_per_seq), d_ff // FFN_COLUMN_CHUNKS, final)
    return y, k, v, s_new, gbuf_new, fbuf_new


def kernel(x_prompt, x_sample, cache_k, cache_v, state_gdn, state_gdn_conv, state_ffn_conv, norm_mix_g, w_in, lambda_q1, lambda_k1, lambda_q2, lambda_k2, subln_g, gdn_conv_w, gdn_a_log, gdn_dt_bias, gdn_norm_g, w_proj_a, w_proj_b, w_out, norm_ffn_g, w_up, ffn_conv_w, ffn_conv_b, w_down, final_norm_g):
    P = dict(norm_mix_g=norm_mix_g, w_in=w_in, lambda_q1=lambda_q1, lambda_k1=lambda_k1,
             lambda_q2=lambda_q2, lambda_k2=lambda_k2, subln_g=subln_g, gdn_conv_w=gdn_conv_w,
             gdn_a_log=gdn_a_log, gdn_dt_bias=gdn_dt_bias, gdn_norm_g=gdn_norm_g,
             w_proj_a=w_proj_a, w_proj_b=w_proj_b, w_out=w_out, norm_ffn_g=norm_ffn_g,
             w_up=w_up, ffn_conv_w=ffn_conv_w, ffn_conv_b=ffn_conv_b, w_down=w_down,
             final_norm_g=final_norm_g)
    depth = w_in.shape[0]
    bp, tp, d = x_prompt.shape
    bs, ts, _ = x_sample.shape
    past = cache_k.shape[2]
    assert ts == CHUNK and past % CHUNK == 0 and tp % CHUNK == 0
    nqkv = state_gdn_conv.shape[-1]
    d_ff2 = state_ffn_conv.shape[-1]
    weights = [_prep_layer_weights(l, P) for l in range(depth)]
    ck = cache_k.reshape(depth, bs, past * A_HEADS, A_VDIM)
    cv = cache_v.reshape(depth, bs, past * A_HEADS, A_VDIM)

    tk = min(ATTN_KEY_TILE, tp)
    tq = min(ATTN_QUERY_BLOCK, tp)
    tm = 512

    def run(x3, nseq, rps, sample):
        assert sample or nseq == 1
        x = x3.reshape(nseq * rps, d)
        kv = tuple(jnp.zeros((depth, nseq * rps, A_HEADS, A_VDIM), F32) for _ in range(2))
        ss, gcs, fcs = [], [], []
        for l in range(depth):
            if sample:
                s0, gbuf, fbuf = state_gdn[l], state_gdn_conv[l], state_ffn_conv[l]
                c_k, c_v = ck, cv
            else:
                s0 = jnp.zeros((nseq, B_HEADS, B_DK, B_DV), F32)
                gbuf = jnp.zeros((nseq, GDN_CONV - 1, nqkv), F32)
                fbuf = jnp.zeros((nseq, FFN_CONV - 1, d_ff2), F32)
                c_k = c_v = None
            x, k_all, v_all, s_new, g_new, f_new = _layer(
                x, l, depth, kv, weights[l], nseq, rps, c_k, c_v, s0, gbuf, fbuf,
                l == depth - 1, tm, tq, tk)
            kv = (k_all, v_all)
            ss.append(s_new)
            gcs.append(g_new)
            fcs.append(f_new)
        kv_shape = (depth, nseq, rps, A_HEADS, A_VDIM)
        return (x.reshape(nseq, rps, d), kv[0].reshape(kv_shape), kv[1].reshape(kv_shape),
                jnp.stack(ss), jnp.stack(gcs), jnp.stack(fcs))

    y_p, k_p, v_p, s_p, gc_p, fc_p = run(x_prompt, bp, tp, False)
    y_s, k_s, v_s, s_s, gc_s, fc_s = run(x_sample, bs, ts, True)
    return (y_p, y_s, k_p, v_p, s_p, gc_p, fc_p, k_s, v_s, s_s, gc_s, fc_s)
```

```python
import functools
import math

import jax
import jax.numpy as jnp
from jax import lax
from jax.experimental import pallas as pl
from jax.experimental.pallas import tpu as pltpu

F32 = jnp.float32
BF16 = jnp.bfloat16

EPS = 1e-6
NEG = -1e30
LOG2E = 1.4426950408889634

CHUNK = 64
A_HEADS = 8
A_HEAD_DIM = 64
A_VDIM = 128
B_HEADS = 8
B_DK = 128
B_DV = 128
GDN_CONV = 4
FFN_CONV = 3
LANES = 128
HIST = 8

VMEM_LIMIT = 56 * 1024 * 1024


def _cparams(n_axes):
    return pltpu.CompilerParams(
        dimension_semantics=("arbitrary",) * n_axes,
        vmem_limit_bytes=VMEM_LIMIT)


def _lambda_init(layer):
    return 0.8 - 0.6 * math.exp(-0.3 * layer)


def _norm_proj_kernel(x_ref, g_ref, w_ref, *refs, segs, n_carried):
    out_refs = refs[n_carried:]
    x = x_ref[...]
    ms = jnp.mean(x * x, axis=-1, keepdims=True)
    xn = (x * lax.rsqrt(ms + EPS) * g_ref[...]).astype(BF16)
    k = 0
    for start, width, kinds in segs:
        w = w_ref[:, start:start + width]
        r = None
        for kind in kinds:
            o_ref = out_refs[k]
            k += 1
            if r is None:
                r = jnp.dot(xn, w, preferred_element_type=F32)
            if kind == "t":
                o_ref[...] = r.T.astype(o_ref.dtype)
            elif kind == "h":
                for hd in range(width // LANES):
                    o_ref[:, hd, :] = r[:, hd * LANES:(hd + 1) * LANES]
            else:
                o_ref[...] = r.astype(o_ref.dtype)


def _norm_proj(x, g, w, segs, tm, layer=0, depth=1, carried=()):
    rows, d = x.shape
    assert rows % tm == 0
    out_shape, out_specs, ksegs, h_outs = [], [], [], []
    for start, width, kinds in segs:
        kk = []
        for kind in kinds:
            if kind == "t":
                out_shape.append(jax.ShapeDtypeStruct((rows // tm, width, tm), BF16))
                out_specs.append(pl.BlockSpec((None, width, tm), lambda i: (i, 0, 0)))
                kk.append("t")
            elif kind == "h":
                h_outs.append(len(out_shape))
                out_shape.append(jax.ShapeDtypeStruct(
                    (depth, rows, width // LANES, LANES), F32))
                out_specs.append(pl.BlockSpec((None, tm, width // LANES, LANES),
                                              lambda i: (layer, i, 0, 0)))
                kk.append("h")
            else:
                out_shape.append(jax.ShapeDtypeStruct((rows, width), kind))
                out_specs.append(pl.BlockSpec((tm, width), lambda i: (i, 0)))
                kk.append("n")
        ksegs.append((start, width, tuple(kk)))
    assert len(carried) in (0, len(h_outs))
    return pl.pallas_call(
        functools.partial(_norm_proj_kernel, segs=tuple(ksegs), n_carried=len(carried)),
        grid=(rows // tm,),
        in_specs=[pl.BlockSpec((tm, d), lambda i: (i, 0)),
                  pl.BlockSpec((1, d), lambda i: (0, 0)),
                  pl.BlockSpec(w.shape, lambda i: (0, 0))]
                 + [pl.BlockSpec(memory_space=pl.ANY)] * len(carried),
        out_specs=out_specs,
        out_shape=out_shape,
        input_output_aliases={3 + n: h_outs[n] for n in range(len(carried))},
        compiler_params=_cparams(1),
        name="norm_proj",
    )(x, g, w, *carried)


def _lambda_value(lam_ref, lam0):
    a = jnp.sum(lam_ref[0:1, :] * lam_ref[1:2, :], axis=-1, keepdims=True)
    b = jnp.sum(lam_ref[2:3, :] * lam_ref[3:4, :], axis=-1, keepdims=True)
    return jnp.exp(a) - jnp.exp(b) + lam0


def _stack_maps(q):
    lane = lax.broadcasted_iota(jnp.int32, q.shape, 1)
    zero = jnp.zeros_like(q)
    return jnp.concatenate([jnp.where(lane < A_HEAD_DIM, q, zero),
                            jnp.where(lane >= A_HEAD_DIM, q, zero)], axis=0)


NT_DIMS = (((1,), (1,)), ((), ()))
ATTN_KEY_TILE = 512
ATTN_QUERY_BLOCK = 1024


def _attn_prompt_kernel(slopes_ref, lam_ref, g_ref, q_ref, k_ref, vt_ref, o_ref,
                        qq_sc, kaug_sc, dfix_sc, sa_sc, sb_sc, m_sc, l_sc, acc_sc,
                        *, tq, tk, lam0):
    h = pl.program_id(0)
    i = pl.program_id(1)
    slope = slopes_ref[h]
    q0 = i * tq
    nt = NT_DIMS
    w = tk
    nd = tq // tk
    n_off = i * nd
    n_strips = 2 * tq // w

    q_t = q_ref[...].astype(F32).T
    feat = lax.broadcasted_iota(jnp.int32, q_t.shape, 0)
    qq_sc[:A_VDIM, :] = jnp.concatenate(
        [jnp.where(feat < A_HEAD_DIM, q_t, 0.0), jnp.where(feat >= A_HEAD_DIM, q_t, 0.0)],
        axis=1).astype(BF16)
    row_q = lax.broadcasted_iota(jnp.int32, (LANES, 2 * tq), 0)
    qq_sc[A_VDIM:, :] = jnp.where(row_q < 3, 1.0, 0.0).astype(BF16)

    @pl.when(i == 0)
    def _():
        b = slope * lax.broadcasted_iota(jnp.int32, (tk, LANES), 0).astype(F32)
        b1 = b.astype(BF16).astype(F32)
        b2 = (b - b1).astype(BF16).astype(F32)
        b3 = b - b1 - b2
        lane_k = lax.broadcasted_iota(jnp.int32, (tk, LANES), 1)
        kaug_sc[...] = jnp.where(lane_k == 0, b1, jnp.where(
            lane_k == 1, b2, jnp.where(lane_k == 2, b3, 0.0))).astype(BF16)
        c = lax.broadcasted_iota(jnp.int32, (tk, w), 0)
        r = lax.broadcasted_iota(jnp.int32, (tk, w), 1)
        fix = slope * jnp.minimum(2 * (r - c), 0).astype(F32)
        dfix_sc[...] = jnp.where((c // CHUNK) <= (r // CHUNK), fix, NEG)

    m_sc[...] = jnp.full(m_sc.shape, NEG, F32)
    l_sc[...] = jnp.zeros(l_sc.shape, F32)
    acc_sc[...] = jnp.zeros(acc_sc.shape, F32)

    groups = 8

    def key_reduce(op, x):
        part = op(x.reshape(groups, tk // groups, x.shape[-1]), axis=0)
        return op(part, axis=0, keepdims=True)

    def softmax_update(st, s_ref, kappa, vtt):
        sl = slice(st * w, (st + 1) * w)
        m_prev = m_sc[:, sl]
        m_new = jnp.maximum(m_prev, key_reduce(jnp.max, s_ref[:, sl]) + kappa)
        alpha = jnp.exp2(m_prev - m_new)
        p = jnp.exp2(s_ref[:, sl] - (m_new - kappa))
        pv = jnp.dot(jnp.concatenate([vtt, jnp.ones((16, tk), BF16)], axis=0), p.astype(BF16),
                     preferred_element_type=F32)
        l_sc[:, sl] = alpha * l_sc[:, sl] + pv[A_VDIM:A_VDIM + 1]
        acc_sc[:, sl] = alpha * acc_sc[:, sl] + pv[:A_VDIM]
        m_sc[:, sl] = m_new

    def produce(j, s_ref, strips=None):
        k0 = pl.multiple_of(j * tk, tk)
        lhs = jnp.concatenate([k_ref[pl.ds(k0, tk), :], kaug_sc[...]], axis=1)
        if strips is None:
            s_ref[...] = jnp.dot(lhs, qq_sc[...], preferred_element_type=F32)
        else:
            for st in strips:
                sl = slice(st * w, (st + 1) * w)
                s_ref[:, sl] = jnp.dot(lhs, qq_sc[:, sl], preferred_element_type=F32)

    def visible_strips(d):
        return [st for st in range(n_strips) if (st * w) % tq >= d * tk]

    def consume(j, s_ref):
        kappa = slope * (j * tk - q0).astype(F32)
        vtj = vt_ref[j]
        for st in range(n_strips):
            softmax_update(st, s_ref, kappa, vtj)

    def consume_overlap(d, s_ref):
        kappa = slope * (d * tk)
        vtd = vt_ref[n_off + d]
        for st in visible_strips(d):
            if (st * w) % tq == d * tk:
                sl = slice(st * w, (st + 1) * w)
                s_ref[:, sl] = s_ref[:, sl] + dfix_sc[...]
            softmax_update(st, s_ref, kappa, vtd)

    def overlap_tiles(first_ref, second_ref):
        for d in range(nd):
            cur, nxt = (first_ref, second_ref) if d % 2 == 0 else (second_ref, first_ref)
            if d + 1 < nd:
                produce(n_off + d + 1, nxt, visible_strips(d + 1))
            consume_overlap(d, cur)

    produce(0, sa_sc)

    def two_tiles(j):
        produce(j + 1, sb_sc)
        consume(j, sa_sc)
        produce(j + 2, sa_sc)
        consume(j + 1, sb_sc)

    def four_tiles(t, carry):
        two_tiles(4 * t)
        two_tiles(4 * t + 2)
        return carry

    lax.fori_loop(0, n_off // 4, four_tiles, 0)

    @pl.when(n_off % 4 >= 2)
    def _():
        two_tiles((n_off // 4) * 4)

    if nd % 2 == 0:
        overlap_tiles(sa_sc, sb_sc)
    else:
        @pl.when(n_off % 2 == 0)
        def _():
            overlap_tiles(sa_sc, sb_sc)

        @pl.when(n_off % 2 == 1)
        def _():
            produce(n_off, sb_sc)
            consume(n_off - 1, sa_sc)
            overlap_tiles(sb_sc, sa_sc)

    lam = _lambda_value(lam_ref, lam0)
    inv_l = 1.0 / l_sc[...]
    acc = acc_sc[...] * inv_l
    o = acc[:, :tq] - lam * acc[:, tq:]
    ms = jnp.mean(o * o, axis=0, keepdims=True)
    o = (o * lax.rsqrt(ms + EPS)).T
    o_ref[...] = (o * g_ref[...] * (1.0 - lam0)).astype(o_ref.dtype)


def _attn_prompt(q, kb, vt, lam_rows, subln_g, slopes, lam0, tq, tk):
    t = q.shape[0]
    assert t % tq == 0 and tq % tk == 0
    return pl.pallas_call(
        functools.partial(_attn_prompt_kernel, tq=tq, tk=tk, lam0=lam0),
        grid=(A_HEADS, t // tq),
        in_specs=[pl.BlockSpec(memory_space=pltpu.SMEM),
                  pl.BlockSpec((8, LANES), lambda h, i: (0, 0)),
                  pl.BlockSpec((1, A_VDIM), lambda h, i: (0, 0)),
                  pl.BlockSpec((tq, A_VDIM), lambda h, i: (i, h)),
                  pl.BlockSpec((t, A_VDIM), lambda h, i: (0, h)),
                  pl.BlockSpec((t // tk, A_VDIM, tk), lambda h, i: (0, h, 0))],
        out_specs=pl.BlockSpec((tq, A_VDIM), lambda h, i: (i, h)),
        out_shape=jax.ShapeDtypeStruct((t, A_HEADS * A_VDIM), BF16),
        scratch_shapes=[pltpu.VMEM((2 * LANES, 2 * tq), BF16),
                        pltpu.VMEM((tk, LANES), BF16),
                        pltpu.VMEM((tk, tk), F32),
                        pltpu.VMEM((tk, 2 * tq), F32),
                        pltpu.VMEM((tk, 2 * tq), F32),
                        pltpu.VMEM((1, 2 * tq), F32),
                        pltpu.VMEM((1, 2 * tq), F32),
                        pltpu.VMEM((A_VDIM, 2 * tq), F32)],
        compiler_params=_cparams(2),
        name="attn_prompt",
    )(slopes, lam_rows, subln_g, q, kb, vt)


SAMPLE_KEY_TILE = 2048


def _attn_sample_kernel(slopes_ref, lam_ref, g_ref, q_ref, kn_ref, vn_ref, kc_ref, vc_ref,
                        o_ref, m_sc, l_sc, acc_sc, *, past, lam0):
    t = pl.program_id(1)
    nt = pl.num_programs(1)
    tq = q_ref.shape[0]
    tk = kc_ref.shape[0] // A_HEADS

    @pl.when(t == 0)
    def _():
        m_sc[...] = jnp.full(m_sc.shape, NEG, F32)
        l_sc[...] = jnp.zeros(l_sc.shape, F32)
        acc_sc[...] = jnp.zeros(acc_sc.shape, F32)

    def update(h, s, v):
        m_prev = m_sc[h]
        m_new = jnp.maximum(m_prev, jnp.max(s, axis=-1, keepdims=True))
        alpha = jnp.exp2(m_prev - m_new)
        p = jnp.exp2(s - m_new)
        l_sc[h] = alpha * l_sc[h] + jnp.sum(p, axis=-1, keepdims=True)
        acc_sc[h] = alpha * acc_sc[h] + jnp.dot(p.astype(BF16), v, preferred_element_type=F32)
        m_sc[h] = m_new

    kpos = t * tk + lax.broadcasted_iota(jnp.int32, (1, tk), 1)
    rel = (kpos - past).astype(F32)
    heads = range(A_HEADS)
    qqs = [_stack_maps(q_ref[:, h * A_VDIM:(h + 1) * A_VDIM]) for h in heads]
    scores = [lax.dot_general(qqs[h], kc_ref[pl.ds(h, tk, stride=A_HEADS), :].astype(BF16),
                              NT_DIMS, preferred_element_type=F32) for h in heads]
    m_prev = [m_sc[h] for h in heads]
    m_new, alpha, p = [], [], []
    for h in heads:
        s = scores[h] + slopes_ref[h] * rel
        m_new.append(jnp.maximum(m_prev[h], jnp.max(s, axis=-1, keepdims=True)))
        alpha.append(jnp.exp2(m_prev[h] - m_new[h]))
        p.append(jnp.exp2(s - m_new[h]))
    pv = [jnp.dot(p[h].astype(BF16), vc_ref[pl.ds(h, tk, stride=A_HEADS), :].astype(BF16),
                  preferred_element_type=F32) for h in heads]
    for h in heads:
        l_sc[h] = alpha[h] * l_sc[h] + jnp.sum(p[h], axis=-1, keepdims=True)
        acc_sc[h] = alpha[h] * acc_sc[h] + pv[h]
        m_sc[h] = m_new[h]

    @pl.when(t == nt - 1)
    def _():
        lam = _lambda_value(lam_ref, lam0)
        r = lax.broadcasted_iota(jnp.int32, (2 * tq, tq), 0)
        c = lax.broadcasted_iota(jnp.int32, (2 * tq, tq), 1)
        r = jnp.where(r >= tq, r - tq, r)
        own = jnp.minimum(c, 2 * r - c).astype(F32)
        for h in range(A_HEADS):
            cols = slice(h * A_VDIM, (h + 1) * A_VDIM)
            s = lax.dot_general(qqs[h], kn_ref[:, cols], NT_DIMS, preferred_element_type=F32)
            update(h, s + slopes_ref[h] * own, vn_ref[:, cols])
            acc = acc_sc[h] * (1.0 / l_sc[h])
            o = acc[:tq] - lam * acc[tq:]
            ms = jnp.mean(o * o, axis=-1, keepdims=True)
            o_ref[:, cols] = (o * lax.rsqrt(ms + EPS) * g_ref[...]
                              * (1.0 - lam0)).astype(o_ref.dtype)


def _attn_sample(q, kb, vb, cache_k, cache_v, layer, lam_rows, subln_g, slopes, lam0, tq):
    nb, past = cache_k.shape[1], cache_k.shape[2] // A_HEADS
    tk = min(SAMPLE_KEY_TILE, past)
    assert past % tk == 0
    aw = A_HEADS * A_VDIM
    rows = pl.BlockSpec((tq, aw), lambda b, t: (b, 0))
    cache = pl.BlockSpec((None, None, tk * A_HEADS, A_VDIM), lambda b, t: (layer, b, t, 0))
    return pl.pallas_call(
        functools.partial(_attn_sample_kernel, past=past, lam0=lam0),
        grid=(nb, past // tk),
        in_specs=[pl.BlockSpec(memory_space=pltpu.SMEM),
                  pl.BlockSpec((8, LANES), lambda b, t: (0, 0)),
                  pl.BlockSpec((1, A_VDIM), lambda b, t: (0, 0)),
                  rows, rows, rows, cache, cache],
        out_specs=rows,
        out_shape=jax.ShapeDtypeStruct((nb * tq, aw), BF16),
        scratch_shapes=[pltpu.VMEM((A_HEADS, 2 * tq, 1), F32),
                        pltpu.VMEM((A_HEADS, 2 * tq, 1), F32),
                        pltpu.VMEM((A_HEADS, 2 * tq, A_VDIM), F32)],
        compiler_params=_cparams(2),
        name="attn_sample",
    )(slopes, lam_rows, subln_g, q, kb, vb, cache_k, cache_v)


_HI = lax.Precision.HIGHEST
GDN_ROWS_PER_STEP = 8 * CHUNK


def _dot(a, b):
    return jnp.dot(a, b, preferred_element_type=F32)


def _split(a):
    hi = a.astype(BF16)
    return hi, (a - hi.astype(F32)).astype(BF16)


def _block_diag2(a, left):
    zero = jnp.zeros_like(a)
    return jnp.concatenate([jnp.where(left, a, zero), jnp.where(left, zero, a)], axis=0)


def _pair_prod(a_parts, b_parts, left):
    ah, bh = a_parts[0], b_parts[0]
    bdh = _block_diag2(bh, left)
    if len(a_parts) == 1 and len(b_parts) == 1:
        return _dot(ah, bdh)
    if len(a_parts) == 1:
        return _dot(jnp.concatenate([ah, ah], axis=1),
                    jnp.concatenate([bdh, _block_diag2(b_parts[1], left)], axis=0))
    r = _dot(jnp.concatenate([ah, a_parts[1]], axis=1), jnp.concatenate([bdh, bdh], axis=0))
    if len(b_parts) == 1:
        return r
    return r + _dot(ah, _block_diag2(b_parts[1], left))


def _inverse_masks(ri, cj):
    n = ri.shape[0]
    masks = [(ri // 2) == (cj // 2)]
    b = 2
    while b < n:
        same_big = ((ri // (2 * b)) == (cj // (2 * b))).astype(jnp.int32)
        same_small = ((ri // b) == (cj // b)).astype(jnp.int32)
        masks.append((same_big - same_small) > 0)
        b *= 2
    return masks


def _unit_lower_inverse(lows, masks, eye, left):
    xs = [eye - jnp.where(masks[0], low, 0.0) for low in lows]
    for mask in masks[1:]:
        xparts = [_split(x) for x in xs]
        ys = [_pair_prod((jnp.where(mask, low, 0.0).astype(BF16),), xp, left)
              for low, xp in zip(lows, xparts)]
        xs = [x - _pair_prod(xp, _split(y), left) for x, xp, y in zip(xs, xparts, ys)]
    return xs


def _gdn_kernel(x_ref, buf_ref, ba_ref, z_ref, s0_ref, cw_ref, par_ref, ng_ref,
                o_ref, s_out_ref, buf_out_ref, xp_sc, s_sc):
    c_idx = pl.program_id(1)
    nc = pl.num_programs(1)
    R = x_ref.shape[0]
    C = CHUNK
    n_sub = R // C
    nq = B_HEADS * B_DK
    cat = jnp.concatenate

    @pl.when(c_idx == 0)
    def _():
        xp_sc[HIST - (GDN_CONV - 1):HIST, :] = buf_ref[...]
        s_sc[...] = s0_ref[...].astype(F32)

    xp_sc[HIST:HIST + R, :] = x_ref[...]
    xp = xp_sc[...]
    conv = xp[HIST:] * cw_ref[GDN_CONV - 1:GDN_CONV, :]
    for j in range(GDN_CONV - 1):
        shifted = pltpu.roll(xp, GDN_CONV - 1 - j, axis=0)
        conv = conv + shifted[HIST:] * cw_ref[j:j + 1, :]
    conv = conv * jax.nn.sigmoid(conv)
    tail = xp_sc[R:R + HIST, :]
    xp_sc[0:HIST, :] = tail
    buf_out_ref[...] = tail[HIST - (GDN_CONV - 1):HIST, :]

    ba = ba_ref[...]
    lane = lax.broadcasted_iota(jnp.int32, ba.shape, 1)
    beta_all = jax.nn.sigmoid(ba)
    g_all = jnp.where((lane >= B_HEADS) & (lane < 2 * B_HEADS),
                      -jnp.exp(par_ref[0:1, :]) * jax.nn.softplus(ba + par_ref[1:2, :]), 0.0)
    rr = lax.broadcasted_iota(jnp.int32, (R, R), 0)
    cc = lax.broadcasted_iota(jnp.int32, (R, R), 1)
    same_chunk = ((rr // C) - (cc // C)) == 0
    tri = jnp.where(same_chunk, jnp.where(rr >= cc, 1.0, 0.0), 0.0).astype(F32)
    gcum_all = jnp.dot(tri, g_all, precision=_HI, preferred_element_type=F32)

    ri = lax.broadcasted_iota(jnp.int32, (C, 2 * C), 0)
    cl = lax.broadcasted_iota(jnp.int32, (C, 2 * C), 1)
    left = cl < C
    cj = jnp.where(left, cl, cl - C)
    incl = ri >= cj
    strict = ri > cj
    eye = jnp.where(ri == cj, 1.0, 0.0).astype(F32)
    inv_masks = _inverse_masks(ri, cj)
    zk = jnp.zeros((C, B_DK), F32)
    z2 = jnp.zeros((C, B_DV + B_DK), F32)
    zv = jnp.zeros((C, B_DV), F32)

    subs = range(n_sub)
    pairs = range(B_HEADS // 2)
    q, k, v, kb, beta, gc, eg, g_last, gcum_t = [], [], [], [], [], [], [], [], []
    for c in subs:
        rows = slice(c * C, (c + 1) * C)
        gsub = gcum_all[rows]
        gcum_t.append(cat([gsub, gsub], axis=0).T)
        qc, kc, vc, kbc, bc, gcc, egc = [], [], [], [], [], [], []
        for h in range(B_HEADS):
            qh = conv[rows, h * B_DK:(h + 1) * B_DK]
            kh = conv[rows, nq + h * B_DK:nq + (h + 1) * B_DK]
            qc.append(qh * lax.rsqrt(jnp.sum(qh * qh, axis=-1, keepdims=True) + EPS)
                      * (B_DK ** -0.5))
            kc.append(kh * lax.rsqrt(jnp.sum(kh * kh, axis=-1, keepdims=True) + EPS))
            vc.append(conv[rows, 2 * nq + h * B_DV:2 * nq + (h + 1) * B_DV])
            bc.append(beta_all[rows, h:h + 1])
            gcc.append(gsub[:, B_HEADS + h:B_HEADS + h + 1])
            kbc.append(kc[-1] * bc[-1])
            egc.append(jnp.exp(gcc[-1]))
        q.append(qc)
        k.append(kc)
        v.append(vc)
        kb.append(kbc)
        beta.append(bc)
        gc.append(gcc)
        eg.append(egc)
        g_last.append([g[C - 1:C, :] for g in gcc])

    lows, attns = [], []
    for c in subs:
        for p in pairs:
            a, b = 2 * p, 2 * p + 1
            g_col = jnp.where(left, gc[c][a], gc[c][b])
            g_row = jnp.where(left[0:1], gcum_t[c][B_HEADS + a:B_HEADS + a + 1, :],
                              gcum_t[c][B_HEADS + b:B_HEADS + b + 1, :])
            decay = jnp.where(incl, jnp.exp(jnp.where(incl, g_col - g_row, 0.0)), 0.0)
            lhs = cat([cat([kb[c][a], kb[c][b]], axis=1), cat([q[c][a], q[c][b]], axis=1)],
                      axis=0)
            rhs = cat([cat([k[c][a], zk], axis=1), cat([zk, k[c][b]], axis=1)], axis=0)
            m1 = lax.dot_general(lhs.astype(BF16), rhs.astype(BF16), NT_DIMS,
                                 preferred_element_type=F32)
            lows.append(jnp.where(strict, m1[:C] * decay, 0.0).astype(BF16).astype(F32))
            attns.append(m1[C:] * decay)

    tinvs = _unit_lower_inverse(lows, inv_masks, eye, left)

    sols = []
    for c in subs:
        for p in pairs:
            a, b = 2 * p, 2 * p + 1
            rhs_a = cat([v[c][a] * beta[c][a], kb[c][a] * eg[c][a]], axis=1)
            rhs_b = cat([v[c][b] * beta[c][b], kb[c][b] * eg[c][b]], axis=1)
            rhs_bd = cat([cat([rhs_a, z2], axis=1), cat([z2, rhs_b], axis=1)], axis=0)
            sols.append(_dot(tinvs[c * len(pairs) + p].astype(BF16), rhs_bd.astype(BF16)))

    s_cur = [s_sc[h] for h in range(B_HEADS)]
    for c in subs:
        rows = slice(c * C, (c + 1) * C)
        v_new, q_s = [], []
        for h in range(B_HEADS):
            sol = sols[c * len(pairs) + h // 2]
            base = (h % 2) * (B_DV + B_DK)
            u = sol[:, base:base + B_DV]
            w = sol[:, base + B_DV:base + B_DV + B_DK]
            m2 = _dot(cat([w, q[c][h] * eg[c][h]], axis=0).astype(BF16), s_cur[h].astype(BF16))
            v_new.append(u - m2[:C])
            q_s.append(m2[C:])
        o_pairs, upds = [], []
        for p in pairs:
            a, b = 2 * p, 2 * p + 1
            vn_bd = cat([cat([v_new[a], zv], axis=1), cat([zv, v_new[b]], axis=1)],
                        axis=0).astype(BF16)
            o_pairs.append(_dot(attns[c * len(pairs) + p].astype(BF16), vn_bd))
            kd_t = cat([k[c][a] * jnp.exp(g_last[c][a] - gc[c][a]),
                        k[c][b] * jnp.exp(g_last[c][b] - gc[c][b])], axis=0).T
            upds.append(_dot(kd_t.astype(BF16), vn_bd))
        for h in range(B_HEADS):
            lo = (h % 2) * B_DV
            s_cur[h] = s_cur[h] * jnp.exp(g_last[c][h]) + upds[h // 2][:, lo:lo + B_DV]
            o = q_s[h] + o_pairs[h // 2][:, lo:lo + B_DV]
            zh = z_ref[rows, h * B_DV:(h + 1) * B_DV]
            on = o * lax.rsqrt(jnp.mean(o * o, axis=-1, keepdims=True) + EPS) * ng_ref[...]
            o_ref[rows, h * B_DV:(h + 1) * B_DV] = (
                on * (zh * jax.nn.sigmoid(zh))).astype(o_ref.dtype)
    for h in range(B_HEADS):
        s_sc[h] = s_cur[h]

    @pl.when(c_idx == nc - 1)
    def _():
        s_out_ref[...] = s_sc[...].astype(s_out_ref.dtype)


def _gdn(qkv, ba, z, s0, buf, conv_w, par, norm_g, n_seq, rows_per_seq, rows_per_step):
    assert rows_per_seq % rows_per_step == 0 and rows_per_step % CHUNK == 0
    nc = rows_per_seq // rows_per_step
    nqkv = qkv.shape[1]
    row = lambda b, c: (b * nc + c, 0)
    return pl.pallas_call(
        _gdn_kernel,
        grid=(n_seq, nc),
        in_specs=[pl.BlockSpec((rows_per_step, nqkv), row),
                  pl.BlockSpec((None, GDN_CONV - 1, nqkv), lambda b, c: (b, 0, 0)),
                  pl.BlockSpec((rows_per_step, LANES), row),
                  pl.BlockSpec((rows_per_step, B_HEADS * B_DV), row),
                  pl.BlockSpec((None, B_HEADS, B_DK, B_DV), lambda b, c: (b, 0, 0, 0)),
                  pl.BlockSpec((GDN_CONV, nqkv), lambda b, c: (0, 0)),
                  pl.BlockSpec((8, LANES), lambda b, c: (0, 0)),
                  pl.BlockSpec((1, B_DV), lambda b, c: (0, 0))],
        out_specs=[pl.BlockSpec((rows_per_step, B_HEADS * B_DV), row),
                   pl.BlockSpec((None, B_HEADS, B_DK, B_DV), lambda b, c: (b, 0, 0, 0)),
                   pl.BlockSpec((None, GDN_CONV - 1, nqkv), lambda b, c: (b, 0, 0))],
        out_shape=[jax.ShapeDtypeStruct((n_seq * rows_per_seq, B_HEADS * B_DV), BF16),
                   jax.ShapeDtypeStruct((n_seq, B_HEADS, B_DK, B_DV), F32),
                   jax.ShapeDtypeStruct((n_seq, GDN_CONV - 1, nqkv), F32)],
        scratch_shapes=[pltpu.VMEM((HIST + rows_per_step, nqkv), F32),
                        pltpu.VMEM((B_HEADS, B_DK, B_DV), F32)],
        compiler_params=_cparams(2),
        name="gdn",
    )(qkv, buf, ba, z, s0, conv_w, par, norm_g)


def _merge_kernel(x_ref, oa_ref, ob_ref, ga_ref, gb_ref, wa_ref, wb_ref, wo_ref, h_ref):
    a = jnp.dot(oa_ref[...], wa_ref[...], preferred_element_type=F32)
    b = jnp.dot(ob_ref[...], wb_ref[...], preferred_element_type=F32)
    merged = jax.nn.sigmoid(ga_ref[...]) * a + jax.nn.sigmoid(gb_ref[...]) * b
    h_ref[...] = x_ref[...] + jnp.dot(merged.astype(BF16), wo_ref[...],
                                      preferred_element_type=F32)


def _merge(x, oa, ob, ga, gb, wa, wb, wo, tm):
    rows, d = x.shape
    rowspec = pl.BlockSpec((tm, d), lambda i: (i, 0))
    wspec = pl.BlockSpec((d, d), lambda i: (0, 0))
    return pl.pallas_call(
        _merge_kernel,
        grid=(rows // tm,),
        in_specs=[rowspec] * 5 + [wspec] * 3,
        out_specs=rowspec,
        out_shape=jax.ShapeDtypeStruct((rows, d), F32),
        compiler_params=_cparams(1),
        name="merge",
    )(x, oa, ob, ga, gb, wa, wb, wo)


FFN_ROW_TILE = 512
FFN_COLUMN_CHUNKS = 1


def _ffn_kernel(h_ref, g_ref, buf_g_ref, buf_v_ref, wg_ref, wv_ref, cwg_ref, cwv_ref,
                cbg_ref, cbv_ref, wd_ref, fg_ref, y_ref, nbuf_g_ref, nbuf_v_ref,
                hn_sc, ug_sc, uv_sc, acc_sc, *, tiles_per_seq, final):
    i = pl.program_id(0)
    j = pl.program_id(1)
    nj = pl.num_programs(1)
    tm = h_ref.shape[0]
    first = (i % tiles_per_seq) == 0
    nh = FFN_CONV - 1

    @pl.when(j == 0)
    def _():
        x = h_ref[...]
        ms = jnp.mean(x * x, axis=-1, keepdims=True)
        hn_sc[...] = (x * lax.rsqrt(ms + EPS) * g_ref[...]).astype(BF16)
        acc_sc[...] = jnp.zeros(acc_sc.shape, F32)

    def conv_half(u_sc, w_ref, cw_ref, cb_ref, buf_ref, nbuf_ref):
        @pl.when(first)
        def _():
            u_sc[j, HIST - nh:HIST, :] = buf_ref[...]
        u_sc[j, HIST:HIST + tm, :] = jnp.dot(hn_sc[...], w_ref[...], preferred_element_type=F32)
        out = u_sc[j, HIST:HIST + tm, :] * cw_ref[nh:nh + 1, :] + cb_ref[...]
        for t in range(nh):
            off = HIST - nh + t
            out = out + u_sc[j, off:off + tm, :] * cw_ref[t:t + 1, :]
        tail = u_sc[j, tm:tm + HIST, :]
        u_sc[j, 0:HIST, :] = tail
        nbuf_ref[j] = tail[HIST - nh:HIST, :]
        return out

    gate = conv_half(ug_sc, wg_ref, cwg_ref, cbg_ref, buf_g_ref, nbuf_g_ref)
    val = conv_half(uv_sc, wv_ref, cwv_ref, cbv_ref, buf_v_ref, nbuf_v_ref)
    act = (gate * jax.nn.sigmoid(gate) * val).astype(BF16)
    acc_sc[...] += jnp.dot(act, wd_ref[...], preferred_element_type=F32)

    @pl.when(j == nj - 1)
    def _():
        y = h_ref[...] + acc_sc[...]
        if final:
            ms = jnp.mean(y * y, axis=-1, keepdims=True)
            y = y * lax.rsqrt(ms + EPS) * fg_ref[...]
        y_ref[...] = y


def _ffn(h, g, buf, w_up, conv_w, conv_b, w_down, final_g, n_seq, rows_per_seq, tm, cw, final):
    rows, d = h.shape
    d_ff = w_down.shape[0]
    assert d_ff % cw == 0 and rows_per_seq % tm == 0
    nj = d_ff // cw
    tps = rows_per_seq // tm
    nh = FFN_CONV - 1
    gate_col = lambda i, j: (0, j)
    val_col = lambda i, j: (0, nj + j)
    seq_gate = lambda i, j: (i // tps, 0, j)
    seq_val = lambda i, j: (i // tps, 0, nj + j)
    y, nbg, nbv = pl.pallas_call(
        functools.partial(_ffn_kernel, tiles_per_seq=tps, final=final),
        grid=(rows // tm, nj),
        in_specs=[pl.BlockSpec((tm, d), lambda i, j: (i, 0)),
                  pl.BlockSpec((1, d), lambda i, j: (0, 0)),
                  pl.BlockSpec((None, nh, cw), seq_gate),
                  pl.BlockSpec((None, nh, cw), seq_val),
                  pl.BlockSpec((d, cw), gate_col),
                  pl.BlockSpec((d, cw), val_col),
                  pl.BlockSpec((FFN_CONV, cw), gate_col),
                  pl.BlockSpec((FFN_CONV, cw), val_col),
                  pl.BlockSpec((1, cw), gate_col),
                  pl.BlockSpec((1, cw), val_col),
                  pl.BlockSpec((cw, d), lambda i, j: (j, 0)),
                  pl.BlockSpec((1, d), lambda i, j: (0, 0))],
        out_specs=[pl.BlockSpec((tm, d), lambda i, j: (i, 0)),
                   pl.BlockSpec((None, nj, nh, cw), lambda i, j: (i // tps, 0, 0, 0)),
                   pl.BlockSpec((None, nj, nh, cw), lambda i, j: (i // tps, 0, 0, 0))],
        out_shape=[jax.ShapeDtypeStruct((rows, d), F32),
                   jax.ShapeDtypeStruct((n_seq, nj, nh, cw), F32),
                   jax.ShapeDtypeStruct((n_seq, nj, nh, cw), F32)],
        scratch_shapes=[pltpu.VMEM((tm, d), BF16),
                        pltpu.VMEM((nj, HIST + tm, cw), F32),
                        pltpu.VMEM((nj, HIST + tm, cw), F32),
                        pltpu.VMEM((tm, d), F32)],
        compiler_params=_cparams(2),
        name="ffn",
    )(h, g, buf, buf, w_up, w_up, conv_w, conv_w, conv_b, conv_b, w_down, final_g)
    unchunk = lambda a: a.transpose(0, 2, 1, 3).reshape(n_seq, nh, d_ff)
    return y, jnp.concatenate([unchunk(nbg), unchunk(nbv)], axis=-1)


def _prep_layer_weights(l, P):
    w_in = P["w_in"][l]
    d = w_in.shape[0]
    aw = A_HEADS * A_VDIM
    nqkv = B_HEADS * (2 * B_DK + B_DV)
    c_qkvb = 3 * aw
    c_beta = c_qkvb + nqkv
    c_z = c_beta + 2 * B_HEADS
    bw = B_HEADS * B_DV
    w_q = w_in[:, :aw] * (A_HEAD_DIM ** -0.5 * LOG2E)
    w_a = jnp.concatenate([w_q, w_in[:, aw:3 * aw]], axis=1).astype(BF16)
    w_b = jnp.concatenate([w_in[:, c_qkvb:c_beta + 2 * B_HEADS],
                           jnp.zeros((d, LANES - 2 * B_HEADS), F32)], axis=1).astype(BF16)
    w_c = w_in[:, c_z:c_z + bw + 2 * d].astype(BF16)
    lam_rows = jnp.zeros((8, LANES), F32)
    for r, name in enumerate(("lambda_q1", "lambda_k1", "lambda_q2", "lambda_k2")):
        lam_rows = lam_rows.at[r, :A_HEAD_DIM].set(P[name][l])
    par = jnp.zeros((8, LANES), F32)
    par = par.at[0, B_HEADS:2 * B_HEADS].set(P["gdn_a_log"][l])
    par = par.at[1, B_HEADS:2 * B_HEADS].set(P["gdn_dt_bias"][l])
    return dict(
        w_a=w_a, w_b=w_b, w_c=w_c, lam_rows=lam_rows, par=par,
        norm_mix_g=P["norm_mix_g"][l][None, :],
        subln_g=P["subln_g"][l][None, :],
        gdn_conv_w=P["gdn_conv_w"][l],
        gdn_norm_g=P["gdn_norm_g"][l][None, :],
        w_proj_a=P["w_proj_a"][l].astype(BF16),
        w_proj_b=P["w_proj_b"][l].astype(BF16),
        w_out=P["w_out"][l].astype(BF16),
        norm_ffn_g=P["norm_ffn_g"][l][None, :],
        w_up=P["w_up"][l].astype(BF16),
        ffn_conv_w=P["ffn_conv_w"][l],
        ffn_conv_b=P["ffn_conv_b"][l][None, :],
        w_down=P["w_down"][l].astype(BF16),
        final_g=P["final_norm_g"][None, :],
    )


def _layer(x, l, depth, kv_prev, W, n_seq, rows_per_seq, cache_k, cache_v, s0, gbuf, fbuf,
           final, tm, tq, tk):
    rows, d = x.shape
    tm = min(tm, rows)
    aw = A_HEADS * A_VDIM
    nqkv = B_HEADS * (2 * B_DK + B_DV)
    lam0 = _lambda_init(l)
    slopes = (2.0 ** (-8.0 * jnp.arange(1, A_HEADS + 1, dtype=F32) / A_HEADS)) * LOG2E
    prompt = cache_k is None

    v_kinds = ("h", "t") if prompt else ("h", BF16)
    q, k, kb, v, vx = _norm_proj(
        x, W["norm_mix_g"], W["w_a"],
        ((0, aw, (BF16,)), (aw, aw, ("h", BF16)), (2 * aw, aw, v_kinds)), tm if not prompt else tk,
        layer=l, depth=depth, carried=kv_prev)
    qkvb, ba = _norm_proj(x, W["norm_mix_g"], W["w_b"],
                          ((0, nqkv, (F32,)), (nqkv, LANES, (F32,))), tm)
    z, ga, gb = _norm_proj(x, W["norm_mix_g"], W["w_c"],
                           ((0, aw, (F32,)), (aw, d, (F32,)), (aw + d, d, (F32,))), tm)

    if prompt:
        o_a = _attn_prompt(q, kb, vx, W["lam_rows"], W["subln_g"], slopes, lam0, tq, tk)
    else:
        o_a = _attn_sample(q, kb, vx, cache_k, cache_v, l, W["lam_rows"], W["subln_g"],
                           slopes, lam0, rows_per_seq)

    o_b, s_new, gbuf_new = _gdn(qkvb, ba, z, s0, gbuf, W["gdn_conv_w"], W["par"],
                                W["gdn_norm_g"], n_seq, rows_per_seq,
                                min(GDN_ROWS_PER_STEP, rows_per_seq))

    h = _merge(x, o_a, o_b, ga, gb, W["w_proj_a"], W["w_proj_b"], W["w_out"], tm)
    d_ff = W["w_down"].shape[0]
    y, fbuf_new = _ffn(h, W["norm_ffn_g"], fbuf, W["w_up"], W["ffn_conv_w"], W["ffn_conv_b"],
                       W["w_down"], W["final_g"], n_seq, rows_per_seq,
                       min(FFN_ROW_TILE, rows_per_seq), d_ff // FFN_COLUMN_CHUNKS, final)
    return y, k, v, s_new, gbuf_new, fbuf_new


def kernel(x_prompt, x_sample, cache_k, cache_v, state_gdn, state_gdn_conv, state_ffn_conv, norm_mix_g, w_in, lambda_q1, lambda_k1, lambda_q2, lambda_k2, subln_g, gdn_conv_w, gdn_a_log, gdn_dt_bias, gdn_norm_g, w_proj_a, w_proj_b, w_out, norm_ffn_g, w_up, ffn_conv_w, ffn_conv_b, w_down, final_norm_g):
    P = dict(norm_mix_g=norm_mix_g, w_in=w_in, lambda_q1=lambda_q1, lambda_k1=lambda_k1,
             lambda_q2=lambda_q2, lambda_k2=lambda_k2, subln_g=subln_g, gdn_conv_w=gdn_conv_w,
             gdn_a_log=gdn_a_log, gdn_dt_bias=gdn_dt_bias, gdn_norm_g=gdn_norm_g,
             w_proj_a=w_proj_a, w_proj_b=w_proj_b, w_out=w_out, norm_ffn_g=norm_ffn_g,
             w_up=w_up, ffn_conv_w=ffn_conv_w, ffn_conv_b=ffn_conv_b, w_down=w_down,
             final_norm_g=final_norm_g)
    depth = w_in.shape[0]
    bp, tp, d = x_prompt.shape
    bs, ts, _ = x_sample.shape
    past = cache_k.shape[2]
    assert ts == CHUNK and past % CHUNK == 0 and tp % CHUNK == 0
    nqkv = state_gdn_conv.shape[-1]
    d_ff2 = state_ffn_conv.shape[-1]
    weights = [_prep_layer_weights(l, P) for l in range(depth)]
    ck = cache_k.reshape(depth, bs, past * A_HEADS, A_VDIM)
    cv = cache_v.reshape(depth, bs, past * A_HEADS, A_VDIM)

    tk = min(ATTN_KEY_TILE, tp)
    tq = min(ATTN_QUERY_BLOCK, tp)
    tm = 512

    def run(x3, nseq, rps, sample):
        assert sample or nseq == 1
        x = x3.reshape(nseq * rps, d)
        kv = tuple(jnp.zeros((depth, nseq * rps, A_HEADS, A_VDIM), F32) for _ in range(2))
        ss, gcs, fcs = [], [], []
        for l in range(depth):
            if sample:
                s0, gbuf, fbuf = state_gdn[l], state_gdn_conv[l], state_ffn_conv[l]
                c_k, c_v = ck, cv
            else:
                s0 = jnp.zeros((nseq, B_HEADS, B_DK, B_DV), F32)
                gbuf = jnp.zeros((nseq, GDN_CONV - 1, nqkv), F32)
                fbuf = jnp.zeros((nseq, FFN_CONV - 1, d_ff2), F32)
                c_k = c_v = None
            x, k_all, v_all, s_new, g_new, f_new = _layer(
                x, l, depth, kv, weights[l], nseq, rps, c_k, c_v, s0, gbuf, fbuf,
                l == depth - 1, tm, tq, tk)
            kv = (k_all, v_all)
            ss.append(s_new)
            gcs.append(g_new)
            fcs.append(f_new)
        kv_shape = (depth, nseq, rps, A_HEADS, A_VDIM)
        return (x.reshape(nseq, rps, d), kv[0].reshape(kv_shape), kv[1].reshape(kv_shape),
                jnp.stack(ss), jnp.stack(gcs), jnp.stack(fcs))

    y_p, k_p, v_p, s_p, gc_p, fc_p = run(x_prompt, bp, tp, False)
    y_s, k_s, v_s, s_s, gc_s, fc_s = run(x_sample, bs, ts, True)
    return (y_p, y_s, k_p, v_p, s_p, gc_p, fc_p, k_s, v_s, s_s, gc_s, fc_s)
```

```python
import functools
import math

import jax
import jax.numpy as jnp
from jax import lax
from jax.experimental import pallas as pl
from jax.experimental.pallas import tpu as pltpu

F32 = jnp.float32
BF16 = jnp.bfloat16

EPS = 1e-6
NEG = -1e30
LOG2E = 1.4426950408889634

CHUNK = 64
A_HEADS = 8
A_HEAD_DIM = 64
A_VDIM = 128
B_HEADS = 8
B_DK = 128
B_DV = 128
GDN_CONV = 4
FFN_CONV = 3
LANES = 128
HIST = 8

VMEM_LIMIT = 56 * 1024 * 1024


def _cparams(n_axes):
    return pltpu.CompilerParams(
        dimension_semantics=("arbitrary",) * n_axes,
        vmem_limit_bytes=VMEM_LIMIT)


def _lambda_init(layer):
    return 0.8 - 0.6 * math.exp(-0.3 * layer)


def _norm_proj_kernel(x_ref, g_ref, w_ref, *refs, segs, n_carried):
    out_refs = refs[n_carried:]
    x = x_ref[...]
    ms = jnp.mean(x * x, axis=-1, keepdims=True)
    xn = (x * lax.rsqrt(ms + EPS) * g_ref[...]).astype(BF16)
    k = 0
    for start, width, kinds in segs:
        w = w_ref[:, start:start + width]
        r = None
        for kind in kinds:
            o_ref = out_refs[k]
            k += 1
            if r is None:
                r = jnp.dot(xn, w, preferred_element_type=F32)
            if kind == "t":
                o_ref[...] = r.T.astype(o_ref.dtype)
            elif kind == "h":
                for hd in range(width // LANES):
                    o_ref[:, hd, :] = r[:, hd * LANES:(hd + 1) * LANES]
            else:
                o_ref[...] = r.astype(o_ref.dtype)


def _norm_proj(x, g, w, segs, tm, layer=0, depth=1, carried=()):
    rows, d = x.shape
    assert rows % tm == 0
    out_shape, out_specs, ksegs, h_outs = [], [], [], []
    for start, width, kinds in segs:
        kk = []
        for kind in kinds:
            if kind == "t":
                out_shape.append(jax.ShapeDtypeStruct((rows // tm, width, tm), BF16))
                out_specs.append(pl.BlockSpec((None, width, tm), lambda i: (i, 0, 0)))
                kk.append("t")
            elif kind == "h":
                h_outs.append(len(out_shape))
                out_shape.append(jax.ShapeDtypeStruct(
                    (depth, rows, width // LANES, LANES), F32))
                out_specs.append(pl.BlockSpec((None, tm, width // LANES, LANES),
                                              lambda i: (layer, i, 0, 0)))
                kk.append("h")
            else:
                out_shape.append(jax.ShapeDtypeStruct((rows, width), kind))
                out_specs.append(pl.BlockSpec((tm, width), lambda i: (i, 0)))
                kk.append("n")
        ksegs.append((start, width, tuple(kk)))
    assert len(carried) in (0, len(h_outs))
    return pl.pallas_call(
        functools.partial(_norm_proj_kernel, segs=tuple(ksegs), n_carried=len(carried)),
        grid=(rows // tm,),
        in_specs=[pl.BlockSpec((tm, d), lambda i: (i, 0)),
                  pl.BlockSpec((1, d), lambda i: (0, 0)),
                  pl.BlockSpec(w.shape, lambda i: (0, 0))]
                 + [pl.BlockSpec(memory_space=pl.ANY)] * len(carried),
        out_specs=out_specs,
        out_shape=out_shape,
        input_output_aliases={3 + n: h_outs[n] for n in range(len(carried))},
        compiler_params=_cparams(1),
        name="norm_proj",
    )(x, g, w, *carried)


def _lambda_value(lam_ref, lam0):
    a = jnp.sum(lam_ref[0:1, :] * lam_ref[1:2, :], axis=-1, keepdims=True)
    b = jnp.sum(lam_ref[2:3, :] * lam_ref[3:4, :], axis=-1, keepdims=True)
    return jnp.exp(a) - jnp.exp(b) + lam0


def _stack_maps(q):
    lane = lax.broadcasted_iota(jnp.int32, q.shape, 1)
    zero = jnp.zeros_like(q)
    return jnp.concatenate([jnp.where(lane < A_HEAD_DIM, q, zero),
                            jnp.where(lane >= A_HEAD_DIM, q, zero)], axis=0)


NT_DIMS = (((1,), (1,)), ((), ()))
ATTN_KEY_TILE = 512
ATTN_QUERY_BLOCK = 1024


def _attn_prompt_kernel(slopes_ref, lam_ref, g_ref, q_ref, k_ref, vt_ref, o_ref,
                        qq_sc, kaug_sc, dfix_sc, sa_sc, sb_sc, m_sc, l_sc, acc_sc,
                        *, tq, tk, lam0):
    h = pl.program_id(0)
    i = pl.program_id(1)
    slope = slopes_ref[h]
    q0 = i * tq
    nt = NT_DIMS
    w = tk
    nd = tq // tk
    n_off = i * nd
    n_strips = 2 * tq // w

    q_t = q_ref[...].astype(F32).T
    feat = lax.broadcasted_iota(jnp.int32, q_t.shape, 0)
    qq_sc[:A_VDIM, :] = jnp.concatenate(
        [jnp.where(feat < A_HEAD_DIM, q_t, 0.0), jnp.where(feat >= A_HEAD_DIM, q_t, 0.0)],
        axis=1).astype(BF16)
    row_q = lax.broadcasted_iota(jnp.int32, (LANES, 2 * tq), 0)
    qq_sc[A_VDIM:, :] = jnp.where(row_q < 3, 1.0, 0.0).astype(BF16)

    @pl.when(i == 0)
    def _():
        b = slope * lax.broadcasted_iota(jnp.int32, (tk, LANES), 0).astype(F32)
        b1 = b.astype(BF16).astype(F32)
        b2 = (b - b1).astype(BF16).astype(F32)
        b3 = b - b1 - b2
        lane_k = lax.broadcasted_iota(jnp.int32, (tk, LANES), 1)
        kaug_sc[...] = jnp.where(lane_k == 0, b1, jnp.where(
            lane_k == 1, b2, jnp.where(lane_k == 2, b3, 0.0))).astype(BF16)
        c = lax.broadcasted_iota(jnp.int32, (tk, w), 0)
        r = lax.broadcasted_iota(jnp.int32, (tk, w), 1)
        fix = slope * jnp.minimum(2 * (r - c), 0).astype(F32)
        dfix_sc[...] = jnp.where((c // CHUNK) <= (r // CHUNK), fix, NEG)

    m_sc[...] = jnp.full(m_sc.shape, NEG, F32)
    l_sc[...] = jnp.zeros(l_sc.shape, F32)
    acc_sc[...] = jnp.zeros(acc_sc.shape, F32)

    groups = 8

    def key_reduce(op, x):
        part = op(x.reshape(groups, tk // groups, x.shape[-1]), axis=0)
        return op(part, axis=0, keepdims=True)

    def softmax_update(st, s_ref, kappa, vtt):
        sl = slice(st * w, (st + 1) * w)
        m_prev = m_sc[:, sl]
        m_new = jnp.maximum(m_prev, key_reduce(jnp.max, s_ref[:, sl]) + kappa)
        alpha = jnp.exp2(m_prev - m_new)
        p = jnp.exp2(s_ref[:, sl] - (m_new - kappa))
        pv = jnp.dot(jnp.concatenate([vtt, jnp.ones((16, tk), BF16)], axis=0), p.astype(BF16),
                     preferred_element_type=F32)
        l_sc[:, sl] = alpha * l_sc[:, sl] + pv[A_VDIM:A_VDIM + 1]
        acc_sc[:, sl] = alpha * acc_sc[:, sl] + pv[:A_VDIM]
        m_sc[:, sl] = m_new

    def produce(j, s_ref, strips=None):
        k0 = pl.multiple_of(j * tk, tk)
        lhs = jnp.concatenate([k_ref[pl.ds(k0, tk), :], kaug_sc[...]], axis=1)
        if strips is None:
            s_ref[...] = jnp.dot(lhs, qq_sc[...], preferred_element_type=F32)
        else:
            for st in strips:
                sl = slice(st * w, (st + 1) * w)
                s_ref[:, sl] = jnp.dot(lhs, qq_sc[:, sl], preferred_element_type=F32)

    def visible_strips(d):
        return [st for st in range(n_strips) if (st * w) % tq >= d * tk]

    def consume(j, s_ref):
        kappa = slope * (j * tk - q0).astype(F32)
        vtj = vt_ref[j]
        for st in range(n_strips):
            softmax_update(st, s_ref, kappa, vtj)

    def consume_overlap(d, s_ref):
        kappa = slope * (d * tk)
        vtd = vt_ref[n_off + d]
        for st in visible_strips(d):
            if (st * w) % tq == d * tk:
                sl = slice(st * w, (st + 1) * w)
                s_ref[:, sl] = s_ref[:, sl] + dfix_sc[...]
            softmax_update(st, s_ref, kappa, vtd)

    def overlap_tiles(first_ref, second_ref):
        for d in range(nd):
            cur, nxt = (first_ref, second_ref) if d % 2 == 0 else (second_ref, first_ref)
            if d + 1 < nd:
                produce(n_off + d + 1, nxt, visible_strips(d + 1))
            consume_overlap(d, cur)

    produce(0, sa_sc)

    def produce_consume(jp, p_ref, jc, c_ref):
        kappa = slope * (jc * tk - q0).astype(F32)
        vtj = vt_ref[jc]
        for st in range(n_strips):
            produce(jp, p_ref, [st])
            softmax_update(st, c_ref, kappa, vtj)

    def two_tiles(j):
        produce_consume(j + 1, sb_sc, j, sa_sc)
        produce_consume(j + 2, sa_sc, j + 1, sb_sc)

    def four_tiles(t, carry):
        two_tiles(4 * t)
        two_tiles(4 * t + 2)
        return carry

    lax.fori_loop(0, n_off // 4, four_tiles, 0)

    @pl.when(n_off % 4 >= 2)
    def _():
        two_tiles((n_off // 4) * 4)

    if nd % 2 == 0:
        overlap_tiles(sa_sc, sb_sc)
    else:
        @pl.when(n_off % 2 == 0)
        def _():
            overlap_tiles(sa_sc, sb_sc)

        @pl.when(n_off % 2 == 1)
        def _():
            produce(n_off, sb_sc)
            consume(n_off - 1, sa_sc)
            overlap_tiles(sb_sc, sa_sc)

    lam = _lambda_value(lam_ref, lam0)
    inv_l = 1.0 / l_sc[...]
    acc = acc_sc[...] * inv_l
    o = acc[:, :tq] - lam * acc[:, tq:]
    ms = jnp.mean(o * o, axis=0, keepdims=True)
    o = (o * lax.rsqrt(ms + EPS)).T
    o_ref[...] = (o * g_ref[...] * (1.0 - lam0)).astype(o_ref.dtype)


def _attn_prompt(q, kb, vt, lam_rows, subln_g, slopes, lam0, tq, tk):
    t = q.shape[0]
    assert t % tq == 0 and tq % tk == 0
    return pl.pallas_call(
        functools.partial(_attn_prompt_kernel, tq=tq, tk=tk, lam0=lam0),
        grid=(A_HEADS, t // tq),
        in_specs=[pl.BlockSpec(memory_space=pltpu.SMEM),
                  pl.BlockSpec((8, LANES), lambda h, i: (0, 0)),
                  pl.BlockSpec((1, A_VDIM), lambda h, i: (0, 0)),
                  pl.BlockSpec((tq, A_VDIM), lambda h, i: (i, h)),
                  pl.BlockSpec((t, A_VDIM), lambda h, i: (0, h)),
                  pl.BlockSpec((t // tk, A_VDIM, tk), lambda h, i: (0, h, 0))],
        out_specs=pl.BlockSpec((tq, A_VDIM), lambda h, i: (i, h)),
        out_shape=jax.ShapeDtypeStruct((t, A_HEADS * A_VDIM), BF16),
        scratch_shapes=[pltpu.VMEM((2 * LANES, 2 * tq), BF16),
                        pltpu.VMEM((tk, LANES), BF16),
                        pltpu.VMEM((tk, tk), F32),
                        pltpu.VMEM((tk, 2 * tq), F32),
                        pltpu.VMEM((tk, 2 * tq), F32),
                        pltpu.VMEM((1, 2 * tq), F32),
                        pltpu.VMEM((1, 2 * tq), F32),
                        pltpu.VMEM((A_VDIM, 2 * tq), F32)],
        compiler_params=_cparams(2),
        name="attn_prompt",
    )(slopes, lam_rows, subln_g, q, kb, vt)


SAMPLE_KEY_TILE = 2048


def _attn_sample_kernel(slopes_ref, lam_ref, g_ref, q_ref, kn_ref, vn_ref, kc_ref, vc_ref,
                        o_ref, m_sc, l_sc, acc_sc, *, past, lam0):
    t = pl.program_id(1)
    nt = pl.num_programs(1)
    tq = q_ref.shape[0]
    tk = kc_ref.shape[0] // A_HEADS

    @pl.when(t == 0)
    def _():
        m_sc[...] = jnp.full(m_sc.shape, NEG, F32)
        l_sc[...] = jnp.zeros(l_sc.shape, F32)
        acc_sc[...] = jnp.zeros(acc_sc.shape, F32)

    def update(h, s, v):
        m_prev = m_sc[h]
        m_new = jnp.maximum(m_prev, jnp.max(s, axis=-1, keepdims=True))
        alpha = jnp.exp2(m_prev - m_new)
        p = jnp.exp2(s - m_new)
        l_sc[h] = alpha * l_sc[h] + jnp.sum(p, axis=-1, keepdims=True)
        acc_sc[h] = alpha * acc_sc[h] + jnp.dot(p.astype(BF16), v, preferred_element_type=F32)
        m_sc[h] = m_new

    kpos = t * tk + lax.broadcasted_iota(jnp.int32, (1, tk), 1)
    rel = (kpos - past).astype(F32)
    heads = range(A_HEADS)
    qqs = [_stack_maps(q_ref[:, h * A_VDIM:(h + 1) * A_VDIM]) for h in heads]
    scores = [lax.dot_general(qqs[h], kc_ref[pl.ds(h, tk, stride=A_HEADS), :].astype(BF16),
                              NT_DIMS, preferred_element_type=F32) for h in heads]
    m_prev = [m_sc[h] for h in heads]
    m_new, alpha, p = [], [], []
    for h in heads:
        s = scores[h] + slopes_ref[h] * rel
        m_new.append(jnp.maximum(m_prev[h], jnp.max(s, axis=-1, keepdims=True)))
        alpha.append(jnp.exp2(m_prev[h] - m_new[h]))
        p.append(jnp.exp2(s - m_new[h]))
    pv = [jnp.dot(p[h].astype(BF16), vc_ref[pl.ds(h, tk, stride=A_HEADS), :].astype(BF16),
                  preferred_element_type=F32) for h in heads]
    for h in heads:
        l_sc[h] = alpha[h] * l_sc[h] + jnp.sum(p[h], axis=-1, keepdims=True)
        acc_sc[h] = alpha[h] * acc_sc[h] + pv[h]
        m_sc[h] = m_new[h]

    @pl.when(t == nt - 1)
    def _():
        lam = _lambda_value(lam_ref, lam0)
        r = lax.broadcasted_iota(jnp.int32, (2 * tq, tq), 0)
        c = lax.broadcasted_iota(jnp.int32, (2 * tq, tq), 1)
        r = jnp.where(r >= tq, r - tq, r)
        own = jnp.minimum(c, 2 * r - c).astype(F32)
        for h in range(A_HEADS):
            cols = slice(h * A_VDIM, (h + 1) * A_VDIM)
            s = lax.dot_general(qqs[h], kn_ref[:, cols], NT_DIMS, preferred_element_type=F32)
            update(h, s + slopes_ref[h] * own, vn_ref[:, cols])
            acc = acc_sc[h] * (1.0 / l_sc[h])
            o = acc[:tq] - lam * acc[tq:]
            ms = jnp.mean(o * o, axis=-1, keepdims=True)
            o_ref[:, cols] = (o * lax.rsqrt(ms + EPS) * g_ref[...]
                              * (1.0 - lam0)).astype(o_ref.dtype)


def _attn_sample(q, kb, vb, cache_k, cache_v, layer, lam_rows, subln_g, slopes, lam0, tq):
    nb, past = cache_k.shape[1], cache_k.shape[2] // A_HEADS
    tk = min(SAMPLE_KEY_TILE, past)
    assert past % tk == 0
    aw = A_HEADS * A_VDIM
    rows = pl.BlockSpec((tq, aw), lambda b, t: (b, 0))
    cache = pl.BlockSpec((None, None, tk * A_HEADS, A_VDIM), lambda b, t: (layer, b, t, 0))
    return pl.pallas_call(
        functools.partial(_attn_sample_kernel, past=past, lam0=lam0),
        grid=(nb, past // tk),
        in_specs=[pl.BlockSpec(memory_space=pltpu.SMEM),
                  pl.BlockSpec((8, LANES), lambda b, t: (0, 0)),
                  pl.BlockSpec((1, A_VDIM), lambda b, t: (0, 0)),
                  rows, rows, rows, cache, cache],
        out_specs=rows,
        out_shape=jax.ShapeDtypeStruct((nb * tq, aw), BF16),
        scratch_shapes=[pltpu.VMEM((A_HEADS, 2 * tq, 1), F32),
                        pltpu.VMEM((A_HEADS, 2 * tq, 1), F32),
                        pltpu.VMEM((A_HEADS, 2 * tq, A_VDIM), F32)],
        compiler_params=_cparams(2),
        name="attn_sample",
    )(slopes, lam_rows, subln_g, q, kb, vb, cache_k, cache_v)


_HI = lax.Precision.HIGHEST
GDN_ROWS_PER_STEP = 8 * CHUNK


def _dot(a, b):
    return jnp.dot(a, b, preferred_element_type=F32)


def _split(a):
    hi = a.astype(BF16)
    return hi, (a - hi.astype(F32)).astype(BF16)


def _block_diag2(a, left):
    zero = jnp.zeros_like(a)
    return jnp.concatenate([jnp.where(left, a, zero), jnp.where(left, zero, a)], axis=0)


def _pair_prod(a_parts, b_parts, left):
    ah, bh = a_parts[0], b_parts[0]
    bdh = _block_diag2(bh, left)
    if len(a_parts) == 1 and len(b_parts) == 1:
        return _dot(ah, bdh)
    if len(a_parts) == 1:
        return _dot(jnp.concatenate([ah, ah], axis=1),
                    jnp.concatenate([bdh, _block_diag2(b_parts[1], left)], axis=0))
    r = _dot(jnp.concatenate([ah, a_parts[1]], axis=1), jnp.concatenate([bdh, bdh], axis=0))
    if len(b_parts) == 1:
        return r
    return r + _dot(ah, _block_diag2(b_parts[1], left))


def _inverse_masks(ri, cj):
    n = ri.shape[0]
    masks = [(ri // 2) == (cj // 2)]
    b = 2
    while b < n:
        same_big = ((ri // (2 * b)) == (cj // (2 * b))).astype(jnp.int32)
        same_small = ((ri // b) == (cj // b)).astype(jnp.int32)
        masks.append((same_big - same_small) > 0)
        b *= 2
    return masks


def _unit_lower_inverse(lows, masks, eye, left):
    xs = [eye - jnp.where(masks[0], low, 0.0) for low in lows]
    for mask in masks[1:]:
        xparts = [_split(x) for x in xs]
        ys = [_pair_prod((jnp.where(mask, low, 0.0).astype(BF16),), xp, left)
              for low, xp in zip(lows, xparts)]
        xs = [x - _pair_prod(xp, _split(y), left) for x, xp, y in zip(xs, xparts, ys)]
    return xs


def _gdn_kernel(x_ref, buf_ref, ba_ref, z_ref, s0_ref, cw_ref, par_ref, ng_ref,
                o_ref, s_out_ref, buf_out_ref, xp_sc, s_sc):
    c_idx = pl.program_id(1)
    nc = pl.num_programs(1)
    R = x_ref.shape[0]
    C = CHUNK
    n_sub = R // C
    nq = B_HEADS * B_DK
    cat = jnp.concatenate

    @pl.when(c_idx == 0)
    def _():
        xp_sc[HIST - (GDN_CONV - 1):HIST, :] = buf_ref[...]
        s_sc[...] = s0_ref[...].astype(F32)

    xp_sc[HIST:HIST + R, :] = x_ref[...]
    xp = xp_sc[...]
    conv = xp[HIST:] * cw_ref[GDN_CONV - 1:GDN_CONV, :]
    for j in range(GDN_CONV - 1):
        shifted = pltpu.roll(xp, GDN_CONV - 1 - j, axis=0)
        conv = conv + shifted[HIST:] * cw_ref[j:j + 1, :]
    conv = conv * jax.nn.sigmoid(conv)
    tail = xp_sc[R:R + HIST, :]
    xp_sc[0:HIST, :] = tail
    buf_out_ref[...] = tail[HIST - (GDN_CONV - 1):HIST, :]

    ba = ba_ref[...]
    lane = lax.broadcasted_iota(jnp.int32, ba.shape, 1)
    beta_all = jax.nn.sigmoid(ba)
    g_all = jnp.where((lane >= B_HEADS) & (lane < 2 * B_HEADS),
                      -jnp.exp(par_ref[0:1, :]) * jax.nn.softplus(ba + par_ref[1:2, :]), 0.0)
    rr = lax.broadcasted_iota(jnp.int32, (R, R), 0)
    cc = lax.broadcasted_iota(jnp.int32, (R, R), 1)
    same_chunk = ((rr // C) - (cc // C)) == 0
    tri = jnp.where(same_chunk, jnp.where(rr >= cc, 1.0, 0.0), 0.0).astype(F32)
    gcum_all = jnp.dot(tri, g_all, precision=_HI, preferred_element_type=F32)

    ri = lax.broadcasted_iota(jnp.int32, (C, 2 * C), 0)
    cl = lax.broadcasted_iota(jnp.int32, (C, 2 * C), 1)
    left = cl < C
    cj = jnp.where(left, cl, cl - C)
    incl = ri >= cj
    strict = ri > cj
    eye = jnp.where(ri == cj, 1.0, 0.0).astype(F32)
    inv_masks = _inverse_masks(ri, cj)
    zk = jnp.zeros((C, B_DK), F32)
    z2 = jnp.zeros((C, B_DV + B_DK), F32)
    zv = jnp.zeros((C, B_DV), F32)

    subs = range(n_sub)
    pairs = range(B_HEADS // 2)
    q, k, v, kb, beta, gc, eg, g_last, gcum_t = [], [], [], [], [], [], [], [], []
    for c in subs:
        rows = slice(c * C, (c + 1) * C)
        gsub = gcum_all[rows]
        gcum_t.append(cat([gsub, gsub], axis=0).T)
        qc, kc, vc, kbc, bc, gcc, egc = [], [], [], [], [], [], []
        for h in range(B_HEADS):
            qh = conv[rows, h * B_DK:(h + 1) * B_DK]
            kh = conv[rows, nq + h * B_DK:nq + (h + 1) * B_DK]
            qc.append(qh * lax.rsqrt(jnp.sum(qh * qh, axis=-1, keepdims=True) + EPS)
                      * (B_DK ** -0.5))
            kc.append(kh * lax.rsqrt(jnp.sum(kh * kh, axis=-1, keepdims=True) + EPS))
            vc.append(conv[rows, 2 * nq + h * B_DV:2 * nq + (h + 1) * B_DV])
            bc.append(beta_all[rows, h:h + 1])
            gcc.append(gsub[:, B_HEADS + h:B_HEADS + h + 1])
            kbc.append(kc[-1] * bc[-1])
            egc.append(jnp.exp(gcc[-1]))
        q.append(qc)
        k.append(kc)
        v.append(vc)
        kb.append(kbc)
        beta.append(bc)
        gc.append(gcc)
        eg.append(egc)
        g_last.append([g[C - 1:C, :] for g in gcc])

    lows, attns = [], []
    for c in subs:
        for p in pairs:
            a, b = 2 * p, 2 * p + 1
            g_col = jnp.where(left, gc[c][a], gc[c][b])
            g_row = jnp.where(left[0:1], gcum_t[c][B_HEADS + a:B_HEADS + a + 1, :],
                              gcum_t[c][B_HEADS + b:B_HEADS + b + 1, :])
            decay = jnp.where(incl, jnp.exp(jnp.where(incl, g_col - g_row, 0.0)), 0.0)
            lhs = cat([cat([kb[c][a], kb[c][b]], axis=1), cat([q[c][a], q[c][b]], axis=1)],
                      axis=0)
            rhs = cat([cat([k[c][a], zk], axis=1), cat([zk, k[c][b]], axis=1)], axis=0)
            m1 = lax.dot_general(lhs.astype(BF16), rhs.astype(BF16), NT_DIMS,
                                 preferred_element_type=F32)
            lows.append(jnp.where(strict, m1[:C] * decay, 0.0).astype(BF16).astype(F32))
            attns.append(m1[C:] * decay)

    tinvs = _unit_lower_inverse(lows, inv_masks, eye, left)

    sols = []
    for c in subs:
        for p in pairs:
            a, b = 2 * p, 2 * p + 1
            rhs_a = cat([v[c][a] * beta[c][a], kb[c][a] * eg[c][a]], axis=1)
            rhs_b = cat([v[c][b] * beta[c][b], kb[c][b] * eg[c][b]], axis=1)
            rhs_bd = cat([cat([rhs_a, z2], axis=1), cat([z2, rhs_b], axis=1)], axis=0)
            sols.append(_dot(tinvs[c * len(pairs) + p].astype(BF16), rhs_bd.astype(BF16)))

    s_cur = [s_sc[h] for h in range(B_HEADS)]
    for c in subs:
        rows = slice(c * C, (c + 1) * C)
        v_new, q_s = [], []
        for h in range(B_HEADS):
            sol = sols[c * len(pairs) + h // 2]
            base = (h % 2) * (B_DV + B_DK)
            u = sol[:, base:base + B_DV]
            w = sol[:, base + B_DV:base + B_DV + B_DK]
            m2 = _dot(cat([w, q[c][h] * eg[c][h]], axis=0).astype(BF16), s_cur[h].astype(BF16))
            v_new.append(u - m2[:C])
            q_s.append(m2[C:])
        o_pairs, upds = [], []
        for p in pairs:
            a, b = 2 * p, 2 * p + 1
            vn_bd = cat([cat([v_new[a], zv], axis=1), cat([zv, v_new[b]], axis=1)],
                        axis=0).astype(BF16)
            o_pairs.append(_dot(attns[c * len(pairs) + p].astype(BF16), vn_bd))
            kd_t = cat([k[c][a] * jnp.exp(g_last[c][a] - gc[c][a]),
                        k[c][b] * jnp.exp(g_last[c][b] - gc[c][b])], axis=0).T
            upds.append(_dot(kd_t.astype(BF16), vn_bd))
        for h in range(B_HEADS):
            lo = (h % 2) * B_DV
            s_cur[h] = s_cur[h] * jnp.exp(g_last[c][h]) + upds[h // 2][:, lo:lo + B_DV]
            o = q_s[h] + o_pairs[h // 2][:, lo:lo + B_DV]
            zh = z_ref[rows, h * B_DV:(h + 1) * B_DV]
            on = o * lax.rsqrt(jnp.mean(o * o, axis=-1, keepdims=True) + EPS) * ng_ref[...]
            o_ref[rows, h * B_DV:(h + 1) * B_DV] = (
                on * (zh * jax.nn.sigmoid(zh))).astype(o_ref.dtype)
    for h in range(B_HEADS):
        s_sc[h] = s_cur[h]

    @pl.when(c_idx == nc - 1)
    def _():
        s_out_ref[...] = s_sc[...].astype(s_out_ref.dtype)


def _gdn(qkv, ba, z, s0, buf, conv_w, par, norm_g, n_seq, rows_per_seq, rows_per_step):
    assert rows_per_seq % rows_per_step == 0 and rows_per_step % CHUNK == 0
    nc = rows_per_seq // rows_per_step
    nqkv = qkv.shape[1]
    row = lambda b, c: (b * nc + c, 0)
    return pl.pallas_call(
        _gdn_kernel,
        grid=(n_seq, nc),
        in_specs=[pl.BlockSpec((rows_per_step, nqkv), row),
                  pl.BlockSpec((None, GDN_CONV - 1, nqkv), lambda b, c: (b, 0, 0)),
                  pl.BlockSpec((rows_per_step, LANES), row),
                  pl.BlockSpec((rows_per_step, B_HEADS * B_DV), row),
                  pl.BlockSpec((None, B_HEADS, B_DK, B_DV), lambda b, c: (b, 0, 0, 0)),
                  pl.BlockSpec((GDN_CONV, nqkv), lambda b, c: (0, 0)),
                  pl.BlockSpec((8, LANES), lambda b, c: (0, 0)),
                  pl.BlockSpec((1, B_DV), lambda b, c: (0, 0))],
        out_specs=[pl.BlockSpec((rows_per_step, B_HEADS * B_DV), row),
                   pl.BlockSpec((None, B_HEADS, B_DK, B_DV), lambda b, c: (b, 0, 0, 0)),
                   pl.BlockSpec((None, GDN_CONV - 1, nqkv), lambda b, c: (b, 0, 0))],
        out_shape=[jax.ShapeDtypeStruct((n_seq * rows_per_seq, B_HEADS * B_DV), BF16),
                   jax.ShapeDtypeStruct((n_seq, B_HEADS, B_DK, B_DV), F32),
                   jax.ShapeDtypeStruct((n_seq, GDN_CONV - 1, nqkv), F32)],
        scratch_shapes=[pltpu.VMEM((HIST + rows_per_step, nqkv), F32),
                        pltpu.VMEM((B_HEADS, B_DK, B_DV), F32)],
        compiler_params=_cparams(2),
        name="gdn",
    )(qkv, buf, ba, z, s0, conv_w, par, norm_g)


def _merge_kernel(x_ref, oa_ref, ob_ref, ga_ref, gb_ref, wa_ref, wb_ref, wo_ref, h_ref):
    a = jnp.dot(oa_ref[...], wa_ref[...], preferred_element_type=F32)
    b = jnp.dot(ob_ref[...], wb_ref[...], preferred_element_type=F32)
    merged = jax.nn.sigmoid(ga_ref[...]) * a + jax.nn.sigmoid(gb_ref[...]) * b
    h_ref[...] = x_ref[...] + jnp.dot(merged.astype(BF16), wo_ref[...],
                                      preferred_element_type=F32)


def _merge(x, oa, ob, ga, gb, wa, wb, wo, tm):
    rows, d = x.shape
    rowspec = pl.BlockSpec((tm, d), lambda i: (i, 0))
    wspec = pl.BlockSpec((d, d), lambda i: (0, 0))
    return pl.pallas_call(
        _merge_kernel,
        grid=(rows // tm,),
        in_specs=[rowspec] * 5 + [wspec] * 3,
        out_specs=rowspec,
        out_shape=jax.ShapeDtypeStruct((rows, d), F32),
        compiler_params=_cparams(1),
        name="merge",
    )(x, oa, ob, ga, gb, wa, wb, wo)


FFN_ROW_TILE = 512
FFN_COLUMN_CHUNKS = 1


def _ffn_kernel(h_ref, g_ref, buf_g_ref, buf_v_ref, wg_ref, wv_ref, cwg_ref, cwv_ref,
                cbg_ref, cbv_ref, wd_ref, fg_ref, y_ref, nbuf_g_ref, nbuf_v_ref,
                hn_sc, ug_sc, uv_sc, acc_sc, *, tiles_per_seq, final):
    i = pl.program_id(0)
    j = pl.program_id(1)
    nj = pl.num_programs(1)
    tm = h_ref.shape[0]
    first = (i % tiles_per_seq) == 0
    nh = FFN_CONV - 1

    @pl.when(j == 0)
    def _():
        x = h_ref[...]
        ms = jnp.mean(x * x, axis=-1, keepdims=True)
        hn_sc[...] = (x * lax.rsqrt(ms + EPS) * g_ref[...]).astype(BF16)
        acc_sc[...] = jnp.zeros(acc_sc.shape, F32)

    def conv_half(u_sc, w_ref, cw_ref, cb_ref, buf_ref, nbuf_ref):
        @pl.when(first)
        def _():
            u_sc[j, HIST - nh:HIST, :] = buf_ref[...]
        u_sc[j, HIST:HIST + tm, :] = jnp.dot(hn_sc[...], w_ref[...], preferred_element_type=F32)
        out = u_sc[j, HIST:HIST + tm, :] * cw_ref[nh:nh + 1, :] + cb_ref[...]
        for t in range(nh):
            off = HIST - nh + t
            out = out + u_sc[j, off:off + tm, :] * cw_ref[t:t + 1, :]
        tail = u_sc[j, tm:tm + HIST, :]
        u_sc[j, 0:HIST, :] = tail
        nbuf_ref[j] = tail[HIST - nh:HIST, :]
        return out

    gate = conv_half(ug_sc, wg_ref, cwg_ref, cbg_ref, buf_g_ref, nbuf_g_ref)
    val = conv_half(uv_sc, wv_ref, cwv_ref, cbv_ref, buf_v_ref, nbuf_v_ref)
    act = (gate * jax.nn.sigmoid(gate) * val).astype(BF16)
    acc_sc[...] += jnp.dot(act, wd_ref[...], preferred_element_type=F32)

    @pl.when(j == nj - 1)
    def _():
        y = h_ref[...] + acc_sc[...]
        if final:
            ms = jnp.mean(y * y, axis=-1, keepdims=True)
            y = y * lax.rsqrt(ms + EPS) * fg_ref[...]
        y_ref[...] = y


def _ffn(h, g, buf, w_up, conv_w, conv_b, w_down, final_g, n_seq, rows_per_seq, tm, cw, final):
    rows, d = h.shape
    d_ff = w_down.shape[0]
    assert d_ff % cw == 0 and rows_per_seq % tm == 0
    nj = d_ff // cw
    tps = rows_per_seq // tm
    nh = FFN_CONV - 1
    gate_col = lambda i, j: (0, j)
    val_col = lambda i, j: (0, nj + j)
    seq_gate = lambda i, j: (i // tps, 0, j)
    seq_val = lambda i, j: (i // tps, 0, nj + j)
    y, nbg, nbv = pl.pallas_call(
        functools.partial(_ffn_kernel, tiles_per_seq=tps, final=final),
        grid=(rows // tm, nj),
        in_specs=[pl.BlockSpec((tm, d), lambda i, j: (i, 0)),
                  pl.BlockSpec((1, d), lambda i, j: (0, 0)),
                  pl.BlockSpec((None, nh, cw), seq_gate),
                  pl.BlockSpec((None, nh, cw), seq_val),
                  pl.BlockSpec((d, cw), gate_col),
                  pl.BlockSpec((d, cw), val_col),
                  pl.BlockSpec((FFN_CONV, cw), gate_col),
                  pl.BlockSpec((FFN_CONV, cw), val_col),
                  pl.BlockSpec((1, cw), gate_col),
                  pl.BlockSpec((1, cw), val_col),
                  pl.BlockSpec((cw, d), lambda i, j: (j, 0)),
                  pl.BlockSpec((1, d), lambda i, j: (0, 0))],
        out_specs=[pl.BlockSpec((tm, d), lambda i, j: (i, 0)),
                   pl.BlockSpec((None, nj, nh, cw), lambda i, j: (i // tps, 0, 0, 0)),
                   pl.BlockSpec((None, nj, nh, cw), lambda i, j: (i // tps, 0, 0, 0))],
        out_shape=[jax.ShapeDtypeStruct((rows, d), F32),
                   jax.ShapeDtypeStruct((n_seq, nj, nh, cw), F32),
                   jax.ShapeDtypeStruct((n_seq, nj, nh, cw), F32)],
        scratch_shapes=[pltpu.VMEM((tm, d), BF16),
                        pltpu.VMEM((nj, HIST + tm, cw), F32),
                        pltpu.VMEM((nj, HIST + tm, cw), F32),
                        pltpu.VMEM((tm, d), F32)],
        compiler_params=_cparams(2),
        name="ffn",
    )(h, g, buf, buf, w_up, w_up, conv_w, conv_w, conv_b, conv_b, w_down, final_g)
    unchunk = lambda a: a.transpose(0, 2, 1, 3).reshape(n_seq, nh, d_ff)
    return y, jnp.concatenate([unchunk(nbg), unchunk(nbv)], axis=-1)


def _prep_layer_weights(l, P):
    w_in = P["w_in"][l]
    d = w_in.shape[0]
    aw = A_HEADS * A_VDIM
    nqkv = B_HEADS * (2 * B_DK + B_DV)
    c_qkvb = 3 * aw
    c_beta = c_qkvb + nqkv
    c_z = c_beta + 2 * B_HEADS
    bw = B_HEADS * B_DV
    w_q = w_in[:, :aw] * (A_HEAD_DIM ** -0.5 * LOG2E)
    w_a = jnp.concatenate([w_q, w_in[:, aw:3 * aw]], axis=1).astype(BF16)
    w_b = jnp.concatenate([w_in[:, c_qkvb:c_beta + 2 * B_HEADS],
                           jnp.zeros((d, LANES - 2 * B_HEADS), F32)], axis=1).astype(BF16)
    w_c = w_in[:, c_z:c_z + bw + 2 * d].astype(BF16)
    lam_rows = jnp.zeros((8, LANES), F32)
    for r, name in enumerate(("lambda_q1", "lambda_k1", "lambda_q2", "lambda_k2")):
        lam_rows = lam_rows.at[r, :A_HEAD_DIM].set(P[name][l])
    par = jnp.zeros((8, LANES), F32)
    par = par.at[0, B_HEADS:2 * B_HEADS].set(P["gdn_a_log"][l])
    par = par.at[1, B_HEADS:2 * B_HEADS].set(P["gdn_dt_bias"][l])
    return dict(
        w_a=w_a, w_b=w_b, w_c=w_c, lam_rows=lam_rows, par=par,
        norm_mix_g=P["norm_mix_g"][l][None, :],
        subln_g=P["subln_g"][l][None, :],
        gdn_conv_w=P["gdn_conv_w"][l],
        gdn_norm_g=P["gdn_norm_g"][l][None, :],
        w_proj_a=P["w_proj_a"][l].astype(BF16),
        w_proj_b=P["w_proj_b"][l].astype(BF16),
        w_out=P["w_out"][l].astype(BF16),
        norm_ffn_g=P["norm_ffn_g"][l][None, :],
        w_up=P["w_up"][l].astype(BF16),
        ffn_conv_w=P["ffn_conv_w"][l],
        ffn_conv_b=P["ffn_conv_b"][l][None, :],
        w_down=P["w_down"][l].astype(BF16),
        final_g=P["final_norm_g"][None, :],
    )


def _layer(x, l, depth, kv_prev, W, n_seq, rows_per_seq, cache_k, cache_v, s0, gbuf, fbuf,
           final, tm, tq, tk):
    rows, d = x.shape
    tm = min(tm, rows)
    aw = A_HEADS * A_VDIM
    nqkv = B_HEADS * (2 * B_DK + B_DV)
    lam0 = _lambda_init(l)
    slopes = (2.0 ** (-8.0 * jnp.arange(1, A_HEADS + 1, dtype=F32) / A_HEADS)) * LOG2E
    prompt = cache_k is None

    v_kinds = ("h", "t") if prompt else ("h", BF16)
    q, k, kb, v, vx = _norm_proj(
        x, W["norm_mix_g"], W["w_a"],
        ((0, aw, (BF16,)), (aw, aw, ("h", BF16)), (2 * aw, aw, v_kinds)), tm if not prompt else tk,
        layer=l, depth=depth, carried=kv_prev)
    qkvb, ba = _norm_proj(x, W["norm_mix_g"], W["w_b"],
                          ((0, nqkv, (F32,)), (nqkv, LANES, (F32,))), tm)
    z, ga, gb = _norm_proj(x, W["norm_mix_g"], W["w_c"],
                           ((0, aw, (F32,)), (aw, d, (F32,)), (aw + d, d, (F32,))), tm)

    if prompt:
        o_a = _attn_prompt(q, kb, vx, W["lam_rows"], W["subln_g"], slopes, lam0, tq, tk)
    else:
        o_a = _attn_sample(q, kb, vx, cache_k, cache_v, l, W["lam_rows"], W["subln_g"],
                           slopes, lam0, rows_per_seq)

    o_b, s_new, gbuf_new = _gdn(qkvb, ba, z, s0, gbuf, W["gdn_conv_w"], W["par"],
                                W["gdn_norm_g"], n_seq, rows_per_seq,
                                min(GDN_ROWS_PER_STEP, rows_per_seq))

    h = _merge(x, o_a, o_b, ga, gb, W["w_proj_a"], W["w_proj_b"], W["w_out"], tm)
    d_ff = W["w_down"].shape[0]
    y, fbuf_new = _ffn(h, W["norm_ffn_g"], fbuf, W["w_up"], W["ffn_conv_w"], W["ffn_conv_b"],
                       W["w_down"], W["final_g"], n_seq, rows_per_seq,
                       min(FFN_ROW_TILE, rows_per_seq), d_ff // FFN_COLUMN_CHUNKS, final)
    return y, k, v, s_new, gbuf_new, fbuf_new


def kernel(x_prompt, x_sample, cache_k, cache_v, state_gdn, state_gdn_conv, state_ffn_conv, norm_mix_g, w_in, lambda_q1, lambda_k1, lambda_q2, lambda_k2, subln_g, gdn_conv_w, gdn_a_log, gdn_dt_bias, gdn_norm_g, w_proj_a, w_proj_b, w_out, norm_ffn_g, w_up, ffn_conv_w, ffn_conv_b, w_down, final_norm_g):
    P = dict(norm_mix_g=norm_mix_g, w_in=w_in, lambda_q1=lambda_q1, lambda_k1=lambda_k1,
             lambda_q2=lambda_q2, lambda_k2=lambda_k2, subln_g=subln_g, gdn_conv_w=gdn_conv_w,
             gdn_a_log=gdn_a_log, gdn_dt_bias=gdn_dt_bias, gdn_norm_g=gdn_norm_g,
             w_proj_a=w_proj_a, w_proj_b=w_proj_b, w_out=w_out, norm_ffn_g=norm_ffn_g,
             w_up=w_up, ffn_conv_w=ffn_conv_w, ffn_conv_b=ffn_conv_b, w_down=w_down,
             final_norm_g=final_norm_g)
    depth = w_in.shape[0]
    bp, tp, d = x_prompt.shape
    bs, ts, _ = x_sample.shape
    past = cache_k.shape[2]
    assert ts == CHUNK and past % CHUNK == 0 and tp % CHUNK == 0
    nqkv = state_gdn_conv.shape[-1]
    d_ff2 = state_ffn_conv.shape[-1]
    weights = [_prep_layer_weights(l, P) for l in range(depth)]
    ck = cache_k.reshape(depth, bs, past * A_HEADS, A_VDIM)
    cv = cache_v.reshape(depth, bs, past * A_HEADS, A_VDIM)

    tk = min(ATTN_KEY_TILE, tp)
    tq = min(ATTN_QUERY_BLOCK, tp)
    tm = 512

    def run(x3, nseq, rps, sample):
        assert sample or nseq == 1
        x = x3.reshape(nseq * rps, d)
        kv = tuple(jnp.zeros((depth, nseq * rps, A_HEADS, A_VDIM), F32) for _ in range(2))
        ss, gcs, fcs = [], [], []
        for l in range(depth):
            if sample:
                s0, gbuf, fbuf = state_gdn[l], state_gdn_conv[l], state_ffn_conv[l]
                c_k, c_v = ck, cv
            else:
                s0 = jnp.zeros((nseq, B_HEADS, B_DK, B_DV), F32)
                gbuf = jnp.zeros((nseq, GDN_CONV - 1, nqkv), F32)
                fbuf = jnp.zeros((nseq, FFN_CONV - 1, d_ff2), F32)
                c_k = c_v = None
            x, k_all, v_all, s_new, g_new, f_new = _layer(
                x, l, depth, kv, weights[l], nseq, rps, c_k, c_v, s0, gbuf, fbuf,
                l == depth - 1, tm, tq, tk)
            kv = (k_all, v_all)
            ss.append(s_new)
            gcs.append(g_new)
            fcs.append(f_new)
        kv_shape = (depth, nseq, rps, A_HEADS, A_VDIM)
        return (x.reshape(nseq, rps, d), kv[0].reshape(kv_shape), kv[1].reshape(kv_shape),
                jnp.stack(ss), jnp.stack(gcs), jnp.stack(fcs))

    y_p, k_p, v_p, s_p, gc_p, fc_p = run(x_prompt, bp, tp, False)
    y_s, k_s, v_s, s_s, gc_s, fc_s = run(x_sample, bs, ts, True)
    return (y_p, y_s, k_p, v_p, s_p, gc_p, fc_p, k_s, v_s, s_s, gc_s, fc_s)
```

```python
import functools
import math

import jax
import jax.numpy as jnp
from jax import lax
from jax.experimental import pallas as pl
from jax.experimental.pallas import tpu as pltpu

F32 = jnp.float32
BF16 = jnp.bfloat16

EPS = 1e-6
NEG = -1e30
LOG2E = 1.4426950408889634

CHUNK = 64
A_HEADS = 8
A_HEAD_DIM = 64
A_VDIM = 128
B_HEADS = 8
B_DK = 128
B_DV = 128
GDN_CONV = 4
FFN_CONV = 3
LANES = 128
HIST = 8

VMEM_LIMIT = 56 * 1024 * 1024


def _cparams(n_axes):
    return pltpu.CompilerParams(
        dimension_semantics=("arbitrary",) * n_axes,
        vmem_limit_bytes=VMEM_LIMIT)


def _lambda_init(layer):
    return 0.8 - 0.6 * math.exp(-0.3 * layer)


def _norm_proj_kernel(x_ref, g_ref, w_ref, *refs, segs, n_carried):
    out_refs = refs[n_carried:]
    x = x_ref[...]
    ms = jnp.mean(x * x, axis=-1, keepdims=True)
    xn = (x * lax.rsqrt(ms + EPS) * g_ref[...]).astype(BF16)
    k = 0
    for start, width, kinds in segs:
        w = w_ref[:, start:start + width]
        r = None
        for kind in kinds:
            o_ref = out_refs[k]
            k += 1
            if r is None:
                r = jnp.dot(xn, w, preferred_element_type=F32)
            if kind == "t":
                o_ref[...] = r.T.astype(o_ref.dtype)
            else:
                o_ref[...] = r.astype(o_ref.dtype)


def _norm_proj(x, g, w, segs, tm, layer=0, depth=1, carried=()):
    rows, d = x.shape
    assert rows % tm == 0
    out_shape, out_specs, ksegs, h_outs = [], [], [], []
    for start, width, kinds in segs:
        kk = []
        for kind in kinds:
            if kind == "t":
                out_shape.append(jax.ShapeDtypeStruct((rows // tm, width, tm), BF16))
                out_specs.append(pl.BlockSpec((None, width, tm), lambda i: (i, 0, 0)))
                kk.append("t")
            elif kind == "h":
                h_outs.append(len(out_shape))
                out_shape.append(jax.ShapeDtypeStruct((depth, rows, width), F32))
                out_specs.append(pl.BlockSpec((None, tm, width), lambda i: (layer, i, 0)))
                kk.append("n")
            else:
                out_shape.append(jax.ShapeDtypeStruct((rows, width), kind))
                out_specs.append(pl.BlockSpec((tm, width), lambda i: (i, 0)))
                kk.append("n")
        ksegs.append((start, width, tuple(kk)))
    assert len(carried) in (0, len(h_outs))
    return pl.pallas_call(
        functools.partial(_norm_proj_kernel, segs=tuple(ksegs), n_carried=len(carried)),
        grid=(rows // tm,),
        in_specs=[pl.BlockSpec((tm, d), lambda i: (i, 0)),
                  pl.BlockSpec((1, d), lambda i: (0, 0)),
                  pl.BlockSpec(w.shape, lambda i: (0, 0))]
                 + [pl.BlockSpec(memory_space=pl.ANY)] * len(carried),
        out_specs=out_specs,
        out_shape=out_shape,
        input_output_aliases={3 + n: h_outs[n] for n in range(len(carried))},
        compiler_params=_cparams(1),
        name="norm_proj",
    )(x, g, w, *carried)


def _lambda_value(lam_ref, lam0):
    a = jnp.sum(lam_ref[0:1, :] * lam_ref[1:2, :], axis=-1, keepdims=True)
    b = jnp.sum(lam_ref[2:3, :] * lam_ref[3:4, :], axis=-1, keepdims=True)
    return jnp.exp(a) - jnp.exp(b) + lam0


def _stack_maps(q):
    lane = lax.broadcasted_iota(jnp.int32, q.shape, 1)
    zero = jnp.zeros_like(q)
    return jnp.concatenate([jnp.where(lane < A_HEAD_DIM, q, zero),
                            jnp.where(lane >= A_HEAD_DIM, q, zero)], axis=0)


NT_DIMS = (((1,), (1,)), ((), ()))
ATTN_KEY_TILE = 512
ATTN_QUERY_BLOCK = 1024


def _attn_prompt_kernel(slopes_ref, lam_ref, g_ref, q_ref, k_ref, vt_ref, o_ref,
                        qq_sc, kaug_sc, dfix_sc, sa_sc, sb_sc, m_sc, l_sc, acc_sc,
                        *, tq, tk, lam0):
    h = pl.program_id(0)
    i = pl.program_id(1)
    slope = slopes_ref[h]
    q0 = i * tq
    nt = NT_DIMS
    w = tk
    nd = tq // tk
    n_off = i * nd
    n_strips = 2 * tq // w

    q_t = q_ref[...].astype(F32).T
    feat = lax.broadcasted_iota(jnp.int32, q_t.shape, 0)
    qq_sc[:A_VDIM, :] = jnp.concatenate(
        [jnp.where(feat < A_HEAD_DIM, q_t, 0.0), jnp.where(feat >= A_HEAD_DIM, q_t, 0.0)],
        axis=1).astype(BF16)
    row_q = lax.broadcasted_iota(jnp.int32, (LANES, 2 * tq), 0)
    qq_sc[A_VDIM:, :] = jnp.where(row_q < 3, 1.0, 0.0).astype(BF16)

    @pl.when(i == 0)
    def _():
        b = slope * lax.broadcasted_iota(jnp.int32, (tk, LANES), 0).astype(F32)
        b1 = b.astype(BF16).astype(F32)
        b2 = (b - b1).astype(BF16).astype(F32)
        b3 = b - b1 - b2
        lane_k = lax.broadcasted_iota(jnp.int32, (tk, LANES), 1)
        kaug_sc[...] = jnp.where(lane_k == 0, b1, jnp.where(
            lane_k == 1, b2, jnp.where(lane_k == 2, b3, 0.0))).astype(BF16)
        c = lax.broadcasted_iota(jnp.int32, (tk, w), 0)
        r = lax.broadcasted_iota(jnp.int32, (tk, w), 1)
        fix = slope * jnp.minimum(2 * (r - c), 0).astype(F32)
        dfix_sc[...] = jnp.where((c // CHUNK) <= (r // CHUNK), fix, NEG)

    m_sc[...] = jnp.full(m_sc.shape, NEG, F32)
    l_sc[...] = jnp.zeros(l_sc.shape, F32)
    acc_sc[...] = jnp.zeros(acc_sc.shape, F32)

    groups = 8

    def key_reduce(op, x):
        part = op(x.reshape(groups, tk // groups, x.shape[-1]), axis=0)
        return op(part, axis=0, keepdims=True)

    def softmax_update(st, s_ref, kappa, vtt):
        sl = slice(st * w, (st + 1) * w)
        m_prev = m_sc[:, sl]
        m_new = jnp.maximum(m_prev, key_reduce(jnp.max, s_ref[:, sl]) + kappa)
        alpha = jnp.exp2(m_prev - m_new)
        p = jnp.exp2(s_ref[:, sl] - (m_new - kappa))
        pv = jnp.dot(jnp.concatenate([vtt, jnp.ones((16, tk), BF16)], axis=0), p.astype(BF16),
                     preferred_element_type=F32)
        l_sc[:, sl] = alpha * l_sc[:, sl] + pv[A_VDIM:A_VDIM + 1]
        acc_sc[:, sl] = alpha * acc_sc[:, sl] + pv[:A_VDIM]
        m_sc[:, sl] = m_new

    def produce(j, s_ref, strips=None):
        k0 = pl.multiple_of(j * tk, tk)
        lhs = jnp.concatenate([k_ref[pl.ds(k0, tk), :], kaug_sc[...]], axis=1)
        if strips is None:
            s_ref[...] = jnp.dot(lhs, qq_sc[...], preferred_element_type=F32)
        else:
            for st in strips:
                sl = slice(st * w, (st + 1) * w)
                s_ref[:, sl] = jnp.dot(lhs, qq_sc[:, sl], preferred_element_type=F32)

    def visible_strips(d):
        return [st for st in range(n_strips) if (st * w) % tq >= d * tk]

    def consume(j, s_ref):
        kappa = slope * (j * tk - q0).astype(F32)
        vtj = vt_ref[j]
        for st in range(n_strips):
            softmax_update(st, s_ref, kappa, vtj)

    def consume_overlap(d, s_ref):
        kappa = slope * (d * tk)
        vtd = vt_ref[n_off + d]
        for st in visible_strips(d):
            if (st * w) % tq == d * tk:
                sl = slice(st * w, (st + 1) * w)
                s_ref[:, sl] = s_ref[:, sl] + dfix_sc[...]
            softmax_update(st, s_ref, kappa, vtd)

    def overlap_tiles(first_ref, second_ref):
        for d in range(nd):
            cur, nxt = (first_ref, second_ref) if d % 2 == 0 else (second_ref, first_ref)
            if d + 1 < nd:
                produce(n_off + d + 1, nxt, visible_strips(d + 1))
            consume_overlap(d, cur)

    produce(0, sa_sc)

    def produce_consume(jp, p_ref, jc, c_ref):
        kappa = slope * (jc * tk - q0).astype(F32)
        vtj = vt_ref[jc]
        for st in range(n_strips):
            produce(jp, p_ref, [st])
            softmax_update(st, c_ref, kappa, vtj)

    def two_tiles(j):
        produce_consume(j + 1, sb_sc, j, sa_sc)
        produce_consume(j + 2, sa_sc, j + 1, sb_sc)

    def four_tiles(t, carry):
        two_tiles(4 * t)
        two_tiles(4 * t + 2)
        return carry

    lax.fori_loop(0, n_off // 4, four_tiles, 0)

    @pl.when(n_off % 4 >= 2)
    def _():
        two_tiles((n_off // 4) * 4)

    if nd % 2 == 0:
        overlap_tiles(sa_sc, sb_sc)
    else:
        @pl.when(n_off % 2 == 0)
        def _():
            overlap_tiles(sa_sc, sb_sc)

        @pl.when(n_off % 2 == 1)
        def _():
            produce(n_off, sb_sc)
            consume(n_off - 1, sa_sc)
            overlap_tiles(sb_sc, sa_sc)

    lam = _lambda_value(lam_ref, lam0)
    inv_l = 1.0 / l_sc[...]
    acc = acc_sc[...] * inv_l
    o = acc[:, :tq] - lam * acc[:, tq:]
    ms = jnp.mean(o * o, axis=0, keepdims=True)
    o = (o * lax.rsqrt(ms + EPS)).T
    o_ref[...] = (o * g_ref[...] * (1.0 - lam0)).astype(o_ref.dtype)


def _attn_prompt(q, kb, vt, lam_rows, subln_g, slopes, lam0, tq, tk):
    t = q.shape[0]
    assert t % tq == 0 and tq % tk == 0
    return pl.pallas_call(
        functools.partial(_attn_prompt_kernel, tq=tq, tk=tk, lam0=lam0),
        grid=(A_HEADS, t // tq),
        in_specs=[pl.BlockSpec(memory_space=pltpu.SMEM),
                  pl.BlockSpec((8, LANES), lambda h, i: (0, 0)),
                  pl.BlockSpec((1, A_VDIM), lambda h, i: (0, 0)),
                  pl.BlockSpec((tq, A_VDIM), lambda h, i: (i, h)),
                  pl.BlockSpec((t, A_VDIM), lambda h, i: (0, h)),
                  pl.BlockSpec((t // tk, A_VDIM, tk), lambda h, i: (0, h, 0))],
        out_specs=pl.BlockSpec((tq, A_VDIM), lambda h, i: (i, h)),
        out_shape=jax.ShapeDtypeStruct((t, A_HEADS * A_VDIM), BF16),
        scratch_shapes=[pltpu.VMEM((2 * LANES, 2 * tq), BF16),
                        pltpu.VMEM((tk, LANES), BF16),
                        pltpu.VMEM((tk, tk), F32),
                        pltpu.VMEM((tk, 2 * tq), F32),
                        pltpu.VMEM((tk, 2 * tq), F32),
                        pltpu.VMEM((1, 2 * tq), F32),
                        pltpu.VMEM((1, 2 * tq), F32),
                        pltpu.VMEM((A_VDIM, 2 * tq), F32)],
        compiler_params=_cparams(2),
        name="attn_prompt",
    )(slopes, lam_rows, subln_g, q, kb, vt)


SAMPLE_KEY_TILE = 2048


def _attn_sample_kernel(slopes_ref, lam_ref, g_ref, q_ref, kn_ref, vn_ref, kc_ref, vc_ref,
                        o_ref, m_sc, l_sc, acc_sc, *, past, lam0):
    t = pl.program_id(1)
    nt = pl.num_programs(1)
    tq = q_ref.shape[0]
    tk = kc_ref.shape[0] // A_HEADS

    @pl.when(t == 0)
    def _():
        m_sc[...] = jnp.full(m_sc.shape, NEG, F32)
        l_sc[...] = jnp.zeros(l_sc.shape, F32)
        acc_sc[...] = jnp.zeros(acc_sc.shape, F32)

    def update(h, s, v):
        m_prev = m_sc[h]
        m_new = jnp.maximum(m_prev, jnp.max(s, axis=-1, keepdims=True))
        alpha = jnp.exp2(m_prev - m_new)
        p = jnp.exp2(s - m_new)
        l_sc[h] = alpha * l_sc[h] + jnp.sum(p, axis=-1, keepdims=True)
        acc_sc[h] = alpha * acc_sc[h] + jnp.dot(p.astype(BF16), v, preferred_element_type=F32)
        m_sc[h] = m_new

    kpos = t * tk + lax.broadcasted_iota(jnp.int32, (1, tk), 1)
    rel = (kpos - past).astype(F32)
    heads = range(A_HEADS)
    qqs = [_stack_maps(q_ref[:, h * A_VDIM:(h + 1) * A_VDIM]) for h in heads]
    scores = [lax.dot_general(qqs[h], kc_ref[pl.ds(h, tk, stride=A_HEADS), :].astype(BF16),
                              NT_DIMS, preferred_element_type=F32) for h in heads]
    m_prev = [m_sc[h] for h in heads]
    m_new, alpha, p = [], [], []
    for h in heads:
        s = scores[h] + slopes_ref[h] * rel
        m_new.append(jnp.maximum(m_prev[h], jnp.max(s, axis=-1, keepdims=True)))
        alpha.append(jnp.exp2(m_prev[h] - m_new[h]))
        p.append(jnp.exp2(s - m_new[h]))
    pv = [jnp.dot(p[h].astype(BF16), vc_ref[pl.ds(h, tk, stride=A_HEADS), :].astype(BF16),
                  preferred_element_type=F32) for h in heads]
    for h in heads:
        l_sc[h] = alpha[h] * l_sc[h] + jnp.sum(p[h], axis=-1, keepdims=True)
        acc_sc[h] = alpha[h] * acc_sc[h] + pv[h]
        m_sc[h] = m_new[h]

    @pl.when(t == nt - 1)
    def _():
        lam = _lambda_value(lam_ref, lam0)
        r = lax.broadcasted_iota(jnp.int32, (2 * tq, tq), 0)
        c = lax.broadcasted_iota(jnp.int32, (2 * tq, tq), 1)
        r = jnp.where(r >= tq, r - tq, r)
        own = jnp.minimum(c, 2 * r - c).astype(F32)
        for h in range(A_HEADS):
            cols = slice(h * A_VDIM, (h + 1) * A_VDIM)
            s = lax.dot_general(qqs[h], kn_ref[:, cols], NT_DIMS, preferred_element_type=F32)
            update(h, s + slopes_ref[h] * own, vn_ref[:, cols])
            acc = acc_sc[h] * (1.0 / l_sc[h])
            o = acc[:tq] - lam * acc[tq:]
            ms = jnp.mean(o * o, axis=-1, keepdims=True)
            o_ref[:, cols] = (o * lax.rsqrt(ms + EPS) * g_ref[...]
                              * (1.0 - lam0)).astype(o_ref.dtype)


def _attn_sample(q, kb, vb, cache_k, cache_v, layer, lam_rows, subln_g, slopes, lam0, tq):
    nb, past = cache_k.shape[1], cache_k.shape[2] // A_HEADS
    tk = min(SAMPLE_KEY_TILE, past)
    assert past % tk == 0
    aw = A_HEADS * A_VDIM
    rows = pl.BlockSpec((tq, aw), lambda b, t: (b, 0))
    cache = pl.BlockSpec((None, None, tk * A_HEADS, A_VDIM), lambda b, t: (layer, b, t, 0))
    return pl.pallas_call(
        functools.partial(_attn_sample_kernel, past=past, lam0=lam0),
        grid=(nb, past // tk),
        in_specs=[pl.BlockSpec(memory_space=pltpu.SMEM),
                  pl.BlockSpec((8, LANES), lambda b, t: (0, 0)),
                  pl.BlockSpec((1, A_VDIM), lambda b, t: (0, 0)),
                  rows, rows, rows, cache, cache],
        out_specs=rows,
        out_shape=jax.ShapeDtypeStruct((nb * tq, aw), BF16),
        scratch_shapes=[pltpu.VMEM((A_HEADS, 2 * tq, 1), F32),
                        pltpu.VMEM((A_HEADS, 2 * tq, 1), F32),
                        pltpu.VMEM((A_HEADS, 2 * tq, A_VDIM), F32)],
        compiler_params=_cparams(2),
        name="attn_sample",
    )(slopes, lam_rows, subln_g, q, kb, vb, cache_k, cache_v)


_HI = lax.Precision.HIGHEST
GDN_ROWS_PER_STEP = 8 * CHUNK


def _dot(a, b):
    return jnp.dot(a, b, preferred_element_type=F32)


def _split(a):
    hi = a.astype(BF16)
    return hi, (a - hi.astype(F32)).astype(BF16)


def _block_diag2(a, left):
    zero = jnp.zeros_like(a)
    return jnp.concatenate([jnp.where(left, a, zero), jnp.where(left, zero, a)], axis=0)


def _pair_prod(a_parts, b_parts, left):
    ah, bh = a_parts[0], b_parts[0]
    bdh = _block_diag2(bh, left)
    if len(a_parts) == 1 and len(b_parts) == 1:
        return _dot(ah, bdh)
    if len(a_parts) == 1:
        return _dot(jnp.concatenate([ah, ah], axis=1),
                    jnp.concatenate([bdh, _block_diag2(b_parts[1], left)], axis=0))
    r = _dot(jnp.concatenate([ah, a_parts[1]], axis=1), jnp.concatenate([bdh, bdh], axis=0))
    if len(b_parts) == 1:
        return r
    return r + _dot(ah, _block_diag2(b_parts[1], left))


def _inverse_masks(ri, cj):
    n = ri.shape[0]
    masks = [(ri // 2) == (cj // 2)]
    b = 2
    while b < n:
        same_big = ((ri // (2 * b)) == (cj // (2 * b))).astype(jnp.int32)
        same_small = ((ri // b) == (cj // b)).astype(jnp.int32)
        masks.append((same_big - same_small) > 0)
        b *= 2
    return masks


def _unit_lower_inverse(lows, masks, eye, left):
    xs = [eye - jnp.where(masks[0], low, 0.0) for low in lows]
    for mask in masks[1:]:
        xparts = [_split(x) for x in xs]
        ys = [_pair_prod((jnp.where(mask, low, 0.0).astype(BF16),), xp, left)
              for low, xp in zip(lows, xparts)]
        xs = [x - _pair_prod(xp, _split(y), left) for x, xp, y in zip(xs, xparts, ys)]
    return xs


def _gdn_kernel(x_ref, buf_ref, ba_ref, z_ref, s0_ref, cw_ref, par_ref, ng_ref,
                o_ref, s_out_ref, buf_out_ref, xp_sc, s_sc):
    c_idx = pl.program_id(1)
    nc = pl.num_programs(1)
    R = x_ref.shape[0]
    C = CHUNK
    n_sub = R // C
    nq = B_HEADS * B_DK
    cat = jnp.concatenate

    @pl.when(c_idx == 0)
    def _():
        xp_sc[HIST - (GDN_CONV - 1):HIST, :] = buf_ref[...]
        s_sc[...] = s0_ref[...].astype(F32)

    xp_sc[HIST:HIST + R, :] = x_ref[...]
    xp = xp_sc[...]
    conv = xp[HIST:] * cw_ref[GDN_CONV - 1:GDN_CONV, :]
    for j in range(GDN_CONV - 1):
        shifted = pltpu.roll(xp, GDN_CONV - 1 - j, axis=0)
        conv = conv + shifted[HIST:] * cw_ref[j:j + 1, :]
    conv = conv * jax.nn.sigmoid(conv)
    tail = xp_sc[R:R + HIST, :]
    xp_sc[0:HIST, :] = tail
    buf_out_ref[...] = tail[HIST - (GDN_CONV - 1):HIST, :]

    ba = ba_ref[...]
    lane = lax.broadcasted_iota(jnp.int32, ba.shape, 1)
    beta_all = jax.nn.sigmoid(ba)
    g_all = jnp.where((lane >= B_HEADS) & (lane < 2 * B_HEADS),
                      -jnp.exp(par_ref[0:1, :]) * jax.nn.softplus(ba + par_ref[1:2, :]), 0.0)
    rr = lax.broadcasted_iota(jnp.int32, (R, R), 0)
    cc = lax.broadcasted_iota(jnp.int32, (R, R), 1)
    same_chunk = ((rr // C) - (cc // C)) == 0
    tri = jnp.where(same_chunk, jnp.where(rr >= cc, 1.0, 0.0), 0.0).astype(F32)
    gcum_all = jnp.dot(tri, g_all, precision=_HI, preferred_element_type=F32)

    ri = lax.broadcasted_iota(jnp.int32, (C, 2 * C), 0)
    cl = lax.broadcasted_iota(jnp.int32, (C, 2 * C), 1)
    left = cl < C
    cj = jnp.where(left, cl, cl - C)
    incl = ri >= cj
    strict = ri > cj
    eye = jnp.where(ri == cj, 1.0, 0.0).astype(F32)
    inv_masks = _inverse_masks(ri, cj)
    zk = jnp.zeros((C, B_DK), F32)
    z2 = jnp.zeros((C, B_DV + B_DK), F32)
    zv = jnp.zeros((C, B_DV), F32)

    subs = range(n_sub)
    pairs = range(B_HEADS // 2)
    q, k, v, kb, beta, gc, eg, g_last, gcum_t = [], [], [], [], [], [], [], [], []
    for c in subs:
        rows = slice(c * C, (c + 1) * C)
        gsub = gcum_all[rows]
        gcum_t.append(cat([gsub, gsub], axis=0).T)
        qc, kc, vc, kbc, bc, gcc, egc = [], [], [], [], [], [], []
        for h in range(B_HEADS):
            qh = conv[rows, h * B_DK:(h + 1) * B_DK]
            kh = conv[rows, nq + h * B_DK:nq + (h + 1) * B_DK]
            qc.append(qh * lax.rsqrt(jnp.sum(qh * qh, axis=-1, keepdims=True) + EPS)
                      * (B_DK ** -0.5))
            kc.append(kh * lax.rsqrt(jnp.sum(kh * kh, axis=-1, keepdims=True) + EPS))
            vc.append(conv[rows, 2 * nq + h * B_DV:2 * nq + (h + 1) * B_DV])
            bc.append(beta_all[rows, h:h + 1])
            gcc.append(gsub[:, B_HEADS + h:B_HEADS + h + 1])
            kbc.append(kc[-1] * bc[-1])
            egc.append(jnp.exp(gcc[-1]))
        q.append(qc)
        k.append(kc)
        v.append(vc)
        kb.append(kbc)
        beta.append(bc)
        gc.append(gcc)
        eg.append(egc)
        g_last.append([g[C - 1:C, :] for g in gcc])

    lows, attns = [], []
    for c in subs:
        for p in pairs:
            a, b = 2 * p, 2 * p + 1
            g_col = jnp.where(left, gc[c][a], gc[c][b])
            g_row = jnp.where(left[0:1], gcum_t[c][B_HEADS + a:B_HEADS + a + 1, :],
                              gcum_t[c][B_HEADS + b:B_HEADS + b + 1, :])
            decay = jnp.where(incl, jnp.exp(jnp.where(incl, g_col - g_row, 0.0)), 0.0)
            lhs = cat([cat([kb[c][a], kb[c][b]], axis=1), cat([q[c][a], q[c][b]], axis=1)],
                      axis=0)
            rhs = cat([cat([k[c][a], zk], axis=1), cat([zk, k[c][b]], axis=1)], axis=0)
            m1 = lax.dot_general(lhs.astype(BF16), rhs.astype(BF16), NT_DIMS,
                                 preferred_element_type=F32)
            lows.append(jnp.where(strict, m1[:C] * decay, 0.0).astype(BF16).astype(F32))
            attns.append(m1[C:] * decay)

    tinvs = _unit_lower_inverse(lows, inv_masks, eye, left)

    sols = []
    for c in subs:
        for p in pairs:
            a, b = 2 * p, 2 * p + 1
            rhs_a = cat([v[c][a] * beta[c][a], kb[c][a] * eg[c][a]], axis=1)
            rhs_b = cat([v[c][b] * beta[c][b], kb[c][b] * eg[c][b]], axis=1)
            rhs_bd = cat([cat([rhs_a, z2], axis=1), cat([z2, rhs_b], axis=1)], axis=0)
            sols.append(_dot(tinvs[c * len(pairs) + p].astype(BF16), rhs_bd.astype(BF16)))

    s_cur = [s_sc[h] for h in range(B_HEADS)]
    for c in subs:
        rows = slice(c * C, (c + 1) * C)
        v_new, q_s = [], []
        for h in range(B_HEADS):
            sol = sols[c * len(pairs) + h // 2]
            base = (h % 2) * (B_DV + B_DK)
            u = sol[:, base:base + B_DV]
            w = sol[:, base + B_DV:base + B_DV + B_DK]
            m2 = _dot(cat([w, q[c][h] * eg[c][h]], axis=0).astype(BF16), s_cur[h].astype(BF16))
            v_new.append(u - m2[:C])
            q_s.append(m2[C:])
        o_pairs, upds = [], []
        for p in pairs:
            a, b = 2 * p, 2 * p + 1
            vn_bd = cat([cat([v_new[a], zv], axis=1), cat([zv, v_new[b]], axis=1)],
                        axis=0).astype(BF16)
            o_pairs.append(_dot(attns[c * len(pairs) + p].astype(BF16), vn_bd))
            kd_t = cat([k[c][a] * jnp.exp(g_last[c][a] - gc[c][a]),
                        k[c][b] * jnp.exp(g_last[c][b] - gc[c][b])], axis=0).T
            upds.append(_dot(kd_t.astype(BF16), vn_bd))
        for h in range(B_HEADS):
            lo = (h % 2) * B_DV
            s_cur[h] = s_cur[h] * jnp.exp(g_last[c][h]) + upds[h // 2][:, lo:lo + B_DV]
            o = q_s[h] + o_pairs[h // 2][:, lo:lo + B_DV]
            zh = z_ref[rows, h * B_DV:(h + 1) * B_DV]
            on = o * lax.rsqrt(jnp.mean(o * o, axis=-1, keepdims=True) + EPS) * ng_ref[...]
            o_ref[rows, h * B_DV:(h + 1) * B_DV] = (
                on * (zh * jax.nn.sigmoid(zh))).astype(o_ref.dtype)
    for h in range(B_HEADS):
        s_sc[h] = s_cur[h]

    @pl.when(c_idx == nc - 1)
    def _():
        s_out_ref[...] = s_sc[...].astype(s_out_ref.dtype)


def _gdn(qkv, ba, z, s0, buf, conv_w, par, norm_g, n_seq, rows_per_seq, rows_per_step):
    assert rows_per_seq % rows_per_step == 0 and rows_per_step % CHUNK == 0
    nc = rows_per_seq // rows_per_step
    nqkv = qkv.shape[1]
    row = lambda b, c: (b * nc + c, 0)
    return pl.pallas_call(
        _gdn_kernel,
        grid=(n_seq, nc),
        in_specs=[pl.BlockSpec((rows_per_step, nqkv), row),
                  pl.BlockSpec((None, GDN_CONV - 1, nqkv), lambda b, c: (b, 0, 0)),
                  pl.BlockSpec((rows_per_step, LANES), row),
                  pl.BlockSpec((rows_per_step, B_HEADS * B_DV), row),
                  pl.BlockSpec((None, B_HEADS, B_DK, B_DV), lambda b, c: (b, 0, 0, 0)),
                  pl.BlockSpec((GDN_CONV, nqkv), lambda b, c: (0, 0)),
                  pl.BlockSpec((8, LANES), lambda b, c: (0, 0)),
                  pl.BlockSpec((1, B_DV), lambda b, c: (0, 0))],
        out_specs=[pl.BlockSpec((rows_per_step, B_HEADS * B_DV), row),
                   pl.BlockSpec((None, B_HEADS, B_DK, B_DV), lambda b, c: (b, 0, 0, 0)),
                   pl.BlockSpec((None, GDN_CONV - 1, nqkv), lambda b, c: (b, 0, 0))],
        out_shape=[jax.ShapeDtypeStruct((n_seq * rows_per_seq, B_HEADS * B_DV), BF16),
                   jax.ShapeDtypeStruct((n_seq, B_HEADS, B_DK, B_DV), F32),
                   jax.ShapeDtypeStruct((n_seq, GDN_CONV - 1, nqkv), F32)],
        scratch_shapes=[pltpu.VMEM((HIST + rows_per_step, nqkv), F32),
                        pltpu.VMEM((B_HEADS, B_DK, B_DV), F32)],
        compiler_params=_cparams(2),
        name="gdn",
    )(qkv, buf, ba, z, s0, conv_w, par, norm_g)


def _merge_kernel(x_ref, oa_ref, ob_ref, ga_ref, gb_ref, wa_ref, wb_ref, wo_ref, h_ref):
    a = jnp.dot(oa_ref[...], wa_ref[...], preferred_element_type=F32)
    b = jnp.dot(ob_ref[...], wb_ref[...], preferred_element_type=F32)
    merged = jax.nn.sigmoid(ga_ref[...]) * a + jax.nn.sigmoid(gb_ref[...]) * b
    h_ref[...] = x_ref[...] + jnp.dot(merged.astype(BF16), wo_ref[...],
                                      preferred_element_type=F32)


def _merge(x, oa, ob, ga, gb, wa, wb, wo, tm):
    rows, d = x.shape
    rowspec = pl.BlockSpec((tm, d), lambda i: (i, 0))
    wspec = pl.BlockSpec((d, d), lambda i: (0, 0))
    return pl.pallas_call(
        _merge_kernel,
        grid=(rows // tm,),
        in_specs=[rowspec] * 5 + [wspec] * 3,
        out_specs=rowspec,
        out_shape=jax.ShapeDtypeStruct((rows, d), F32),
        compiler_params=_cparams(1),
        name="merge",
    )(x, oa, ob, ga, gb, wa, wb, wo)


FFN_ROW_TILE = 512
FFN_COLUMN_CHUNKS = 1


def _ffn_kernel(h_ref, g_ref, buf_g_ref, buf_v_ref, wg_ref, wv_ref, cwg_ref, cwv_ref,
                cbg_ref, cbv_ref, wd_ref, fg_ref, y_ref, nbuf_g_ref, nbuf_v_ref,
                hn_sc, ug_sc, uv_sc, acc_sc, *, tiles_per_seq, final):
    i = pl.program_id(0)
    j = pl.program_id(1)
    nj = pl.num_programs(1)
    tm = h_ref.shape[0]
    first = (i % tiles_per_seq) == 0
    nh = FFN_CONV - 1

    @pl.when(j == 0)
    def _():
        x = h_ref[...]
        ms = jnp.mean(x * x, axis=-1, keepdims=True)
        hn_sc[...] = (x * lax.rsqrt(ms + EPS) * g_ref[...]).astype(BF16)
        acc_sc[...] = jnp.zeros(acc_sc.shape, F32)

    def conv_half(u_sc, w_ref, cw_ref, cb_ref, buf_ref, nbuf_ref):
        @pl.when(first)
        def _():
            u_sc[j, HIST - nh:HIST, :] = buf_ref[...]
        u_sc[j, HIST:HIST + tm, :] = jnp.dot(hn_sc[...], w_ref[...], preferred_element_type=F32)
        out = u_sc[j, HIST:HIST + tm, :] * cw_ref[nh:nh + 1, :] + cb_ref[...]
        for t in range(nh):
            off = HIST - nh + t
            out = out + u_sc[j, off:off + tm, :] * cw_ref[t:t + 1, :]
        tail = u_sc[j, tm:tm + HIST, :]
        u_sc[j, 0:HIST, :] = tail
        nbuf_ref[j] = tail[HIST - nh:HIST, :]
        return out

    gate = conv_half(ug_sc, wg_ref, cwg_ref, cbg_ref, buf_g_ref, nbuf_g_ref)
    val = conv_half(uv_sc, wv_ref, cwv_ref, cbv_ref, buf_v_ref, nbuf_v_ref)
    act = (gate * jax.nn.sigmoid(gate) * val).astype(BF16)
    acc_sc[...] += jnp.dot(act, wd_ref[...], preferred_element_type=F32)

    @pl.when(j == nj - 1)
    def _():
        y = h_ref[...] + acc_sc[...]
        if final:
            ms = jnp.mean(y * y, axis=-1, keepdims=True)
            y = y * lax.rsqrt(ms + EPS) * fg_ref[...]
        y_ref[...] = y


def _ffn(h, g, buf, w_up, conv_w, conv_b, w_down, final_g, n_seq, rows_per_seq, tm, cw, final):
    rows, d = h.shape
    d_ff = w_down.shape[0]
    assert d_ff % cw == 0 and rows_per_seq % tm == 0
    nj = d_ff // cw
    tps = rows_per_seq // tm
    nh = FFN_CONV - 1
    gate_col = lambda i, j: (0, j)
    val_col = lambda i, j: (0, nj + j)
    seq_gate = lambda i, j: (i // tps, 0, j)
    seq_val = lambda i, j: (i // tps, 0, nj + j)
    y, nbg, nbv = pl.pallas_call(
        functools.partial(_ffn_kernel, tiles_per_seq=tps, final=final),
        grid=(rows // tm, nj),
        in_specs=[pl.BlockSpec((tm, d), lambda i, j: (i, 0)),
                  pl.BlockSpec((1, d), lambda i, j: (0, 0)),
                  pl.BlockSpec((None, nh, cw), seq_gate),
                  pl.BlockSpec((None, nh, cw), seq_val),
                  pl.BlockSpec((d, cw), gate_col),
                  pl.BlockSpec((d, cw), val_col),
                  pl.BlockSpec((FFN_CONV, cw), gate_col),
                  pl.BlockSpec((FFN_CONV, cw), val_col),
                  pl.BlockSpec((1, cw), gate_col),
                  pl.BlockSpec((1, cw), val_col),
                  pl.BlockSpec((cw, d), lambda i, j: (j, 0)),
                  pl.BlockSpec((1, d), lambda i, j: (0, 0))],
        out_specs=[pl.BlockSpec((tm, d), lambda i, j: (i, 0)),
                   pl.BlockSpec((None, nj, nh, cw), lambda i, j: (i // tps, 0, 0, 0)),
                   pl.BlockSpec((None, nj, nh, cw), lambda i, j: (i // tps, 0, 0, 0))],
        out_shape=[jax.ShapeDtypeStruct((rows, d), F32),
                   jax.ShapeDtypeStruct((n_seq, nj, nh, cw), F32),
                   jax.ShapeDtypeStruct((n_seq, nj, nh, cw), F32)],
        scratch_shapes=[pltpu.VMEM((tm, d), BF16),
                        pltpu.VMEM((nj, HIST + tm, cw), F32),
                        pltpu.VMEM((nj, HIST + tm, cw), F32),
                        pltpu.VMEM((tm, d), F32)],
        compiler_params=_cparams(2),
        name="ffn",
    )(h, g, buf, buf, w_up, w_up, conv_w, conv_w, conv_b, conv_b, w_down, final_g)
    unchunk = lambda a: a.transpose(0, 2, 1, 3).reshape(n_seq, nh, d_ff)
    return y, jnp.concatenate([unchunk(nbg), unchunk(nbv)], axis=-1)


def _prep_layer_weights(l, P):
    w_in = P["w_in"][l]
    d = w_in.shape[0]
    aw = A_HEADS * A_VDIM
    nqkv = B_HEADS * (2 * B_DK + B_DV)
    c_qkvb = 3 * aw
    c_beta = c_qkvb + nqkv
    c_z = c_beta + 2 * B_HEADS
    bw = B_HEADS * B_DV
    w_q = w_in[:, :aw] * (A_HEAD_DIM ** -0.5 * LOG2E)
    w_a = jnp.concatenate([w_q, w_in[:, aw:3 * aw]], axis=1).astype(BF16)
    w_b = jnp.concatenate([w_in[:, c_qkvb:c_beta + 2 * B_HEADS],
                           jnp.zeros((d, LANES - 2 * B_HEADS), F32)], axis=1).astype(BF16)
    w_c = w_in[:, c_z:c_z + bw + 2 * d].astype(BF16)
    lam_rows = jnp.zeros((8, LANES), F32)
    for r, name in enumerate(("lambda_q1", "lambda_k1", "lambda_q2", "lambda_k2")):
        lam_rows = lam_rows.at[r, :A_HEAD_DIM].set(P[name][l])
    par = jnp.zeros((8, LANES), F32)
    par = par.at[0, B_HEADS:2 * B_HEADS].set(P["gdn_a_log"][l])
    par = par.at[1, B_HEADS:2 * B_HEADS].set(P["gdn_dt_bias"][l])
    return dict(
        w_a=w_a, w_b=w_b, w_c=w_c, lam_rows=lam_rows, par=par,
        norm_mix_g=P["norm_mix_g"][l][None, :],
        subln_g=P["subln_g"][l][None, :],
        gdn_conv_w=P["gdn_conv_w"][l],
        gdn_norm_g=P["gdn_norm_g"][l][None, :],
        w_proj_a=P["w_proj_a"][l].astype(BF16),
        w_proj_b=P["w_proj_b"][l].astype(BF16),
        w_out=P["w_out"][l].astype(BF16),
        norm_ffn_g=P["norm_ffn_g"][l][None, :],
        w_up=P["w_up"][l].astype(BF16),
        ffn_conv_w=P["ffn_conv_w"][l],
        ffn_conv_b=P["ffn_conv_b"][l][None, :],
        w_down=P["w_down"][l].astype(BF16),
        final_g=P["final_norm_g"][None, :],
    )


def _layer(x, l, depth, kv_prev, W, n_seq, rows_per_seq, cache_k, cache_v, s0, gbuf, fbuf,
           final, tm, tq, tk):
    rows, d = x.shape
    tm = min(tm, rows)
    aw = A_HEADS * A_VDIM
    nqkv = B_HEADS * (2 * B_DK + B_DV)
    lam0 = _lambda_init(l)
    slopes = (2.0 ** (-8.0 * jnp.arange(1, A_HEADS + 1, dtype=F32) / A_HEADS)) * LOG2E
    prompt = cache_k is None

    v_kinds = ("h", "t") if prompt else ("h", BF16)
    q, k, kb, v, vx = _norm_proj(
        x, W["norm_mix_g"], W["w_a"],
        ((0, aw, (BF16,)), (aw, aw, ("h", BF16)), (2 * aw, aw, v_kinds)), tm if not prompt else tk,
        layer=l, depth=depth, carried=kv_prev)
    qkvb, ba = _norm_proj(x, W["norm_mix_g"], W["w_b"],
                          ((0, nqkv, (F32,)), (nqkv, LANES, (F32,))), tm)
    z, ga, gb = _norm_proj(x, W["norm_mix_g"], W["w_c"],
                           ((0, aw, (F32,)), (aw, d, (F32,)), (aw + d, d, (F32,))), tm)

    if prompt:
        o_a = _attn_prompt(q, kb, vx, W["lam_rows"], W["subln_g"], slopes, lam0, tq, tk)
    else:
        o_a = _attn_sample(q, kb, vx, cache_k, cache_v, l, W["lam_rows"], W["subln_g"],
                           slopes, lam0, rows_per_seq)

    o_b, s_new, gbuf_new = _gdn(qkvb, ba, z, s0, gbuf, W["gdn_conv_w"], W["par"],
                                W["gdn_norm_g"], n_seq, rows_per_seq,
                                min(GDN_ROWS_PER_STEP, rows_per_seq))

    h = _merge(x, o_a, o_b, ga, gb, W["w_proj_a"], W["w_proj_b"], W["w_out"], tm)
    d_ff = W["w_down"].shape[0]
    y, fbuf_new = _ffn(h, W["norm_ffn_g"], fbuf, W["w_up"], W["ffn_conv_w"], W["ffn_conv_b"],
                       W["w_down"], W["final_g"], n_seq, rows_per_seq,
                       min(FFN_ROW_TILE, rows_per_seq), d_ff // FFN_COLUMN_CHUNKS, final)
    return y, k, v, s_new, gbuf_new, fbuf_new


def kernel(x_prompt, x_sample, cache_k, cache_v, state_gdn, state_gdn_conv, state_ffn_conv, norm_mix_g, w_in, lambda_q1, lambda_k1, lambda_q2, lambda_k2, subln_g, gdn_conv_w, gdn_a_log, gdn_dt_bias, gdn_norm_g, w_proj_a, w_proj_b, w_out, norm_ffn_g, w_up, ffn_conv_w, ffn_conv_b, w_down, final_norm_g):
    P = dict(norm_mix_g=norm_mix_g, w_in=w_in, lambda_q1=lambda_q1, lambda_k1=lambda_k1,
             lambda_q2=lambda_q2, lambda_k2=lambda_k2, subln_g=subln_g, gdn_conv_w=gdn_conv_w,
             gdn_a_log=gdn_a_log, gdn_dt_bias=gdn_dt_bias, gdn_norm_g=gdn_norm_g,
             w_proj_a=w_proj_a, w_proj_b=w_proj_b, w_out=w_out, norm_ffn_g=norm_ffn_g,
             w_up=w_up, ffn_conv_w=ffn_conv_w, ffn_conv_b=ffn_conv_b, w_down=w_down,
             final_norm_g=final_norm_g)
    depth = w_in.shape[0]
    bp, tp, d = x_prompt.shape
    bs, ts, _ = x_sample.shape
    past = cache_k.shape[2]
    assert ts == CHUNK and past % CHUNK == 0 and tp % CHUNK == 0
    nqkv = state_gdn_conv.shape[-1]
    d_ff2 = state_ffn_conv.shape[-1]
    weights = [_prep_layer_weights(l, P) for l in range(depth)]
    ck = cache_k.reshape(depth, bs, past * A_HEADS, A_VDIM)
    cv = cache_v.reshape(depth, bs, past * A_HEADS, A_VDIM)

    tk = min(ATTN_KEY_TILE, tp)
    tq = min(ATTN_QUERY_BLOCK, tp)
    tm = 512

    def run(x3, nseq, rps, sample):
        assert sample or nseq == 1
        x = x3.reshape(nseq * rps, d)
        kv = tuple(jnp.zeros((depth, nseq * rps, A_HEADS * A_VDIM), F32) for _ in range(2))
        ss, gcs, fcs = [], [], []
        for l in range(depth):
            if sample:
                s0, gbuf, fbuf = state_gdn[l], state_gdn_conv[l], state_ffn_conv[l]
                c_k, c_v = ck, cv
            else:
                s0 = jnp.zeros((nseq, B_HEADS, B_DK, B_DV), F32)
                gbuf = jnp.zeros((nseq, GDN_CONV - 1, nqkv), F32)
                fbuf = jnp.zeros((nseq, FFN_CONV - 1, d_ff2), F32)
                c_k = c_v = None
            x, k_all, v_all, s_new, g_new, f_new = _layer(
                x, l, depth, kv, weights[l], nseq, rps, c_k, c_v, s0, gbuf, fbuf,
                l == depth - 1, tm, tq, tk)
            kv = (k_all, v_all)
            ss.append(s_new)
            gcs.append(g_new)
            fcs.append(f_new)
        kv_shape = (depth, nseq, rps, A_HEADS, A_VDIM)
        return (x.reshape(nseq, rps, d), kv[0].reshape(kv_shape), kv[1].reshape(kv_shape),
                jnp.stack(ss), jnp.stack(gcs), jnp.stack(fcs))

    y_p, k_p, v_p, s_p, gc_p, fc_p = run(x_prompt, bp, tp, False)
    y_s, k_s, v_s, s_s, gc_s, fc_s = run(x_sample, bs, ts, True)
    return (y_p, y_s, k_p, v_p, s_p, gc_p, fc_p, k_s, v_s, s_s, gc_s, fc_s)
```
